```python
import jax, jax.numpy as jnp
from jax import lax
import numpy as np

D_MODEL = 1024
BATCH = 8
SEQ = 2048
DEPTH = 4

CHUNK = 64
GLA_HEADS = 4
GLA_DK = 64
GLA_DV = 128
GLA_RANK = 16
GLA_TAU = 16.0
RET_HEADS = 4
RET_DK = 128
RET_DV = 128
ROPE_BASE = 10000.0
D_FF = 2816
N_EXPERTS = 8
TOP_K = 2
D_FF_EXPERT = 1408
EPS = 1e-6
GLA_QK = GLA_HEADS * GLA_DK
GLA_V = GLA_HEADS * GLA_DV
RET_QK = RET_HEADS * RET_DK
RET_V = RET_HEADS * RET_DV
MIX_WIDTH = GLA_V + RET_V
IN_SPLITS = (GLA_QK, GLA_QK, GLA_V, GLA_V, GLA_RANK, RET_QK, RET_QK, RET_V, RET_V)
IN_WIDTH = sum(IN_SPLITS)
N_DENSE = (DEPTH + 1) // 2
N_MOE = DEPTH // 2

kernel_name = 'hybrid_gla_retention_moe_adaln'

F32 = jnp.float32


def rms_norm(x, g):
    x32 = x.astype(F32)
    y = x32 * lax.rsqrt(jnp.mean(x32 * x32, axis=-1, keepdims=True) + EPS)
    return (y * g.astype(F32)).astype(x.dtype)


def to_chunks(t, n_heads):
    return t.reshape(t.shape[0], t.shape[1] // CHUNK, CHUNK, n_heads, -1)


def chunk_scan(decay, update):
    def step(state, inp):
        a, u = inp
        return a[..., None] * state + u, state
    init = jnp.zeros_like(update[:, 0])
    _, prev = lax.scan(step, init, (jnp.moveaxis(decay, 1, 0), jnp.moveaxis(update, 1, 0)))
    return jnp.moveaxis(prev, 0, 1)


def gla_core(q, k, v, log_a):
    b = jnp.cumsum(log_a, axis=2)
    b_last = b[:, :, -1:]
    eb, enb = jnp.exp(b), jnp.exp(-b)
    q_f, k_f = q * eb, k * enb
    q_b, k_b = q * enb, k * eb
    t_idx = jnp.arange(CHUNK)
    causal = t_idx[:, None] >= t_idx[None, :]
    s_fwd = jnp.einsum('bnthk,bnshk->bnhts', q_f, k_f)
    s_bwd = jnp.einsum('bnthk,bnshk->bnhts', q_b, k_b)
    scores = jnp.where(causal, s_fwd, s_bwd)
    o_intra = jnp.einsum('bnhts,bnshv->bnthv', scores, v)
    upd = jnp.einsum('bnshk,bnshv->bnhkv', k * jnp.exp(b_last - b), v)
    s_prev = chunk_scan(jnp.exp(b_last[:, :, 0]), upd)
    o_inter = jnp.einsum('bnthk,bnhkv->bnthv', q_f, s_prev)
    return o_intra + o_inter


def retention_core(q, k, v, log_gamma):
    t = jnp.arange(CHUNK, dtype=F32)
    dist = jnp.abs(t[:, None] - t[None, :])
    d_intra = jnp.exp(log_gamma[:, None, None] * dist)
    scores = jnp.einsum('bnthk,bnshk->bnhts', q, k) * d_intra
    o_intra = jnp.einsum('bnhts,bnshv->bnthv', scores, v)
    k_dec = k * jnp.exp(log_gamma[None, :] * (CHUNK - 1 - t)[:, None])[None, None, :, :, None]
    upd = jnp.einsum('bnshk,bnshv->bnhkv', k_dec, v)
    decay = jnp.broadcast_to(jnp.exp(log_gamma * CHUNK)[None, None, :, None], upd.shape[:-1])
    r_prev = chunk_scan(decay, upd)
    q_dec = q * jnp.exp(log_gamma[None, :] * (t + 1)[:, None])[None, None, :, :, None]
    o_inter = jnp.einsum('bnthk,bnhkv->bnthv', q_dec, r_prev)
    return o_intra + o_inter


def rotary(x, positions):
    half = x.shape[-1] // 2
    inv_freq = ROPE_BASE ** (-jnp.arange(half, dtype=F32) / half)
    ang = positions.astype(F32)[..., None] * inv_freq
    cos, sin = jnp.cos(ang)[:, :, None, :], jnp.sin(ang)[:, :, None, :]
    x1, x2 = x[..., :half], x[..., half:]
    return jnp.concatenate([x1 * cos - x2 * sin, x1 * sin + x2 * cos], axis=-1)


def hybrid_mixer(h, positions, w_in, w_alpha, b_alpha, gla_g, gn_g, gn_b, w_out):
    bsz, seq, _ = h.shape
    proj = (h @ w_in).astype(F32)
    gq, gk, gv, gr, ga, rq, rk, rv, rg = jnp.split(proj, np.cumsum(IN_SPLITS)[:-1], axis=-1)
    log_a = jax.nn.log_sigmoid(ga @ w_alpha.astype(F32) + b_alpha.astype(F32)) / GLA_TAU
    o_g = gla_core(to_chunks(gq * GLA_DK ** -0.5, GLA_HEADS), to_chunks(gk, GLA_HEADS),
                   to_chunks(gv, GLA_HEADS), to_chunks(log_a, GLA_HEADS))
    o_g = o_g.reshape(bsz, seq, GLA_HEADS, GLA_DV)
    o_g = o_g * lax.rsqrt(jnp.mean(o_g * o_g, axis=-1, keepdims=True) + EPS)
    o_g = o_g.reshape(bsz, seq, GLA_V) * gla_g.astype(F32) * jax.nn.silu(gr)
    log_gamma = jnp.log(1.0 - 2.0 ** (-5.0 - jnp.arange(RET_HEADS, dtype=F32)))
    rq = rotary(rq.reshape(bsz, seq, RET_HEADS, RET_DK), positions) * RET_DK ** -0.5
    rk = rotary(rk.reshape(bsz, seq, RET_HEADS, RET_DK), positions)
    o_r = retention_core(to_chunks(rq, RET_HEADS), to_chunks(rk, RET_HEADS),
                         to_chunks(rv, RET_HEADS), log_gamma)
    o_r = o_r.reshape(bsz, seq, RET_HEADS, RET_DV)
    mu = jnp.mean(o_r, axis=-1, keepdims=True)
    var = jnp.mean(jnp.square(o_r - mu), axis=-1, keepdims=True)
    o_r = ((o_r - mu) * lax.rsqrt(var + EPS)).reshape(bsz, seq, RET_V)
    o_r = (o_r * gn_g.astype(F32) + gn_b.astype(F32)) * jax.nn.silu(rg)
    o = jnp.concatenate([o_g, o_r], axis=-1).astype(h.dtype)
    return o @ w_out


def swiglu(h, w1, w3, w2):
    return (jax.nn.silu(h @ w1) * (h @ w3)) @ w2


def moe_swiglu(h, router_w, w1, w3, w2):
    logits = (h @ router_w).astype(F32)
    top_v, top_i = lax.top_k(logits, TOP_K)
    probs = jax.nn.softmax(top_v, axis=-1)
    gates = jnp.sum(jax.nn.one_hot(top_i, N_EXPERTS, dtype=F32) * probs[..., None], axis=-2)
    gates = gates.astype(h.dtype)
    out = jnp.zeros_like(h)
    for e in range(N_EXPERTS):
        out = out + gates[..., e:e + 1] * swiglu(h, w1[e], w3[e], w2[e])
    return out


def setup_inputs(seed: int = 0) -> dict:
    key = jax.random.key(seed)
    ks = jax.random.split(key, 24)
    nrm = lambda k, shape, scale: jax.random.normal(k, shape, F32) * scale
    D = D_MODEL
    offsets = jax.random.randint(ks[2], (BATCH, 1), 0, 4096, dtype=jnp.int32)
    return {
        'x': nrm(ks[0], (BATCH, SEQ, D), 1.0),
        'c': nrm(ks[1], (BATCH, D), 1.0),
        'positions': offsets + jnp.arange(SEQ, dtype=jnp.int32)[None, :],
        'ada_w': nrm(ks[3], (DEPTH, D, 6 * D), 0.5 * D ** -0.5),
        'ada_b': nrm(ks[4], (DEPTH, 6 * D), 0.02),
        'norm_mix_g': 1.0 + nrm(ks[5], (DEPTH, D), 0.02),
        'norm_ffn_g': 1.0 + nrm(ks[6], (DEPTH, D), 0.02),
        'w_in': nrm(ks[7], (DEPTH, D, IN_WIDTH), D ** -0.5),
        'gla_w_alpha': nrm(ks[8], (DEPTH, GLA_RANK, GLA_QK), GLA_RANK ** -0.5),
        'gla_b_alpha': nrm(ks[9], (DEPTH, GLA_QK), 0.1),
        'gla_norm_g': 1.0 + nrm(ks[10], (DEPTH, GLA_V), 0.02),
        'ret_gn_g': 1.0 + nrm(ks[11], (DEPTH, RET_V), 0.02),
        'ret_gn_b': nrm(ks[12], (DEPTH, RET_V), 0.02),
        'w_out': nrm(ks[13], (DEPTH, MIX_WIDTH, D), MIX_WIDTH ** -0.5),
        'ffn_w1': nrm(ks[14], (N_DENSE, D, D_FF), D ** -0.5),
        'ffn_w3': nrm(ks[15], (N_DENSE, D, D_FF), D ** -0.5),
        'ffn_w2': nrm(ks[16], (N_DENSE, D_FF, D), D_FF ** -0.5),
        'router_w': nrm(ks[17], (N_MOE, D, N_EXPERTS), D ** -0.5),
        'moe_w1': nrm(ks[18], (N_MOE, N_EXPERTS, D, D_FF_EXPERT), D ** -0.5),
        'moe_w3': nrm(ks[19], (N_MOE, N_EXPERTS, D, D_FF_EXPERT), D ** -0.5),
        'moe_w2': nrm(ks[20], (N_MOE, N_EXPERTS, D_FF_EXPERT, D), D_FF_EXPERT ** -0.5),
        'final_g': 1.0 + nrm(ks[21], (D,), 0.02),
    }


def reference(x, c, positions, ada_w, ada_b, norm_mix_g, norm_ffn_g, w_in, gla_w_alpha,
              gla_b_alpha, gla_norm_g, ret_gn_g, ret_gn_b, w_out, ffn_w1, ffn_w3, ffn_w2,
              router_w, moe_w1, moe_w3, moe_w2, final_g):
    cond = jax.nn.silu(c)
    for layer in range(DEPTH):
        mod = (cond @ ada_w[layer] + ada_b[layer])[:, None, :]
        sh1, sc1, g1, sh2, sc2, g2 = jnp.split(mod, 6, axis=-1)
        h = rms_norm(x, norm_mix_g[layer]) * (1.0 + sc1) + sh1
        x = x + g1 * hybrid_mixer(h, positions, w_in[layer], gla_w_alpha[layer],
                                  gla_b_alpha[layer], gla_norm_g[layer], ret_gn_g[layer],
                                  ret_gn_b[layer], w_out[layer])
        h = rms_norm(x, norm_ffn_g[layer]) * (1.0 + sc2) + sh2
        i = layer // 2
        if layer % 2 == 0:
            y = swiglu(h, ffn_w1[i], ffn_w3[i], ffn_w2[i])
        else:
            y = moe_swiglu(h, router_w[i], moe_w1[i], moe_w3[i], moe_w2[i])
        x = x + g2 * y
    return rms_norm(x, final_g)
```

```python
import functools

import numpy as np
import jax
import jax.numpy as jnp
from jax import lax
from jax.experimental import pallas as pl
from jax.experimental.pallas import tpu as pltpu

F32 = jnp.float32
BF16 = jnp.bfloat16

D_MODEL = 1024
DEPTH = 4
CHUNK = 64
GLA_HEADS = 4
GLA_DK = 64
GLA_DV = 128
GLA_RANK = 16
GLA_TAU = 16.0
RET_HEADS = 4
RET_DK = 128
RET_DV = 128
ROPE_BASE = 10000.0
N_EXPERTS = 8
EPS = 1e-6
GLA_QK = GLA_HEADS * GLA_DK
GLA_V = GLA_HEADS * GLA_DV
RET_QK = RET_HEADS * RET_DK
RET_V = RET_HEADS * RET_DV
MIX_WIDTH = GLA_V + RET_V

LANES = 128
VMEM_LIMIT_BYTES = 56 * 1024 * 1024

PROJ_WIDTH = 2 * GLA_QK + 2 * GLA_V + 2 * RET_QK + 2 * RET_V
OFF_GQ = 0
OFF_GK = OFF_GQ + GLA_QK
OFF_GV = OFF_GK + GLA_QK
OFF_GR = OFF_GV + GLA_V
OFF_RQ = OFF_GR + GLA_V
OFF_RK = OFF_RQ + RET_QK
OFF_RV = OFF_RK + RET_QK
OFF_RG = OFF_RV + RET_V

ATTN_ROWS = 256
ATTN_CHUNKS = ATTN_ROWS // CHUNK


def _sigmoid(x):
    return 1.0 / (1.0 + jnp.exp(-x))


def _rms_mod(x, g, sc, sh):
    ms = jnp.mean(x * x, axis=-1, keepdims=True)
    return (x * lax.rsqrt(ms + EPS)) * g * (1.0 + sc) + sh


def _dot(a, b):
    return jnp.dot(a, b, preferred_element_type=F32)


def _dot_nt(a, b):
    return lax.dot_general(a, b, (((1,), (1,)), ((), ())), preferred_element_type=F32)


def _dot_tn(a, b):
    return lax.dot_general(a, b, (((0,), (0,)), ((), ())), preferred_element_type=F32)


def _ada_kernel(c_ref, w_ref, b_ref, o_ref):
    c = c_ref[...]
    cond = c * _sigmoid(c)
    o_ref[0] = jnp.dot(cond, w_ref[0], precision=lax.Precision.HIGHEST,
                       preferred_element_type=F32) + b_ref[0]


def _ada_mod(c, ada_w, ada_b):
    bsz = c.shape[0]
    tn = 1024
    n_out = ada_w.shape[-1]
    return pl.pallas_call(
        _ada_kernel,
        grid=(DEPTH, n_out // tn),
        in_specs=[
            pl.BlockSpec((bsz, D_MODEL), lambda l, j: (0, 0)),
            pl.BlockSpec((1, D_MODEL, tn), lambda l, j: (l, 0, j)),
            pl.BlockSpec((1, 1, tn), lambda l, j: (l, 0, j)),
        ],
        out_specs=pl.BlockSpec((1, bsz, tn), lambda l, j: (l, 0, j)),
        out_shape=jax.ShapeDtypeStruct((DEPTH, bsz, n_out), F32),
        name="ada_mod",
    )(c, ada_w, ada_b.reshape(DEPTH, 1, n_out))


def _rope_kernel(pos_ref, invf_ref, sign_ref, cos_ref, sin_ref):
    ang = pos_ref[...] * invf_ref[...]
    cos_ref[...] = jnp.cos(ang)
    sin_ref[...] = jnp.sin(ang) * sign_ref[...]


def _rope_tables(positions):
    n = positions.size
    half = RET_DK // 2
    inv_freq = ROPE_BASE ** (-jnp.arange(half, dtype=F32) / half)
    invf = jnp.concatenate([inv_freq, inv_freq]).reshape(1, RET_DK)
    sign = jnp.concatenate([-jnp.ones((half,), F32), jnp.ones((half,), F32)]).reshape(1, RET_DK)
    pos = positions.astype(F32).reshape(n, 1)
    tm = 2048
    row = pl.BlockSpec((1, RET_DK), lambda i: (0, 0))
    out = pl.BlockSpec((tm, RET_DK), lambda i: (i, 0))
    return pl.pallas_call(
        _rope_kernel,
        grid=(n // tm,),
        in_specs=[pl.BlockSpec((tm, 1), lambda i: (i, 0)), row, row],
        out_specs=[out, out],
        out_shape=[jax.ShapeDtypeStruct((n, RET_DK), F32)] * 2,
        name="rope_tables",
    )(pos, invf, sign)


def _inproj_kernel(x_ref, mod_ref, g_ref, w_ref, wga_ref, proj_ref, ga_ref):
    m = mod_ref[0]
    h = _rms_mod(x_ref[0], g_ref[...], m[1:2], m[0:1]).astype(BF16)
    proj_ref[0] = _dot(h, w_ref[...])
    ga_ref[0] = _dot(h, wga_ref[...])


def _in_proj(x, mod, g, w_main, w_ga):
    bsz, seq, _ = x.shape
    tm = 512
    return pl.pallas_call(
        _inproj_kernel,
        grid=(bsz, seq // tm),
        in_specs=[
            pl.BlockSpec((1, tm, D_MODEL), lambda b, i: (b, i, 0)),
            pl.BlockSpec((1, 6, D_MODEL), lambda b, i: (b, 0, 0)),
            pl.BlockSpec((1, D_MODEL), lambda b, i: (0, 0)),
            pl.BlockSpec((D_MODEL, PROJ_WIDTH), lambda b, i: (0, 0)),
            pl.BlockSpec((D_MODEL, LANES), lambda b, i: (0, 0)),
        ],
        out_specs=[
            pl.BlockSpec((1, tm, PROJ_WIDTH), lambda b, i: (b, i, 0)),
            pl.BlockSpec((1, tm, LANES), lambda b, i: (b, i, 0)),
        ],
        out_shape=[
            jax.ShapeDtypeStruct((bsz, seq, PROJ_WIDTH), F32),
            jax.ShapeDtypeStruct((bsz, seq, LANES), F32),
        ],
        compiler_params=pltpu.CompilerParams(
            dimension_semantics=("arbitrary", "arbitrary"),
            vmem_limit_bytes=VMEM_LIMIT_BYTES),
        name="in_proj",
    )(x, mod, g, w_main, w_ga)


def _attn_consts():
    r = ATTN_ROWS
    t = np.arange(r)
    same = (t[:, None] // CHUNK) == (t[None, :] // CHUNK)
    causal = t[:, None] >= t[None, :]
    tri = (same & causal).astype(np.float32)
    m_fwd = tri
    m_bwd = (same & ~causal).astype(np.float32)
    gam = 1.0 - 2.0 ** (-5.0 - np.arange(RET_HEADS, dtype=np.float64))
    lg = np.log(gam)
    dist = (t[:, None] - t[None, :]).astype(np.float64)
    d_ret = np.where(causal[None], np.exp(lg[:, None, None] * dist[None]),
                     np.where(same[None], np.exp(-lg[:, None, None] * dist[None]), 0.0))
    qdec = np.repeat(np.exp(lg[None, :] * (t[:, None] + 1.0)), RET_DK, axis=1)
    kdec = np.repeat(np.exp(lg[None, :] * (r - 1.0 - t[:, None])), RET_DK, axis=1)
    step_decay = [float(np.exp(lg[h] * r)) for h in range(RET_HEADS)]
    hmask = np.zeros((GLA_HEADS, 1, GLA_QK), np.float32)
    for h in range(GLA_HEADS):
        hmask[h, 0, h * GLA_DK:(h + 1) * GLA_DK] = 1.0
    vv = np.arange(GLA_V)[:, None] // GLA_DV
    kk = np.arange(GLA_QK)[None, :] // GLA_DK
    bdiag = (vv == kk).astype(np.float32)
    return dict(tri=tri, m_fwd=m_fwd, m_bwd=m_bwd, d_ret=d_ret.astype(np.float32),
                qdec=qdec.astype(np.float32), kdec=kdec.astype(np.float32),
                step_decay=step_decay, hmask=hmask, bdiag=bdiag)


def _attn_kernel(step_decay, proj_ref, ga_ref, cos_ref, sin_ref, walpha_ref, balpha_ref,
                 glag_ref, gng_ref, gnb_ref, tri_ref, mfwd_ref, mbwd_ref, dret_ref,
                 qdec_ref, kdec_ref, hmask_ref, bdiag_ref, o_ref, st_g, st_r):
    r = ATTN_ROWS

    @pl.when(pl.program_id(1) == 0)
    def _():
        st_g[...] = jnp.zeros_like(st_g)
        st_r[...] = jnp.zeros_like(st_r)

    z = _dot(ga_ref[0].astype(BF16), walpha_ref[...]) + balpha_ref[...]
    log_a = (jnp.minimum(z, 0.0) - jnp.log(1.0 + jnp.exp(-jnp.abs(z)))) * (1.0 / GLA_TAU)
    hi = log_a.astype(BF16)
    r1 = log_a - hi.astype(F32)
    mid = r1.astype(BF16)
    lo = (r1 - mid.astype(F32)).astype(BF16)
    tri = tri_ref[...]
    b = _dot(tri, hi) + _dot(tri, mid) + _dot(tri, lo)
    b3 = b.reshape(ATTN_CHUNKS, CHUNK, GLA_QK)
    b_last = b3[:, CHUNK - 1:CHUNK, :]
    k_upd_scale = jnp.exp(b_last - b3).reshape(r, GLA_QK)
    eb = jnp.exp(b)
    enb = jnp.exp(-b)
    q = proj_ref[0, :, OFF_GQ:OFF_GQ + GLA_QK] * (GLA_DK ** -0.5)
    k = proj_ref[0, :, OFF_GK:OFF_GK + GLA_QK]
    v_bf = proj_ref[0, :, OFF_GV:OFF_GV + GLA_V].astype(BF16)
    q_f = q * eb
    q_b = q * enb
    k_f = (k * enb).astype(BF16)
    k_b = (k * eb).astype(BF16)
    k_u = (k * k_upd_scale).astype(BF16)
    q_f_bf = q_f.astype(BF16)

    bdiag = bdiag_ref[...]
    inter = []
    for g in range(ATTN_CHUNKS):
        rows = slice(g * CHUNK, (g + 1) * CHUNK)
        st = st_g[...]
        inter.append(_dot_nt(q_f_bf[rows], st.astype(BF16)))
        upd_t = _dot_tn(v_bf[rows], k_u[rows]) * bdiag
        st_g[...] = st * jnp.exp(b_last[g]) + upd_t
    o_inter = jnp.concatenate(inter, axis=0)

    m_fwd = mfwd_ref[...] > 0.5
    m_bwd = mbwd_ref[...] > 0.5
    for h in range(GLA_HEADS):
        hm = hmask_ref[h]
        s_f = _dot_nt((q_f * hm).astype(BF16), k_f)
        s_b = _dot_nt((q_b * hm).astype(BF16), k_b)
        sc = jnp.where(m_fwd, s_f, jnp.where(m_bwd, s_b, 0.0)).astype(BF16)
        lanes = slice(h * GLA_DV, (h + 1) * GLA_DV)
        o_h = _dot(sc, v_bf[:, lanes]) + o_inter[:, lanes]
        o_h = o_h * lax.rsqrt(jnp.mean(o_h * o_h, axis=-1, keepdims=True) + EPS)
        gate = proj_ref[0, :, OFF_GR + h * GLA_DV:OFF_GR + (h + 1) * GLA_DV]
        o_h = o_h * glag_ref[:, lanes] * (gate * _sigmoid(gate))
        o_ref[0, :, lanes] = o_h.astype(o_ref.dtype)

    cos = cos_ref[0]
    sin = sin_ref[0]
    for h in range(RET_HEADS):
        lanes = slice(h * RET_DK, (h + 1) * RET_DK)
        qh = proj_ref[0, :, OFF_RQ + h * RET_DK:OFF_RQ + (h + 1) * RET_DK]
        kh = proj_ref[0, :, OFF_RK + h * RET_DK:OFF_RK + (h + 1) * RET_DK]
        vh = proj_ref[0, :, OFF_RV + h * RET_DV:OFF_RV + (h + 1) * RET_DV].astype(BF16)
        qh = (qh * cos + pltpu.roll(qh, RET_DK // 2, axis=1) * sin) * (RET_DK ** -0.5)
        kh = kh * cos + pltpu.roll(kh, RET_DK // 2, axis=1) * sin
        s = _dot_nt(qh.astype(BF16), kh.astype(BF16)) * dret_ref[h]
        st = st_r[h]
        o_h = _dot(s.astype(BF16), vh) + _dot((qh * qdec_ref[:, lanes]).astype(BF16), st.astype(BF16))
        st_r[h] = st * step_decay[h] + _dot_tn((kh * kdec_ref[:, lanes]).astype(BF16), vh)
        mu = jnp.mean(o_h, axis=-1, keepdims=True)
        d = o_h - mu
        var = jnp.mean(d * d, axis=-1, keepdims=True)
        o_h = d * lax.rsqrt(var + EPS) * gng_ref[:, lanes] + gnb_ref[:, lanes]
        gate = proj_ref[0, :, OFF_RG + h * RET_DV:OFF_RG + (h + 1) * RET_DV]
        o_h = o_h * (gate * _sigmoid(gate))
        o_ref[0, :, GLA_V + h * RET_DV:GLA_V + (h + 1) * RET_DV] = o_h.astype(o_ref.dtype)


def _attention(proj, ga, cos, sin, w_alpha, b_alpha, gla_g, gn_g, gn_b):
    bsz, seq, _ = proj.shape
    r = ATTN_ROWS
    c = _attn_consts()
    w_alpha_p = jnp.zeros((LANES, GLA_QK), F32).at[:GLA_RANK].set(w_alpha).astype(BF16)
    const2 = lambda shape: pl.BlockSpec(shape, lambda b, i: (0,) * len(shape))
    row_blk = lambda w: pl.BlockSpec((1, r, w), lambda b, i: (b, i, 0))
    return pl.pallas_call(
        functools.partial(_attn_kernel, c["step_decay"]),
        grid=(bsz, seq // r),
        in_specs=[
            row_blk(PROJ_WIDTH), row_blk(LANES), row_blk(RET_DK), row_blk(RET_DK),
            const2((LANES, GLA_QK)), const2((1, GLA_QK)),
            const2((1, GLA_V)), const2((1, RET_V)), const2((1, RET_V)),
            const2((r, r)), const2((r, r)), const2((r, r)), const2((RET_HEADS, r, r)),
            const2((r, RET_QK)), const2((r, RET_QK)),
            const2((GLA_HEADS, 1, GLA_QK)), const2((GLA_V, GLA_QK)),
        ],
        out_specs=row_blk(MIX_WIDTH),
        out_shape=jax.ShapeDtypeStruct((bsz, seq, MIX_WIDTH), BF16),
        scratch_shapes=[pltpu.VMEM((GLA_V, GLA_QK), F32),
                        pltpu.VMEM((RET_HEADS, RET_DK, RET_DV), F32)],
        compiler_params=pltpu.CompilerParams(
            dimension_semantics=("arbitrary", "arbitrary"),
            vmem_limit_bytes=VMEM_LIMIT_BYTES),
        name="mixer_core",
    )(proj, ga, cos.reshape(bsz, seq, RET_DK), sin.reshape(bsz, seq, RET_DK),
      w_alpha_p, b_alpha.reshape(1, GLA_QK), gla_g.reshape(1, GLA_V),
      gn_g.reshape(1, RET_V), gn_b.reshape(1, RET_V),
      jnp.asarray(c["tri"], BF16), jnp.asarray(c["m_fwd"]), jnp.asarray(c["m_bwd"]),
      jnp.asarray(c["d_ret"]), jnp.asarray(c["qdec"]), jnp.asarray(c["kdec"]),
      jnp.asarray(c["hmask"]), jnp.asarray(c["bdiag"]))


def _outproj_kernel(o_ref, x_ref, mod_ref, w_ref, y_ref):
    m = mod_ref[0]
    y_ref[0] = x_ref[0] + m[2:3] * _dot(o_ref[0], w_ref[...])


def _out_proj(o, x, mod, w_out):
    bsz, seq, _ = x.shape
    tm = 512
    row = lambda w: pl.BlockSpec((1, tm, w), lambda b, i: (b, i, 0))
    return pl.pallas_call(
        _outproj_kernel,
        grid=(bsz, seq // tm),
        in_specs=[row(MIX_WIDTH), row(D_MODEL),
                  pl.BlockSpec((1, 6, D_MODEL), lambda b, i: (b, 0, 0)),
                  pl.BlockSpec((MIX_WIDTH, D_MODEL), lambda b, i: (0, 0))],
        out_specs=row(D_MODEL),
        out_shape=jax.ShapeDtypeStruct(x.shape, F32),
        compiler_params=pltpu.CompilerParams(
            dimension_semantics=("arbitrary", "arbitrary"),
            vmem_limit_bytes=VMEM_LIMIT_BYTES),
        name="out_proj",
    )(o, x, mod, w_out)


def _router_kernel(x_ref, mod_ref, g_ref, rw_ref, gates_ref):
    m = mod_ref[0]
    h = _rms_mod(x_ref[0], g_ref[...], m[4:5], m[3:4])
    logits = jnp.dot(h, rw_ref[...], precision=lax.Precision.HIGHEST,
                     preferred_element_type=F32)
    lane = lax.broadcasted_iota(jnp.int32, logits.shape, 1)
    neg = jnp.float32(-jnp.inf)
    lg = jnp.where(lane < N_EXPERTS, logits, neg)
    m1 = jnp.max(lg, axis=-1, keepdims=True)
    i1 = jnp.min(jnp.where(lg == m1, lane, LANES), axis=-1, keepdims=True)
    lg2 = jnp.where(lane == i1, neg, lg)
    m2 = jnp.max(lg2, axis=-1, keepdims=True)
    i2 = jnp.min(jnp.where(lg2 == m2, lane, LANES), axis=-1, keepdims=True)
    e2 = jnp.exp(m2 - m1)
    p1 = 1.0 / (1.0 + e2)
    p2 = e2 * p1
    gates_ref[0] = jnp.where(lane == i1, p1, 0.0) + jnp.where(lane == i2, p2, 0.0)


def _router(x, mod, g, router_w):
    bsz, seq, _ = x.shape
    tm = 512
    rw = jnp.zeros((D_MODEL, LANES), F32).at[:, :N_EXPERTS].set(router_w)
    return pl.pallas_call(
        _router_kernel,
        grid=(bsz, seq // tm),
        in_specs=[pl.BlockSpec((1, tm, D_MODEL), lambda b, i: (b, i, 0)),
                  pl.BlockSpec((1, 6, D_MODEL), lambda b, i: (b, 0, 0)),
                  pl.BlockSpec((1, D_MODEL), lambda b, i: (0, 0)),
                  pl.BlockSpec((D_MODEL, LANES), lambda b, i: (0, 0))],
        out_specs=pl.BlockSpec((1, tm, LANES), lambda b, i: (b, i, 0)),
        out_shape=jax.ShapeDtypeStruct((bsz, seq, LANES), F32),
        name="router",
    )(x, mod, g, rw)


def _ffn_kernel(use_gates, final_norm, n_groups, *refs):
    if use_gates:
        x_ref, mod_ref, g_ref, gates_ref, w1_ref, w3_ref, w2_ref, fg_ref, y_ref, h_scr, acc_scr = refs
    else:
        x_ref, mod_ref, g_ref, w1_ref, w3_ref, w2_ref, fg_ref, y_ref, h_scr, acc_scr = refs
    e = pl.program_id(2)
    m = mod_ref[0]

    @pl.when(e == 0)
    def _():
        h_scr[...] = _rms_mod(x_ref[0], g_ref[...], m[4:5], m[3:4]).astype(BF16)
        acc_scr[...] = jnp.zeros_like(acc_scr)

    h = h_scr[...]
    a = _dot(h, w1_ref[0])
    u = (a * _sigmoid(a) * _dot(h, w3_ref[0])).astype(BF16)
    y = _dot(u, w2_ref[0])
    if use_gates:
        gates = gates_ref[0]
        lane = lax.broadcasted_iota(jnp.int32, gates.shape, 1)
        y = y * jnp.sum(jnp.where(lane == e, gates, 0.0), axis=-1, keepdims=True)
    acc_scr[...] += y

    @pl.when(e == n_groups - 1)
    def _():
        out = x_ref[0] + m[5:6] * acc_scr[...]
        if final_norm:
            ms = jnp.mean(out * out, axis=-1, keepdims=True)
            out = out * lax.rsqrt(ms + EPS) * fg_ref[...]
        y_ref[0] = out


def _ffn(x, mod, g, w1, w3, w2, gates, final_g, final_norm):
    bsz, seq, _ = x.shape
    n_groups, _, f = w1.shape
    tm = 1024
    use_gates = gates is not None
    row = lambda w: pl.BlockSpec((1, tm, w), lambda b, i, e: (b, i, 0))
    in_specs = [row(D_MODEL),
                pl.BlockSpec((1, 6, D_MODEL), lambda b, i, e: (b, 0, 0)),
                pl.BlockSpec((1, D_MODEL), lambda b, i, e: (0, 0))]
    args = [x, mod, g]
    if use_gates:
        in_specs.append(row(LANES))
        args.append(gates)
    in_specs += [pl.BlockSpec((1, D_MODEL, f), lambda b, i, e: (e, 0, 0)),
                 pl.BlockSpec((1, D_MODEL, f), lambda b, i, e: (e, 0, 0)),
                 pl.BlockSpec((1, f, D_MODEL), lambda b, i, e: (e, 0, 0)),
                 pl.BlockSpec((1, D_MODEL), lambda b, i, e: (0, 0))]
    args += [w1, w3, w2, final_g]
    return pl.pallas_call(
        functools.partial(_ffn_kernel, use_gates, final_norm, n_groups),
        grid=(bsz, seq // tm, n_groups),
        in_specs=in_specs,
        out_specs=row(D_MODEL),
        out_shape=jax.ShapeDtypeStruct(x.shape, F32),
        scratch_shapes=[pltpu.VMEM((tm, D_MODEL), BF16), pltpu.VMEM((tm, D_MODEL), F32)],
        compiler_params=pltpu.CompilerParams(
            dimension_semantics=("arbitrary", "arbitrary", "arbitrary"),
            vmem_limit_bytes=VMEM_LIMIT_BYTES),
        name="ffn",
    )(*args)


def kernel(x, c, positions, ada_w, ada_b, norm_mix_g, norm_ffn_g, w_in, gla_w_alpha, gla_b_alpha,
           gla_norm_g, ret_gn_g, ret_gn_b, w_out, ffn_w1, ffn_w3, ffn_w2, router_w, moe_w1,
           moe_w3, moe_w2, final_g):
    bsz = x.shape[0]
    mod_all = _ada_mod(c, ada_w, ada_b).reshape(DEPTH, bsz, 6, D_MODEL)
    cos, sin = _rope_tables(positions)
    ga_lo = 2 * GLA_QK + 2 * GLA_V
    fg = final_g.reshape(1, D_MODEL)
    f_half = ffn_w1.shape[-1] // 2
    for layer in range(DEPTH):
        mod = mod_all[layer]
        w = w_in[layer]
        w_main = jnp.concatenate([w[:, :ga_lo], w[:, ga_lo + GLA_RANK:]], axis=1).astype(BF16)
        w_ga = jnp.zeros((D_MODEL, LANES), F32).at[:, :GLA_RANK].set(
            w[:, ga_lo:ga_lo + GLA_RANK]).astype(BF16)
        proj, ga = _in_proj(x, mod, norm_mix_g[layer].reshape(1, D_MODEL), w_main, w_ga)
        o = _attention(proj, ga, cos, sin, gla_w_alpha[layer], gla_b_alpha[layer],
                       gla_norm_g[layer], ret_gn_g[layer], ret_gn_b[layer])
        x = _out_proj(o, x, mod, w_out[layer].astype(BF16))
        g_ffn = norm_ffn_g[layer].reshape(1, D_MODEL)
        i = layer // 2
        last = layer == DEPTH - 1
        if layer % 2 == 0:
            w1 = ffn_w1[i].reshape(D_MODEL, 2, f_half).transpose(1, 0, 2).astype(BF16)
            w3 = ffn_w3[i].reshape(D_MODEL, 2, f_half).transpose(1, 0, 2).astype(BF16)
            w2 = ffn_w2[i].reshape(2, f_half, D_MODEL).astype(BF16)
            x = _ffn(x, mod, g_ffn, w1, w3, w2, None, fg, last)
        else:
            gates = _router(x, mod, g_ffn, router_w[i])
            x = _ffn(x, mod, g_ffn, moe_w1[i].astype(BF16), moe_w3[i].astype(BF16),
                     moe_w2[i].astype(BF16), gates, fg, last)
    return x
```

```python
import functools

import numpy as np
import jax
import jax.numpy as jnp
from jax import lax
from jax.experimental import pallas as pl
from jax.experimental.pallas import tpu as pltpu

F32 = jnp.float32
BF16 = jnp.bfloat16

D_MODEL = 1024
DEPTH = 4
CHUNK = 64
GLA_HEADS = 4
GLA_DK = 64
GLA_DV = 128
GLA_RANK = 16
GLA_TAU = 16.0
RET_HEADS = 4
RET_DK = 128
RET_DV = 128
ROPE_BASE = 10000.0
N_EXPERTS = 8
EPS = 1e-6
GLA_QK = GLA_HEADS * GLA_DK
GLA_V = GLA_HEADS * GLA_DV
RET_QK = RET_HEADS * RET_DK
RET_V = RET_HEADS * RET_DV
MIX_WIDTH = GLA_V + RET_V

LANES = 128
VMEM_LIMIT_BYTES = 56 * 1024 * 1024

PROJ_WIDTH = 2 * GLA_QK + 2 * GLA_V + 2 * RET_QK + 2 * RET_V
OFF_GQ = 0
OFF_GK = OFF_GQ + GLA_QK
OFF_GV = OFF_GK + GLA_QK
OFF_GR = OFF_GV + GLA_V
OFF_RQ = OFF_GR + GLA_V
OFF_RK = OFF_RQ + RET_QK
OFF_RV = OFF_RK + RET_QK
OFF_RG = OFF_RV + RET_V

ATTN_ROWS = 256
ATTN_CHUNKS = ATTN_ROWS // CHUNK


def _sigmoid(x):
    return 1.0 / (1.0 + jnp.exp(-x))


def _rms_mod(x, g, sc, sh):
    ms = jnp.mean(x * x, axis=-1, keepdims=True)
    return (x * lax.rsqrt(ms + EPS)) * g * (1.0 + sc) + sh


def _dot(a, b):
    return jnp.dot(a, b, preferred_element_type=F32)


def _dot_nt(a, b):
    return lax.dot_general(a, b, (((1,), (1,)), ((), ())), preferred_element_type=F32)


def _dot_tn(a, b):
    return lax.dot_general(a, b, (((0,), (0,)), ((), ())), preferred_element_type=F32)


def _ada_kernel(c_ref, w_ref, b_ref, o_ref):
    c = c_ref[...]
    cond = c * _sigmoid(c)
    o_ref[0] = jnp.dot(cond, w_ref[0], precision=lax.Precision.HIGHEST,
                       preferred_element_type=F32) + b_ref[0]


def _ada_mod(c, ada_w, ada_b):
    bsz = c.shape[0]
    tn = 1024
    n_out = ada_w.shape[-1]
    return pl.pallas_call(
        _ada_kernel,
        grid=(DEPTH, n_out // tn),
        in_specs=[
            pl.BlockSpec((bsz, D_MODEL), lambda l, j: (0, 0)),
            pl.BlockSpec((1, D_MODEL, tn), lambda l, j: (l, 0, j)),
            pl.BlockSpec((1, 1, tn), lambda l, j: (l, 0, j)),
        ],
        out_specs=pl.BlockSpec((1, bsz, tn), lambda l, j: (l, 0, j)),
        out_shape=jax.ShapeDtypeStruct((DEPTH, bsz, n_out), F32),
        name="ada_mod",
    )(c, ada_w, ada_b.reshape(DEPTH, 1, n_out))


def _rope_kernel(pos_ref, invf_ref, sign_ref, cos_ref, sin_ref):
    ang = pos_ref[...] * invf_ref[...]
    cos_ref[...] = jnp.cos(ang)
    sin_ref[...] = jnp.sin(ang) * sign_ref[...]


def _rope_tables(positions):
    n = positions.size
    half = RET_DK // 2
    inv_freq = ROPE_BASE ** (-jnp.arange(half, dtype=F32) / half)
    invf = jnp.concatenate([inv_freq, inv_freq]).reshape(1, RET_DK)
    sign = jnp.concatenate([-jnp.ones((half,), F32), jnp.ones((half,), F32)]).reshape(1, RET_DK)
    pos = positions.astype(F32).reshape(n, 1)
    tm = 2048
    row = pl.BlockSpec((1, RET_DK), lambda i: (0, 0))
    out = pl.BlockSpec((tm, RET_DK), lambda i: (i, 0))
    return pl.pallas_call(
        _rope_kernel,
        grid=(n // tm,),
        in_specs=[pl.BlockSpec((tm, 1), lambda i: (i, 0)), row, row],
        out_specs=[out, out],
        out_shape=[jax.ShapeDtypeStruct((n, RET_DK), F32)] * 2,
        name="rope_tables",
    )(pos, invf, sign)


def _inproj_kernel(x_ref, mod_ref, g_ref, w_ref, wga_ref, proj_ref, ga_ref):
    m = mod_ref[0]
    h = _rms_mod(x_ref[0], g_ref[...], m[1:2], m[0:1]).astype(BF16)
    proj_ref[0] = _dot(h, w_ref[...])
    ga_ref[0] = _dot(h, wga_ref[...])


def _in_proj(x, mod, g, w_main, w_ga):
    bsz, seq, _ = x.shape
    tm = 512
    return pl.pallas_call(
        _inproj_kernel,
        grid=(bsz, seq // tm),
        in_specs=[
            pl.BlockSpec((1, tm, D_MODEL), lambda b, i: (b, i, 0)),
            pl.BlockSpec((1, 6, D_MODEL), lambda b, i: (b, 0, 0)),
            pl.BlockSpec((1, D_MODEL), lambda b, i: (0, 0)),
            pl.BlockSpec((D_MODEL, PROJ_WIDTH), lambda b, i: (0, 0)),
            pl.BlockSpec((D_MODEL, LANES), lambda b, i: (0, 0)),
        ],
        out_specs=[
            pl.BlockSpec((1, tm, PROJ_WIDTH), lambda b, i: (b, i, 0)),
            pl.BlockSpec((1, tm, LANES), lambda b, i: (b, i, 0)),
        ],
        out_shape=[
            jax.ShapeDtypeStruct((bsz, seq, PROJ_WIDTH), F32),
            jax.ShapeDtypeStruct((bsz, seq, LANES), F32),
        ],
        compiler_params=pltpu.CompilerParams(
            dimension_semantics=("arbitrary", "arbitrary"),
            vmem_limit_bytes=VMEM_LIMIT_BYTES),
        name="in_proj",
    )(x, mod, g, w_main, w_ga)


def _attn_consts():
    r = ATTN_ROWS
    t = np.arange(r)
    same = (t[:, None] // CHUNK) == (t[None, :] // CHUNK)
    causal = t[:, None] >= t[None, :]
    tri = (same & causal).astype(np.float32)
    m_fwd = tri
    m_bwd = (same & ~causal).astype(np.float32)
    gam = 1.0 - 2.0 ** (-5.0 - np.arange(RET_HEADS, dtype=np.float64))
    lg = np.log(gam)
    dist = (t[:, None] - t[None, :]).astype(np.float64)
    d_ret = np.where(causal[None], np.exp(lg[:, None, None] * dist[None]),
                     np.where(same[None], np.exp(-lg[:, None, None] * dist[None]), 0.0))
    qdec = np.repeat(np.exp(lg[None, :] * (t[:, None] + 1.0)), RET_DK, axis=1)
    kdec = np.repeat(np.exp(lg[None, :] * (r - 1.0 - t[:, None])), RET_DK, axis=1)
    step_decay = [float(np.exp(lg[h] * r)) for h in range(RET_HEADS)]
    hmask = np.zeros((GLA_HEADS, 1, GLA_QK), np.float32)
    for h in range(GLA_HEADS):
        hmask[h, 0, h * GLA_DK:(h + 1) * GLA_DK] = 1.0
    vv = np.arange(GLA_V)[:, None] // GLA_DV
    kk = np.arange(GLA_QK)[None, :] // GLA_DK
    bdiag = (vv == kk).astype(np.float32)
    return dict(tri=tri, m_fwd=m_fwd, m_bwd=m_bwd, d_ret=d_ret.astype(np.float32),
                qdec=qdec.astype(np.float32), kdec=kdec.astype(np.float32),
                step_decay=step_decay, hmask=hmask, bdiag=bdiag)


def _attn_kernel(step_decay, proj_ref, ga_ref, cos_ref, sin_ref, walpha_ref, balpha_ref,
                 glag_ref, gng_ref, gnb_ref, tri_ref, mfwd_ref, mbwd_ref, dret_ref,
                 qdec_ref, kdec_ref, hmask_ref, bdiag_ref, o_ref, st_g, st_r):
    r = ATTN_ROWS

    @pl.when(pl.program_id(1) == 0)
    def _():
        st_g[...] = jnp.zeros_like(st_g)
        st_r[...] = jnp.zeros_like(st_r)

    z = _dot(ga_ref[0].astype(BF16), walpha_ref[...]) + balpha_ref[...]
    log_a = (jnp.minimum(z, 0.0) - jnp.log(1.0 + jnp.exp(-jnp.abs(z)))) * (1.0 / GLA_TAU)
    hi = log_a.astype(BF16)
    r1 = log_a - hi.astype(F32)
    mid = r1.astype(BF16)
    lo = (r1 - mid.astype(F32)).astype(BF16)
    tri = tri_ref[...]
    b = _dot(tri, hi) + _dot(tri, mid) + _dot(tri, lo)
    b3 = b.reshape(ATTN_CHUNKS, CHUNK, GLA_QK)
    b_last = b3[:, CHUNK - 1:CHUNK, :]
    k_upd_scale = jnp.exp(b_last - b3).reshape(r, GLA_QK)
    eb = jnp.exp(b)
    enb = jnp.exp(-b)
    q = proj_ref[0, :, OFF_GQ:OFF_GQ + GLA_QK] * (GLA_DK ** -0.5)
    k = proj_ref[0, :, OFF_GK:OFF_GK + GLA_QK]
    v_bf = proj_ref[0, :, OFF_GV:OFF_GV + GLA_V].astype(BF16)
    q_f = q * eb
    q_b = q * enb
    k_f = (k * enb).astype(BF16)
    k_b = (k * eb).astype(BF16)
    k_u = (k * k_upd_scale).astype(BF16)
    q_f_bf = q_f.astype(BF16)

    bdiag = bdiag_ref[...]
    inter = []
    for g in range(ATTN_CHUNKS):
        rows = slice(g * CHUNK, (g + 1) * CHUNK)
        st = st_g[...]
        inter.append(_dot_nt(q_f_bf[rows], st.astype(BF16)))
        upd_t = _dot_tn(v_bf[rows], k_u[rows]) * bdiag
        st_g[...] = st * jnp.exp(b_last[g]) + upd_t
    o_inter = jnp.concatenate(inter, axis=0)

    m_fwd = mfwd_ref[...] > 0.5
    m_bwd = mbwd_ref[...] > 0.5
    for h in range(GLA_HEADS):
        hm = hmask_ref[h]
        s_f = _dot_nt((q_f * hm).astype(BF16), k_f)
        s_b = _dot_nt((q_b * hm).astype(BF16), k_b)
        sc = jnp.where(m_fwd, s_f, jnp.where(m_bwd, s_b, 0.0)).astype(BF16)
        lanes = slice(h * GLA_DV, (h + 1) * GLA_DV)
        o_h = _dot(sc, v_bf[:, lanes]) + o_inter[:, lanes]
        o_h = o_h * lax.rsqrt(jnp.mean(o_h * o_h, axis=-1, keepdims=True) + EPS)
        gate = proj_ref[0, :, OFF_GR + h * GLA_DV:OFF_GR + (h + 1) * GLA_DV]
        o_h = o_h * glag_ref[:, lanes] * (gate * _sigmoid(gate))
        o_ref[0, :, lanes] = o_h.astype(o_ref.dtype)

    cos = cos_ref[0]
    sin = sin_ref[0]
    for h in range(RET_HEADS):
        lanes = slice(h * RET_DK, (h + 1) * RET_DK)
        qh = proj_ref[0, :, OFF_RQ + h * RET_DK:OFF_RQ + (h + 1) * RET_DK]
        kh = proj_ref[0, :, OFF_RK + h * RET_DK:OFF_RK + (h + 1) * RET_DK]
        vh = proj_ref[0, :, OFF_RV + h * RET_DV:OFF_RV + (h + 1) * RET_DV].astype(BF16)
        qh = (qh * cos + pltpu.roll(qh, RET_DK // 2, axis=1) * sin) * (RET_DK ** -0.5)
        kh = kh * cos + pltpu.roll(kh, RET_DK // 2, axis=1) * sin
        s = _dot_nt(qh.astype(BF16), kh.astype(BF16)) * dret_ref[h]
        st = st_r[h]
        o_h = _dot(s.astype(BF16), vh) + _dot((qh * qdec_ref[:, lanes]).astype(BF16), st.astype(BF16))
        st_r[h] = st * step_decay[h] + _dot_tn((kh * kdec_ref[:, lanes]).astype(BF16), vh)
        mu = jnp.mean(o_h, axis=-1, keepdims=True)
        d = o_h - mu
        var = jnp.mean(d * d, axis=-1, keepdims=True)
        o_h = d * lax.rsqrt(var + EPS) * gng_ref[:, lanes] + gnb_ref[:, lanes]
        gate = proj_ref[0, :, OFF_RG + h * RET_DV:OFF_RG + (h + 1) * RET_DV]
        o_h = o_h * (gate * _sigmoid(gate))
        o_ref[0, :, GLA_V + h * RET_DV:GLA_V + (h + 1) * RET_DV] = o_h.astype(o_ref.dtype)


def _attention(proj, ga, cos, sin, w_alpha, b_alpha, gla_g, gn_g, gn_b):
    bsz, seq, _ = proj.shape
    r = ATTN_ROWS
    c = _attn_consts()
    w_alpha_p = jnp.zeros((LANES, GLA_QK), F32).at[:GLA_RANK].set(w_alpha).astype(BF16)
    const2 = lambda shape: pl.BlockSpec(shape, lambda b, i: (0,) * len(shape))
    row_blk = lambda w: pl.BlockSpec((1, r, w), lambda b, i: (b, i, 0))
    return pl.pallas_call(
        functools.partial(_attn_kernel, c["step_decay"]),
        grid=(bsz, seq // r),
        in_specs=[
            row_blk(PROJ_WIDTH), row_blk(LANES), row_blk(RET_DK), row_blk(RET_DK),
            const2((LANES, GLA_QK)), const2((1, GLA_QK)),
            const2((1, GLA_V)), const2((1, RET_V)), const2((1, RET_V)),
            const2((r, r)), const2((r, r)), const2((r, r)), const2((RET_HEADS, r, r)),
            const2((r, RET_QK)), const2((r, RET_QK)),
            const2((GLA_HEADS, 1, GLA_QK)), const2((GLA_V, GLA_QK)),
        ],
        out_specs=row_blk(MIX_WIDTH),
        out_shape=jax.ShapeDtypeStruct((bsz, seq, MIX_WIDTH), BF16),
        scratch_shapes=[pltpu.VMEM((GLA_V, GLA_QK), F32),
                        pltpu.VMEM((RET_HEADS, RET_DK, RET_DV), F32)],
        compiler_params=pltpu.CompilerParams(
            dimension_semantics=("arbitrary", "arbitrary"),
            vmem_limit_bytes=VMEM_LIMIT_BYTES),
        name="mixer_core",
    )(proj, ga, cos.reshape(bsz, seq, RET_DK), sin.reshape(bsz, seq, RET_DK),
      w_alpha_p, b_alpha.reshape(1, GLA_QK), gla_g.reshape(1, GLA_V),
      gn_g.reshape(1, RET_V), gn_b.reshape(1, RET_V),
      jnp.asarray(c["tri"], BF16), jnp.asarray(c["m_fwd"]), jnp.asarray(c["m_bwd"]),
      jnp.asarray(c["d_ret"]), jnp.asarray(c["qdec"]), jnp.asarray(c["kdec"]),
      jnp.asarray(c["hmask"]), jnp.asarray(c["bdiag"]))


def _outproj_kernel(o_ref, x_ref, mod_ref, w_ref, y_ref):
    m = mod_ref[0]
    y_ref[0] = x_ref[0] + m[2:3] * _dot(o_ref[0], w_ref[...])


def _out_proj(o, x, mod, w_out):
    bsz, seq, _ = x.shape
    tm = 512
    row = lambda w: pl.BlockSpec((1, tm, w), lambda b, i: (b, i, 0))
    return pl.pallas_call(
        _outproj_kernel,
        grid=(bsz, seq // tm),
        in_specs=[row(MIX_WIDTH), row(D_MODEL),
                  pl.BlockSpec((1, 6, D_MODEL), lambda b, i: (b, 0, 0)),
                  pl.BlockSpec((MIX_WIDTH, D_MODEL), lambda b, i: (0, 0))],
        out_specs=row(D_MODEL),
        out_shape=jax.ShapeDtypeStruct(x.shape, F32),
        compiler_params=pltpu.CompilerParams(
            dimension_semantics=("arbitrary", "arbitrary"),
            vmem_limit_bytes=VMEM_LIMIT_BYTES),
        name="out_proj",
    )(o, x, mod, w_out)


ROUTE_ROWS = 512
EXPERT_ROWS = 512
INFO_ID, INFO_RANK, INFO_PROB = 0, 2, 4


def _route_kernel(x_ref, mod_ref, g_ref, rw_ref, tril_ref, h_ref, info_ref, cnt_ref, carry):
    @pl.when(pl.program_id(0) == 0)
    def _():
        carry[...] = jnp.zeros_like(carry)

    m = mod_ref[0]
    h = _rms_mod(x_ref[...], g_ref[...], m[4:5], m[3:4])
    h_ref[...] = h
    h_hi = h.astype(BF16)
    h_lo = (h - h_hi.astype(F32)).astype(BF16)
    hh = _dot(h_hi, rw_ref[...])
    logits = hh[:, :LANES] + hh[:, LANES:] + _dot(h_lo, rw_ref[:, :LANES])
    lane = lax.broadcasted_iota(jnp.int32, logits.shape, 1)
    neg = jnp.float32(-jnp.inf)
    lg = jnp.where(lane < N_EXPERTS, logits, neg)
    m1 = jnp.max(lg, axis=-1, keepdims=True)
    i1 = jnp.min(jnp.where(lg == m1, lane, LANES), axis=-1, keepdims=True)
    lg2 = jnp.where(lane == i1, neg, lg)
    m2 = jnp.max(lg2, axis=-1, keepdims=True)
    i2 = jnp.min(jnp.where(lg2 == m2, lane, LANES), axis=-1, keepdims=True)
    e2 = jnp.exp(m2 - m1)
    p1 = 1.0 / (1.0 + e2)
    p2 = e2 * p1
    sel1 = lane == i1
    sel2 = lane == i2
    onehot = jnp.where(sel1, 1.0, 0.0) + jnp.where(sel2, 1.0, 0.0)
    incl = _dot(tril_ref[...], onehot.astype(BF16))
    before = incl - onehot + carry[0:1]
    r1 = jnp.sum(jnp.where(sel1, before, 0.0), axis=-1, keepdims=True)
    r2 = jnp.sum(jnp.where(sel2, before, 0.0), axis=-1, keepdims=True)
    carry[...] = carry[...] + incl[ROUTE_ROWS - 1:ROUTE_ROWS]
    cnt_ref[...] = carry[...]
    rec = jnp.zeros(logits.shape, F32)
    for pos, val in ((INFO_ID, i1.astype(F32)), (INFO_ID + 1, i2.astype(F32)),
                     (INFO_RANK, r1), (INFO_RANK + 1, r2),
                     (INFO_PROB, p1), (INFO_PROB + 1, p2)):
        rec = jnp.where(lane == pos, val, rec)
    info_ref[...] = rec


def _route(x2, mod, g, router_w, rows_per_batch):
    n = x2.shape[0]
    tm = ROUTE_ROWS
    per_b = rows_per_batch // tm
    w_hi = router_w.astype(BF16)
    w_lo = (router_w - w_hi.astype(F32)).astype(BF16)
    rw = jnp.zeros((D_MODEL, 2 * LANES), BF16)
    rw = rw.at[:, :N_EXPERTS].set(w_hi).at[:, LANES:LANES + N_EXPERTS].set(w_lo)
    tril = jnp.asarray(np.tril(np.ones((tm, tm), np.float32)), BF16)
    return pl.pallas_call(
        _route_kernel,
        grid=(n // tm,),
        in_specs=[pl.BlockSpec((tm, D_MODEL), lambda i: (i, 0)),
                  pl.BlockSpec((1, 6, D_MODEL), lambda i: (i // per_b, 0, 0)),
                  pl.BlockSpec((1, D_MODEL), lambda i: (0, 0)),
                  pl.BlockSpec((D_MODEL, 2 * LANES), lambda i: (0, 0)),
                  pl.BlockSpec((tm, tm), lambda i: (0, 0))],
        out_specs=[pl.BlockSpec((tm, D_MODEL), lambda i: (i, 0)),
                   pl.BlockSpec((tm, LANES), lambda i: (i, 0)),
                   pl.BlockSpec((8, LANES), lambda i: (0, 0))],
        out_shape=[jax.ShapeDtypeStruct((n, D_MODEL), F32),
                   jax.ShapeDtypeStruct((n, LANES), F32),
                   jax.ShapeDtypeStruct((8, LANES), F32)],
        scratch_shapes=[pltpu.VMEM((8, LANES), F32)],
        compiler_params=pltpu.CompilerParams(
            dimension_semantics=("arbitrary",), vmem_limit_bytes=VMEM_LIMIT_BYTES),
        name="route",
    )(x2, mod, g, rw, tril)


def _row_copy(src_ref, src_row, dst_ref, dst_row, sem):
    return pltpu.make_async_copy(src_ref.at[pl.ds(src_row, 1)], dst_ref.at[pl.ds(dst_row, 1)], sem)


def _dispatch_kernel(dest_ref, pad_ref, h_ref, xs_ref, zero_scr, sem, pad_sem):
    tm = ROUTE_ROWS
    i = pl.program_id(0)

    @pl.when(i == 0)
    def _():
        zero_scr[...] = jnp.zeros_like(zero_scr)
        for e in range(N_EXPERTS):
            lo = pad_ref[e]
            hi = pad_ref[N_EXPERTS + e]

            def fill(r, c):
                _row_copy(zero_scr, 0, xs_ref, r, pad_sem).start()
                return c

            lax.fori_loop(lo, hi, fill, 0)

            def drain(r, c):
                _row_copy(zero_scr, 0, xs_ref, r, pad_sem).wait()
                return c

            lax.fori_loop(lo, hi, drain, 0)

        def tile_copy(j):
            dst = xs_ref.at[pl.ds(pl.multiple_of(j * EXPERT_ROWS, EXPERT_ROWS), EXPERT_ROWS)]
            return pltpu.make_async_copy(zero_scr, dst, pad_sem)

        def fill_tile(j, c):
            tile_copy(j).start()
            return c

        def drain_tile(j, c):
            tile_copy(j).wait()
            return c

        n_tiles = xs_ref.shape[0] // EXPERT_ROWS
        lax.fori_loop(pad_ref[2 * N_EXPERTS], n_tiles, fill_tile, 0)
        lax.fori_loop(pad_ref[2 * N_EXPERTS], n_tiles, drain_tile, 0)

    base = i * (2 * tm)

    def issue(r, c):
        _row_copy(h_ref, r, xs_ref, dest_ref[base + 2 * r], sem).start()
        _row_copy(h_ref, r, xs_ref, dest_ref[base + 2 * r + 1], sem).start()
        return c

    lax.fori_loop(0, tm, issue, 0, unroll=8)
    for _ in range(2):
        pltpu.make_async_copy(h_ref, xs_ref.at[pl.ds(0, tm)], sem).wait()


def _dispatch(h2, dest, pad_bounds, n_sorted):
    n = h2.shape[0]
    tm = ROUTE_ROWS
    return pl.pallas_call(
        _dispatch_kernel,
        grid_spec=pltpu.PrefetchScalarGridSpec(
            num_scalar_prefetch=2,
            grid=(n // tm,),
            in_specs=[pl.BlockSpec((tm, D_MODEL), lambda i, d, p: (i, 0))],
            out_specs=pl.BlockSpec(memory_space=pl.ANY),
            scratch_shapes=[pltpu.VMEM((EXPERT_ROWS, D_MODEL), F32),
                            pltpu.SemaphoreType.DMA(()), pltpu.SemaphoreType.DMA(())]),
        out_shape=jax.ShapeDtypeStruct((n_sorted, D_MODEL), F32),
        compiler_params=pltpu.CompilerParams(dimension_semantics=("arbitrary",)),
        name="dispatch",
    )(dest, pad_bounds, h2)


def _expert_kernel(tile_blk, tile_e, n_act, x_ref, w1_ref, w3_ref, w2_ref, y_ref):
    active = pl.program_id(0) < n_act[0]

    @pl.when(active)
    def _():
        h = x_ref[...].astype(BF16)
        a = _dot(h, w1_ref[0])
        u = (a * _sigmoid(a) * _dot(h, w3_ref[0])).astype(BF16)
        y_ref[...] = _dot(u, w2_ref[0])

    @pl.when(jnp.logical_not(active))
    def _():
        y_ref[...] = jnp.zeros_like(y_ref)


def _experts(xs, tile_blk, tile_e, n_act, w1, w3, w2):
    n_sorted = xs.shape[0]
    tm = EXPERT_ROWS
    f = w1.shape[-1]
    return pl.pallas_call(
        _expert_kernel,
        grid_spec=pltpu.PrefetchScalarGridSpec(
            num_scalar_prefetch=3,
            grid=(n_sorted // tm,),
            in_specs=[pl.BlockSpec((tm, D_MODEL), lambda i, b, e, n: (b[i], 0)),
                      pl.BlockSpec((1, D_MODEL, f), lambda i, b, e, n: (e[i], 0, 0)),
                      pl.BlockSpec((1, D_MODEL, f), lambda i, b, e, n: (e[i], 0, 0)),
                      pl.BlockSpec((1, f, D_MODEL), lambda i, b, e, n: (e[i], 0, 0))],
            out_specs=pl.BlockSpec((tm, D_MODEL), lambda i, b, e, n: (i, 0))),
        out_shape=jax.ShapeDtypeStruct((n_sorted, D_MODEL), F32),
        compiler_params=pltpu.CompilerParams(
            dimension_semantics=("arbitrary",), vmem_limit_bytes=VMEM_LIMIT_BYTES),
        name="experts",
    )(tile_blk, tile_e, n_act, xs, w1, w3, w2)


def _combine_kernel(final_norm, dest_ref, x_ref, mod_ref, info_ref, fg_ref, ys_ref, out_ref, buf, sem):
    tm = ROUTE_ROWS
    base = pl.program_id(0) * (2 * tm)

    def issue(r, c):
        _row_copy(ys_ref, dest_ref[base + 2 * r], buf.at[0], r, sem).start()
        _row_copy(ys_ref, dest_ref[base + 2 * r + 1], buf.at[1], r, sem).start()
        return c

    lax.fori_loop(0, tm, issue, 0, unroll=8)
    for k in range(2):
        pltpu.make_async_copy(ys_ref.at[pl.ds(0, tm)], buf.at[k], sem).wait()
    info = info_ref[...]
    y = info[:, INFO_PROB:INFO_PROB + 1] * buf[0] + info[:, INFO_PROB + 1:INFO_PROB + 2] * buf[1]
    out = x_ref[...] + mod_ref[0][5:6] * y
    if final_norm:
        ms = jnp.mean(out * out, axis=-1, keepdims=True)
        out = out * lax.rsqrt(ms + EPS) * fg_ref[...]
    out_ref[...] = out


def _combine(x2, mod, info, ys, dest, final_g, final_norm, rows_per_batch):
    n = x2.shape[0]
    tm = ROUTE_ROWS
    per_b = rows_per_batch // tm
    return pl.pallas_call(
        functools.partial(_combine_kernel, final_norm),
        grid_spec=pltpu.PrefetchScalarGridSpec(
            num_scalar_prefetch=1,
            grid=(n // tm,),
            in_specs=[pl.BlockSpec((tm, D_MODEL), lambda i, d: (i, 0)),
                      pl.BlockSpec((1, 6, D_MODEL), lambda i, d: (i // per_b, 0, 0)),
                      pl.BlockSpec((tm, LANES), lambda i, d: (i, 0)),
                      pl.BlockSpec((1, D_MODEL), lambda i, d: (0, 0)),
                      pl.BlockSpec(memory_space=pl.ANY)],
            out_specs=pl.BlockSpec((tm, D_MODEL), lambda i, d: (i, 0)),
            scratch_shapes=[pltpu.VMEM((2, tm, D_MODEL), F32), pltpu.SemaphoreType.DMA(())]),
        out_shape=jax.ShapeDtypeStruct(x2.shape, F32),
        compiler_params=pltpu.CompilerParams(
            dimension_semantics=("arbitrary",), vmem_limit_bytes=VMEM_LIMIT_BYTES),
        name="combine",
    )(dest, x2, mod, info, final_g, ys)


def _moe(x, mod, g, router_w, w1, w3, w2, final_g, final_norm):
    bsz, seq, _ = x.shape
    n = bsz * seq
    tm = EXPERT_ROWS
    x2 = x.reshape(n, D_MODEL)
    h2, info, cnt = _route(x2, mod, g, router_w, seq)
    counts = cnt[0, :N_EXPERTS].astype(jnp.int32)
    tiles = (counts + (tm - 1)) // tm
    tile_end = jnp.cumsum(tiles)
    group_off = (tile_end - tiles) * tm
    ids = info[:, INFO_ID:INFO_ID + 2].astype(jnp.int32)
    rank = info[:, INFO_RANK:INFO_RANK + 2].astype(jnp.int32)
    dest = (group_off[ids] + rank).reshape(2 * n)
    n_tiles = 2 * n // tm + N_EXPERTS
    n_act = tile_end[-1]
    t = jnp.minimum(jnp.arange(n_tiles, dtype=jnp.int32), n_act - 1)
    tile_e = jnp.sum(t[:, None] >= tile_end[None, :], axis=1).astype(jnp.int32)
    pad_bounds = jnp.concatenate([group_off + counts, tile_end * tm, n_act[None]]).astype(jnp.int32)
    xs = _dispatch(h2, dest, pad_bounds, n_tiles * tm)
    ys = _experts(xs, t, tile_e, n_act.reshape(1).astype(jnp.int32), w1, w3, w2)
    out = _combine(x2, mod, info, ys, dest, final_g, final_norm, seq)
    return out.reshape(x.shape)


def _ffn_kernel(final_norm, n_groups, x_ref, mod_ref, g_ref, w1_ref, w3_ref, w2_ref, fg_ref, y_ref,
                h_scr, acc_scr):
    e = pl.program_id(2)
    m = mod_ref[0]

    @pl.when(e == 0)
    def _():
        h_scr[...] = _rms_mod(x_ref[0], g_ref[...], m[4:5], m[3:4]).astype(BF16)
        acc_scr[...] = jnp.zeros_like(acc_scr)

    h = h_scr[...]
    a = _dot(h, w1_ref[0])
    u = (a * _sigmoid(a) * _dot(h, w3_ref[0])).astype(BF16)
    acc_scr[...] += _dot(u, w2_ref[0])

    @pl.when(e == n_groups - 1)
    def _():
        out = x_ref[0] + m[5:6] * acc_scr[...]
        if final_norm:
            ms = jnp.mean(out * out, axis=-1, keepdims=True)
            out = out * lax.rsqrt(ms + EPS) * fg_ref[...]
        y_ref[0] = out


def _ffn(x, mod, g, w1, w3, w2, final_g, final_norm):
    bsz, seq, _ = x.shape
    n_groups, _, f = w1.shape
    tm = 1024
    row = pl.BlockSpec((1, tm, D_MODEL), lambda b, i, e: (b, i, 0))
    vec = pl.BlockSpec((1, D_MODEL), lambda b, i, e: (0, 0))
    return pl.pallas_call(
        functools.partial(_ffn_kernel, final_norm, n_groups),
        grid=(bsz, seq // tm, n_groups),
        in_specs=[row,
                  pl.BlockSpec((1, 6, D_MODEL), lambda b, i, e: (b, 0, 0)),
                  vec,
                  pl.BlockSpec((1, D_MODEL, f), lambda b, i, e: (e, 0, 0)),
                  pl.BlockSpec((1, D_MODEL, f), lambda b, i, e: (e, 0, 0)),
                  pl.BlockSpec((1, f, D_MODEL), lambda b, i, e: (e, 0, 0)),
                  vec],
        out_specs=row,
        out_shape=jax.ShapeDtypeStruct(x.shape, F32),
        scratch_shapes=[pltpu.VMEM((tm, D_MODEL), BF16), pltpu.VMEM((tm, D_MODEL), F32)],
        compiler_params=pltpu.CompilerParams(
            dimension_semantics=("arbitrary", "arbitrary", "arbitrary"),
            vmem_limit_bytes=VMEM_LIMIT_BYTES),
        name="ffn",
    )(x, mod, g, w1, w3, w2, final_g)


def kernel(x, c, positions, ada_w, ada_b, norm_mix_g, norm_ffn_g, w_in, gla_w_alpha, gla_b_alpha,
           gla_norm_g, ret_gn_g, ret_gn_b, w_out, ffn_w1, ffn_w3, ffn_w2, router_w, moe_w1,
           moe_w3, moe_w2, final_g):
    bsz = x.shape[0]
    mod_all = _ada_mod(c, ada_w, ada_b).reshape(DEPTH, bsz, 6, D_MODEL)
    cos, sin = _rope_tables(positions)
    ga_lo = 2 * GLA_QK + 2 * GLA_V
    fg = final_g.reshape(1, D_MODEL)
    f_half = ffn_w1.shape[-1] // 2
    for layer in range(DEPTH):
        mod = mod_all[layer]
        w = w_in[layer]
        w_main = jnp.concatenate([w[:, :ga_lo], w[:, ga_lo + GLA_RANK:]], axis=1).astype(BF16)
        w_ga = jnp.zeros((D_MODEL, LANES), F32).at[:, :GLA_RANK].set(
            w[:, ga_lo:ga_lo + GLA_RANK]).astype(BF16)
        proj, ga = _in_proj(x, mod, norm_mix_g[layer].reshape(1, D_MODEL), w_main, w_ga)
        o = _attention(proj, ga, cos, sin, gla_w_alpha[layer], gla_b_alpha[layer],
                       gla_norm_g[layer], ret_gn_g[layer], ret_gn_b[layer])
        x = _out_proj(o, x, mod, w_out[layer].astype(BF16))
        g_ffn = norm_ffn_g[layer].reshape(1, D_MODEL)
        i = layer // 2
        last = layer == DEPTH - 1
        if layer % 2 == 0:
            w1 = ffn_w1[i].reshape(D_MODEL, 2, f_half).transpose(1, 0, 2).astype(BF16)
            w3 = ffn_w3[i].reshape(D_MODEL, 2, f_half).transpose(1, 0, 2).astype(BF16)
            w2 = ffn_w2[i].reshape(2, f_half, D_MODEL).astype(BF16)
            x = _ffn(x, mod, g_ffn, w1, w3, w2, fg, last)
        else:
            x = _moe(x, mod, g_ffn, router_w[i], moe_w1[i].astype(BF16), moe_w3[i].astype(BF16),
                     moe_w2[i].astype(BF16), fg, last)
    return x
```

```python
import functools

import numpy as np
import jax
import jax.numpy as jnp
from jax import lax
from jax.experimental import pallas as pl
from jax.experimental.pallas import tpu as pltpu

F32 = jnp.float32
BF16 = jnp.bfloat16

D_MODEL = 1024
DEPTH = 4
CHUNK = 64
GLA_HEADS = 4
GLA_DK = 64
GLA_DV = 128
GLA_RANK = 16
GLA_TAU = 16.0
RET_HEADS = 4
RET_DK = 128
RET_DV = 128
ROPE_BASE = 10000.0
N_EXPERTS = 8
EPS = 1e-6
GLA_QK = GLA_HEADS * GLA_DK
GLA_V = GLA_HEADS * GLA_DV
RET_QK = RET_HEADS * RET_DK
RET_V = RET_HEADS * RET_DV
MIX_WIDTH = GLA_V + RET_V

LANES = 128
VMEM_LIMIT_BYTES = 56 * 1024 * 1024

PROJ_WIDTH = 2 * GLA_QK + 2 * GLA_V + 2 * RET_QK + 2 * RET_V
OFF_GQ = 0
OFF_GK = OFF_GQ + GLA_QK
OFF_GV = OFF_GK + GLA_QK
OFF_GR = OFF_GV + GLA_V
OFF_RQ = OFF_GR + GLA_V
OFF_RK = OFF_RQ + RET_QK
OFF_RV = OFF_RK + RET_QK
OFF_RG = OFF_RV + RET_V

ATTN_ROWS = 256
ATTN_CHUNKS = ATTN_ROWS // CHUNK


def _sigmoid(x):
    return 1.0 / (1.0 + jnp.exp(-x))


def _rms_mod(x, g, sc, sh):
    ms = jnp.mean(x * x, axis=-1, keepdims=True)
    return (x * lax.rsqrt(ms + EPS)) * g * (1.0 + sc) + sh


def _dot(a, b):
    return jnp.dot(a, b, preferred_element_type=F32)


def _dot_nt(a, b):
    return lax.dot_general(a, b, (((1,), (1,)), ((), ())), preferred_element_type=F32)


def _dot_tn(a, b):
    return lax.dot_general(a, b, (((0,), (0,)), ((), ())), preferred_element_type=F32)


def _ada_kernel(c_ref, w_ref, b_ref, o_ref):
    c = c_ref[...]
    cond = c * _sigmoid(c)
    o_ref[0] = jnp.dot(cond, w_ref[0], precision=lax.Precision.HIGHEST,
                       preferred_element_type=F32) + b_ref[0]


def _ada_mod(c, ada_w, ada_b):
    bsz = c.shape[0]
    tn = 1024
    n_out = ada_w.shape[-1]
    return pl.pallas_call(
        _ada_kernel,
        grid=(DEPTH, n_out // tn),
        in_specs=[
            pl.BlockSpec((bsz, D_MODEL), lambda l, j: (0, 0)),
            pl.BlockSpec((1, D_MODEL, tn), lambda l, j: (l, 0, j)),
            pl.BlockSpec((1, 1, tn), lambda l, j: (l, 0, j)),
        ],
        out_specs=pl.BlockSpec((1, bsz, tn), lambda l, j: (l, 0, j)),
        out_shape=jax.ShapeDtypeStruct((DEPTH, bsz, n_out), F32),
        name="ada_mod",
    )(c, ada_w, ada_b.reshape(DEPTH, 1, n_out))


def _rope_kernel(pos_ref, invf_ref, sign_ref, cos_ref, sin_ref):
    ang = pos_ref[...] * invf_ref[...]
    cos_ref[...] = jnp.cos(ang)
    sin_ref[...] = jnp.sin(ang) * sign_ref[...]


def _rope_tables(positions):
    n = positions.size
    half = RET_DK // 2
    inv_freq = ROPE_BASE ** (-jnp.arange(half, dtype=F32) / half)
    invf = jnp.concatenate([inv_freq, inv_freq]).reshape(1, RET_DK)
    sign = jnp.concatenate([-jnp.ones((half,), F32), jnp.ones((half,), F32)]).reshape(1, RET_DK)
    pos = positions.astype(F32).reshape(n, 1)
    tm = 2048
    row = pl.BlockSpec((1, RET_DK), lambda i: (0, 0))
    out = pl.BlockSpec((tm, RET_DK), lambda i: (i, 0))
    return pl.pallas_call(
        _rope_kernel,
        grid=(n // tm,),
        in_specs=[pl.BlockSpec((tm, 1), lambda i: (i, 0)), row, row],
        out_specs=[out, out],
        out_shape=[jax.ShapeDtypeStruct((n, RET_DK), F32)] * 2,
        name="rope_tables",
    )(pos, invf, sign)


def _inproj_kernel(x_ref, mod_ref, g_ref, w_ref, wga_ref, proj_ref, ga_ref):
    m = mod_ref[0]
    h = _rms_mod(x_ref[0], g_ref[...], m[1:2], m[0:1]).astype(BF16)
    proj_ref[0] = _dot(h, w_ref[...])
    ga_ref[0] = _dot(h, wga_ref[...])


def _in_proj(x, mod, g, w_main, w_ga):
    bsz, seq, _ = x.shape
    tm = 512
    return pl.pallas_call(
        _inproj_kernel,
        grid=(bsz, seq // tm),
        in_specs=[
            pl.BlockSpec((1, tm, D_MODEL), lambda b, i: (b, i, 0)),
            pl.BlockSpec((1, 6, D_MODEL), lambda b, i: (b, 0, 0)),
            pl.BlockSpec((1, D_MODEL), lambda b, i: (0, 0)),
            pl.BlockSpec((D_MODEL, PROJ_WIDTH), lambda b, i: (0, 0)),
            pl.BlockSpec((D_MODEL, LANES), lambda b, i: (0, 0)),
        ],
        out_specs=[
            pl.BlockSpec((1, tm, PROJ_WIDTH), lambda b, i: (b, i, 0)),
            pl.BlockSpec((1, tm, LANES), lambda b, i: (b, i, 0)),
        ],
        out_shape=[
            jax.ShapeDtypeStruct((bsz, seq, PROJ_WIDTH), F32),
            jax.ShapeDtypeStruct((bsz, seq, LANES), F32),
        ],
        compiler_params=pltpu.CompilerParams(
            dimension_semantics=("arbitrary", "arbitrary"),
            vmem_limit_bytes=VMEM_LIMIT_BYTES),
        name="in_proj",
    )(x, mod, g, w_main, w_ga)


def _attn_consts():
    r = ATTN_ROWS
    t = np.arange(r)
    same = (t[:, None] // CHUNK) == (t[None, :] // CHUNK)
    causal = t[:, None] >= t[None, :]
    tri = (same & causal).astype(np.float32)
    m_fwd = tri
    m_bwd = (same & ~causal).astype(np.float32)
    gam = 1.0 - 2.0 ** (-5.0 - np.arange(RET_HEADS, dtype=np.float64))
    lg = np.log(gam)
    dist = (t[:, None] - t[None, :]).astype(np.float64)
    d_ret = np.where(causal[None], np.exp(lg[:, None, None] * dist[None]),
                     np.where(same[None], np.exp(-lg[:, None, None] * dist[None]), 0.0))
    qdec = np.repeat(np.exp(lg[None, :] * (t[:, None] + 1.0)), RET_DK, axis=1)
    kdec = np.repeat(np.exp(lg[None, :] * (r - 1.0 - t[:, None])), RET_DK, axis=1)
    step_decay = [float(np.exp(lg[h] * r)) for h in range(RET_HEADS)]
    hmask = np.zeros((GLA_HEADS, 1, GLA_QK), np.float32)
    for h in range(GLA_HEADS):
        hmask[h, 0, h * GLA_DK:(h + 1) * GLA_DK] = 1.0
    vv = np.arange(GLA_V)[:, None] // GLA_DV
    kk = np.arange(GLA_QK)[None, :] // GLA_DK
    bdiag = (vv == kk).astype(np.float32)
    return dict(tri=tri, m_fwd=m_fwd, m_bwd=m_bwd, d_ret=d_ret.astype(np.float32),
                qdec=qdec.astype(np.float32), kdec=kdec.astype(np.float32),
                step_decay=step_decay, hmask=hmask, bdiag=bdiag)


def _attn_kernel(step_decay, proj_ref, ga_ref, cos_ref, sin_ref, walpha_ref, balpha_ref,
                 glag_ref, gng_ref, gnb_ref, tri_ref, mfwd_ref, mbwd_ref, dret_ref,
                 qdec_ref, kdec_ref, hmask_ref, bdiag_ref, o_ref, st_g, st_r):
    r = ATTN_ROWS

    @pl.when(pl.program_id(1) == 0)
    def _():
        st_g[...] = jnp.zeros_like(st_g)
        st_r[...] = jnp.zeros_like(st_r)

    z = _dot(ga_ref[0].astype(BF16), walpha_ref[...]) + balpha_ref[...]
    log_a = (jnp.minimum(z, 0.0) - jnp.log(1.0 + jnp.exp(-jnp.abs(z)))) * (1.0 / GLA_TAU)
    hi = log_a.astype(BF16)
    r1 = log_a - hi.astype(F32)
    mid = r1.astype(BF16)
    lo = (r1 - mid.astype(F32)).astype(BF16)
    tri = tri_ref[...]
    b = _dot(tri, hi) + _dot(tri, mid) + _dot(tri, lo)
    b3 = b.reshape(ATTN_CHUNKS, CHUNK, GLA_QK)
    b_last = b3[:, CHUNK - 1:CHUNK, :]
    k_upd_scale = jnp.exp(b_last - b3).reshape(r, GLA_QK)
    eb = jnp.exp(b)
    enb = jnp.exp(-b)
    q = proj_ref[0, :, OFF_GQ:OFF_GQ + GLA_QK] * (GLA_DK ** -0.5)
    k = proj_ref[0, :, OFF_GK:OFF_GK + GLA_QK]
    v_bf = proj_ref[0, :, OFF_GV:OFF_GV + GLA_V].astype(BF16)
    q_f = q * eb
    q_b = q * enb
    k_f = (k * enb).astype(BF16)
    k_b = (k * eb).astype(BF16)
    k_u = (k * k_upd_scale).astype(BF16)
    q_f_bf = q_f.astype(BF16)

    bdiag = bdiag_ref[...]
    inter = []
    for g in range(ATTN_CHUNKS):
        rows = slice(g * CHUNK, (g + 1) * CHUNK)
        st = st_g[...]
        inter.append(_dot_nt(q_f_bf[rows], st.astype(BF16)))
        upd_t = _dot_tn(v_bf[rows], k_u[rows]) * bdiag
        st_g[...] = st * jnp.exp(b_last[g]) + upd_t
    o_inter = jnp.concatenate(inter, axis=0)

    m_fwd = mfwd_ref[...] > 0.5
    m_bwd = mbwd_ref[...] > 0.5
    for h in range(GLA_HEADS):
        hm = hmask_ref[h]
        s_f = _dot_nt((q_f * hm).astype(BF16), k_f)
        s_b = _dot_nt((q_b * hm).astype(BF16), k_b)
        sc = jnp.where(m_fwd, s_f, jnp.where(m_bwd, s_b, 0.0)).astype(BF16)
        lanes = slice(h * GLA_DV, (h + 1) * GLA_DV)
        o_h = _dot(sc, v_bf[:, lanes]) + o_inter[:, lanes]
        o_h = o_h * lax.rsqrt(jnp.mean(o_h * o_h, axis=-1, keepdims=True) + EPS)
        gate = proj_ref[0, :, OFF_GR + h * GLA_DV:OFF_GR + (h + 1) * GLA_DV]
        o_h = o_h * glag_ref[:, lanes] * (gate * _sigmoid(gate))
        o_ref[0, :, lanes] = o_h.astype(o_ref.dtype)

    cos = cos_ref[0]
    sin = sin_ref[0]
    for h in range(RET_HEADS):
        lanes = slice(h * RET_DK, (h + 1) * RET_DK)
        qh = proj_ref[0, :, OFF_RQ + h * RET_DK:OFF_RQ + (h + 1) * RET_DK]
        kh = proj_ref[0, :, OFF_RK + h * RET_DK:OFF_RK + (h + 1) * RET_DK]
        vh = proj_ref[0, :, OFF_RV + h * RET_DV:OFF_RV + (h + 1) * RET_DV].astype(BF16)
        qh = (qh * cos + pltpu.roll(qh, RET_DK // 2, axis=1) * sin) * (RET_DK ** -0.5)
        kh = kh * cos + pltpu.roll(kh, RET_DK // 2, axis=1) * sin
        s = _dot_nt(qh.astype(BF16), kh.astype(BF16)) * dret_ref[h]
        st = st_r[h]
        o_h = _dot(s.astype(BF16), vh) + _dot((qh * qdec_ref[:, lanes]).astype(BF16), st.astype(BF16))
        st_r[h] = st * step_decay[h] + _dot_tn((kh * kdec_ref[:, lanes]).astype(BF16), vh)
        mu = jnp.mean(o_h, axis=-1, keepdims=True)
        d = o_h - mu
        var = jnp.mean(d * d, axis=-1, keepdims=True)
        o_h = d * lax.rsqrt(var + EPS) * gng_ref[:, lanes] + gnb_ref[:, lanes]
        gate = proj_ref[0, :, OFF_RG + h * RET_DV:OFF_RG + (h + 1) * RET_DV]
        o_h = o_h * (gate * _sigmoid(gate))
        o_ref[0, :, GLA_V + h * RET_DV:GLA_V + (h + 1) * RET_DV] = o_h.astype(o_ref.dtype)


def _attention(proj, ga, cos, sin, w_alpha, b_alpha, gla_g, gn_g, gn_b):
    bsz, seq, _ = proj.shape
    r = ATTN_ROWS
    c = _attn_consts()
    w_alpha_p = jnp.zeros((LANES, GLA_QK), F32).at[:GLA_RANK].set(w_alpha).astype(BF16)
    const2 = lambda shape: pl.BlockSpec(shape, lambda b, i: (0,) * len(shape))
    row_blk = lambda w: pl.BlockSpec((1, r, w), lambda b, i: (b, i, 0))
    return pl.pallas_call(
        functools.partial(_attn_kernel, c["step_decay"]),
        grid=(bsz, seq // r),
        in_specs=[
            row_blk(PROJ_WIDTH), row_blk(LANES), row_blk(RET_DK), row_blk(RET_DK),
            const2((LANES, GLA_QK)), const2((1, GLA_QK)),
            const2((1, GLA_V)), const2((1, RET_V)), const2((1, RET_V)),
            const2((r, r)), const2((r, r)), const2((r, r)), const2((RET_HEADS, r, r)),
            const2((r, RET_QK)), const2((r, RET_QK)),
            const2((GLA_HEADS, 1, GLA_QK)), const2((GLA_V, GLA_QK)),
        ],
        out_specs=row_blk(MIX_WIDTH),
        out_shape=jax.ShapeDtypeStruct((bsz, seq, MIX_WIDTH), BF16),
        scratch_shapes=[pltpu.VMEM((GLA_V, GLA_QK), F32),
                        pltpu.VMEM((RET_HEADS, RET_DK, RET_DV), F32)],
        compiler_params=pltpu.CompilerParams(
            dimension_semantics=("arbitrary", "arbitrary"),
            vmem_limit_bytes=VMEM_LIMIT_BYTES),
        name="mixer_core",
    )(proj, ga, cos.reshape(bsz, seq, RET_DK), sin.reshape(bsz, seq, RET_DK),
      w_alpha_p, b_alpha.reshape(1, GLA_QK), gla_g.reshape(1, GLA_V),
      gn_g.reshape(1, RET_V), gn_b.reshape(1, RET_V),
      jnp.asarray(c["tri"], BF16), jnp.asarray(c["m_fwd"]), jnp.asarray(c["m_bwd"]),
      jnp.asarray(c["d_ret"]), jnp.asarray(c["qdec"]), jnp.asarray(c["kdec"]),
      jnp.asarray(c["hmask"]), jnp.asarray(c["bdiag"]))


def _outproj_kernel(o_ref, x_ref, mod_ref, w_ref, y_ref):
    m = mod_ref[0]
    y_ref[0] = x_ref[0] + m[2:3] * _dot(o_ref[0], w_ref[...])


def _out_proj(o, x, mod, w_out):
    bsz, seq, _ = x.shape
    tm = 512
    row = lambda w: pl.BlockSpec((1, tm, w), lambda b, i: (b, i, 0))
    return pl.pallas_call(
        _outproj_kernel,
        grid=(bsz, seq // tm),
        in_specs=[row(MIX_WIDTH), row(D_MODEL),
                  pl.BlockSpec((1, 6, D_MODEL), lambda b, i: (b, 0, 0)),
                  pl.BlockSpec((MIX_WIDTH, D_MODEL), lambda b, i: (0, 0))],
        out_specs=row(D_MODEL),
        out_shape=jax.ShapeDtypeStruct(x.shape, F32),
        compiler_params=pltpu.CompilerParams(
            dimension_semantics=("arbitrary", "arbitrary"),
            vmem_limit_bytes=VMEM_LIMIT_BYTES),
        name="out_proj",
    )(o, x, mod, w_out)


ROUTE_ROWS = 512
EXPERT_ROWS = 512
SEG_ALIGN = 8
LOCAL_ROWS = -(-(2 * ROUTE_ROWS + N_EXPERTS * (SEG_ALIGN - 1)) // LANES) * LANES
INFO_POS, INFO_PROB = 0, 2


def _route_kernel(x_ref, mod_ref, g_ref, rw_ref, tril_ref, upper_ref, h_ref, info_ref, cnt_ref):
    m = mod_ref[0]
    h = _rms_mod(x_ref[...], g_ref[...], m[4:5], m[3:4])
    h_hi = h.astype(BF16)
    h_ref[...] = h_hi
    h_lo = (h - h_hi.astype(F32)).astype(BF16)
    hh = _dot(h_hi, rw_ref[...])
    logits = hh[:, :LANES] + hh[:, LANES:] + _dot(h_lo, rw_ref[:, :LANES])
    lane = lax.broadcasted_iota(jnp.int32, logits.shape, 1)
    neg = jnp.float32(-jnp.inf)
    lg = jnp.where(lane < N_EXPERTS, logits, neg)
    m1 = jnp.max(lg, axis=-1, keepdims=True)
    i1 = jnp.min(jnp.where(lg == m1, lane, LANES), axis=-1, keepdims=True)
    lg2 = jnp.where(lane == i1, neg, lg)
    m2 = jnp.max(lg2, axis=-1, keepdims=True)
    i2 = jnp.min(jnp.where(lg2 == m2, lane, LANES), axis=-1, keepdims=True)
    e2 = jnp.exp(m2 - m1)
    p1 = 1.0 / (1.0 + e2)
    p2 = e2 * p1
    sel1 = lane == i1
    sel2 = lane == i2
    onehot = jnp.where(sel1, 1.0, 0.0) + jnp.where(sel2, 1.0, 0.0)
    incl = _dot(tril_ref[...], onehot.astype(BF16))
    counts = incl[ROUTE_ROWS - 1:ROUTE_ROWS]
    seg_units = jnp.floor((counts + (SEG_ALIGN - 1)) * (1.0 / SEG_ALIGN))
    seg_units8 = jnp.broadcast_to(seg_units, (8, LANES)).astype(BF16)
    seg_off = _dot(seg_units8, upper_ref[...])[0:1] * SEG_ALIGN
    pos = incl - onehot + seg_off
    pos1 = jnp.sum(jnp.where(sel1, pos, 0.0), axis=-1, keepdims=True)
    pos2 = jnp.sum(jnp.where(sel2, pos, 0.0), axis=-1, keepdims=True)
    cnt_ref[...] = jnp.broadcast_to(counts, (8, LANES))
    rec = jnp.zeros(logits.shape, F32)
    for lane_id, val in ((INFO_POS, pos1), (INFO_POS + 1, pos2), (INFO_PROB, p1), (INFO_PROB + 1, p2)):
        rec = jnp.where(lane == lane_id, val, rec)
    info_ref[...] = rec


def _route(x2, mod, g, router_w, rows_per_batch):
    n = x2.shape[0]
    tm = ROUTE_ROWS
    per_b = rows_per_batch // tm
    w_hi = router_w.astype(BF16)
    w_lo = (router_w - w_hi.astype(F32)).astype(BF16)
    rw = jnp.zeros((D_MODEL, 2 * LANES), BF16)
    rw = rw.at[:, :N_EXPERTS].set(w_hi).at[:, LANES:LANES + N_EXPERTS].set(w_lo)
    tril = jnp.asarray(np.tril(np.ones((tm, tm), np.float32)), BF16)
    upper = jnp.asarray(np.triu(np.ones((LANES, LANES), np.float32), 1), BF16)
    return pl.pallas_call(
        _route_kernel,
        grid=(n // tm,),
        in_specs=[pl.BlockSpec((tm, D_MODEL), lambda i: (i, 0)),
                  pl.BlockSpec((1, 6, D_MODEL), lambda i: (i // per_b, 0, 0)),
                  pl.BlockSpec((1, D_MODEL), lambda i: (0, 0)),
                  pl.BlockSpec((D_MODEL, 2 * LANES), lambda i: (0, 0)),
                  pl.BlockSpec((tm, tm), lambda i: (0, 0)),
                  pl.BlockSpec((LANES, LANES), lambda i: (0, 0))],
        out_specs=[pl.BlockSpec((tm, D_MODEL), lambda i: (i, 0)),
                   pl.BlockSpec((tm, LANES), lambda i: (i, 0)),
                   pl.BlockSpec((8, LANES), lambda i: (i, 0))],
        out_shape=[jax.ShapeDtypeStruct((n, D_MODEL), BF16),
                   jax.ShapeDtypeStruct((n, LANES), F32),
                   jax.ShapeDtypeStruct((n // tm * 8, LANES), F32)],
        compiler_params=pltpu.CompilerParams(
            dimension_semantics=("arbitrary",), vmem_limit_bytes=VMEM_LIMIT_BYTES),
        name="route",
    )(x2, mod, g, rw, tril, upper)


def _chunk_copy(src_ref, src_row, dst_ref, dst_row, sem):
    src = src_ref.at[pl.ds(pl.multiple_of(src_row, SEG_ALIGN), SEG_ALIGN)]
    dst = dst_ref.at[pl.ds(pl.multiple_of(dst_row, SEG_ALIGN), SEG_ALIGN)]
    return pltpu.make_async_copy(src, dst, sem)


def _for_each_segment_chunk(seg_ref, n_seg, tile, fn):
    for e in range(N_EXPERTS):
        s = tile * N_EXPERTS + e
        sorted_row = seg_ref[s]
        local_row = seg_ref[n_seg + s]

        def body(c, carry):
            fn(local_row + c * SEG_ALIGN, sorted_row + c * SEG_ALIGN)
            return carry

        lax.fori_loop(0, seg_ref[2 * n_seg + s], body, 0)


def _selection(info, coeff1, coeff2):
    lane = lax.broadcasted_iota(jnp.int32, (info.shape[0], LOCAL_ROWS), 1)
    pos1 = info[:, INFO_POS:INFO_POS + 1].astype(jnp.int32)
    pos2 = info[:, INFO_POS + 1:INFO_POS + 2].astype(jnp.int32)
    return jnp.where(lane == pos1, coeff1, jnp.where(lane == pos2, coeff2, 0.0)).astype(BF16)


def _dispatch_kernel(n_seg, seg_ref, pad_ref, h_ref, info_ref, xs_ref, sorted_scr, zero_scr, sem, pad_sem):
    i = pl.program_id(0)

    @pl.when(i == 0)
    def _():
        zero_scr[...] = jnp.zeros_like(zero_scr)
        for e in range(N_EXPERTS):
            lo = pad_ref[e]
            n_chunks = pad_ref[N_EXPERTS + e]

            def fill(c, carry):
                _chunk_copy(zero_scr, 0, xs_ref, lo + c * SEG_ALIGN, pad_sem).start()
                return carry

            def drain(c, carry):
                _chunk_copy(zero_scr, 0, xs_ref, lo + c * SEG_ALIGN, pad_sem).wait()
                return carry

            lax.fori_loop(0, n_chunks, fill, 0)
            lax.fori_loop(0, n_chunks, drain, 0)

        def tile_copy(j):
            dst = xs_ref.at[pl.ds(pl.multiple_of(j * EXPERT_ROWS, EXPERT_ROWS), EXPERT_ROWS)]
            return pltpu.make_async_copy(zero_scr, dst, pad_sem)

        def fill_tile(j, carry):
            tile_copy(j).start()
            return carry

        def drain_tile(j, carry):
            tile_copy(j).wait()
            return carry

        n_tiles = xs_ref.shape[0] // EXPERT_ROWS
        lax.fori_loop(pad_ref[2 * N_EXPERTS], n_tiles, fill_tile, 0)
        lax.fori_loop(pad_ref[2 * N_EXPERTS], n_tiles, drain_tile, 0)

    sel = _selection(info_ref[...], 1.0, 1.0)
    sorted_scr[...] = _dot_tn(sel, h_ref[...])
    _for_each_segment_chunk(seg_ref, n_seg, i,
                            lambda loc, srt: _chunk_copy(sorted_scr, loc, xs_ref, srt, sem).start())
    _for_each_segment_chunk(seg_ref, n_seg, i,
                            lambda loc, srt: _chunk_copy(sorted_scr, loc, xs_ref, srt, sem).wait())


def _dispatch(h2, info, segs, pads, n_sorted):
    n = h2.shape[0]
    tm = ROUTE_ROWS
    n_seg = n // tm * N_EXPERTS
    return pl.pallas_call(
        functools.partial(_dispatch_kernel, n_seg),
        grid_spec=pltpu.PrefetchScalarGridSpec(
            num_scalar_prefetch=2,
            grid=(n // tm,),
            in_specs=[pl.BlockSpec((tm, D_MODEL), lambda i, s, p: (i, 0)),
                      pl.BlockSpec((tm, LANES), lambda i, s, p: (i, 0))],
            out_specs=pl.BlockSpec(memory_space=pl.ANY),
            scratch_shapes=[pltpu.VMEM((LOCAL_ROWS, D_MODEL), F32),
                            pltpu.VMEM((EXPERT_ROWS, D_MODEL), F32),
                            pltpu.SemaphoreType.DMA(()), pltpu.SemaphoreType.DMA(())]),
        out_shape=jax.ShapeDtypeStruct((n_sorted, D_MODEL), F32),
        compiler_params=pltpu.CompilerParams(
            dimension_semantics=("arbitrary",), vmem_limit_bytes=VMEM_LIMIT_BYTES),
        name="dispatch",
    )(segs, pads, h2, info)


def _expert_kernel(tile_blk, tile_e, n_act, x_ref, w1_ref, w3_ref, w2_ref, y_ref):
    active = pl.program_id(0) < n_act[0]

    @pl.when(active)
    def _():
        h = x_ref[...].astype(BF16)
        a = _dot(h, w1_ref[0])
        u = (a * _sigmoid(a) * _dot(h, w3_ref[0])).astype(BF16)
        y_ref[...] = _dot(u, w2_ref[0])

    @pl.when(jnp.logical_not(active))
    def _():
        y_ref[...] = jnp.zeros_like(y_ref)


def _experts(xs, tile_blk, tile_e, n_act, w1, w3, w2):
    n_sorted = xs.shape[0]
    tm = EXPERT_ROWS
    f = w1.shape[-1]
    return pl.pallas_call(
        _expert_kernel,
        grid_spec=pltpu.PrefetchScalarGridSpec(
            num_scalar_prefetch=3,
            grid=(n_sorted // tm,),
            in_specs=[pl.BlockSpec((tm, D_MODEL), lambda i, b, e, n: (b[i], 0)),
                      pl.BlockSpec((1, D_MODEL, f), lambda i, b, e, n: (e[i], 0, 0)),
                      pl.BlockSpec((1, D_MODEL, f), lambda i, b, e, n: (e[i], 0, 0)),
                      pl.BlockSpec((1, f, D_MODEL), lambda i, b, e, n: (e[i], 0, 0))],
            out_specs=pl.BlockSpec((tm, D_MODEL), lambda i, b, e, n: (i, 0))),
        out_shape=jax.ShapeDtypeStruct((n_sorted, D_MODEL), F32),
        compiler_params=pltpu.CompilerParams(
            dimension_semantics=("arbitrary",), vmem_limit_bytes=VMEM_LIMIT_BYTES),
        name="experts",
    )(tile_blk, tile_e, n_act, xs, w1, w3, w2)


def _combine_kernel(final_norm, n_seg, seg_ref, x_ref, mod_ref, info_ref, fg_ref, ys_ref, out_ref,
                    ybuf, sem):
    i = pl.program_id(0)

    @pl.when(i == 0)
    def _():
        ybuf[...] = jnp.zeros_like(ybuf)

    _for_each_segment_chunk(seg_ref, n_seg, i,
                            lambda loc, srt: _chunk_copy(ys_ref, srt, ybuf, loc, sem).start())
    _for_each_segment_chunk(seg_ref, n_seg, i,
                            lambda loc, srt: _chunk_copy(ys_ref, srt, ybuf, loc, sem).wait())
    info = info_ref[...]
    sel = _selection(info, info[:, INFO_PROB:INFO_PROB + 1], info[:, INFO_PROB + 1:INFO_PROB + 2])
    y = _dot(sel, ybuf[...].astype(BF16))
    out = x_ref[...] + mod_ref[0][5:6] * y
    if final_norm:
        ms = jnp.mean(out * out, axis=-1, keepdims=True)
        out = out * lax.rsqrt(ms + EPS) * fg_ref[...]
    out_ref[...] = out


def _combine(x2, mod, info, ys, segs, final_g, final_norm, rows_per_batch):
    n = x2.shape[0]
    tm = ROUTE_ROWS
    per_b = rows_per_batch // tm
    n_seg = n // tm * N_EXPERTS
    return pl.pallas_call(
        functools.partial(_combine_kernel, final_norm, n_seg),
        grid_spec=pltpu.PrefetchScalarGridSpec(
            num_scalar_prefetch=1,
            grid=(n // tm,),
            in_specs=[pl.BlockSpec((tm, D_MODEL), lambda i, s: (i, 0)),
                      pl.BlockSpec((1, 6, D_MODEL), lambda i, s: (i // per_b, 0, 0)),
                      pl.BlockSpec((tm, LANES), lambda i, s: (i, 0)),
                      pl.BlockSpec((1, D_MODEL), lambda i, s: (0, 0)),
                      pl.BlockSpec(memory_space=pl.ANY)],
            out_specs=pl.BlockSpec((tm, D_MODEL), lambda i, s: (i, 0)),
            scratch_shapes=[pltpu.VMEM((LOCAL_ROWS, D_MODEL), F32), pltpu.SemaphoreType.DMA(())]),
        out_shape=jax.ShapeDtypeStruct(x2.shape, F32),
        compiler_params=pltpu.CompilerParams(
            dimension_semantics=("arbitrary",), vmem_limit_bytes=VMEM_LIMIT_BYTES),
        name="combine",
    )(segs, x2, mod, info, final_g, ys)


def _moe(x, mod, g, router_w, w1, w3, w2, final_g, final_norm):
    bsz, seq, _ = x.shape
    n = bsz * seq
    tm = EXPERT_ROWS
    n_tok_tiles = n // ROUTE_ROWS
    x2 = x.reshape(n, D_MODEL)
    h2, info, cnt = _route(x2, mod, g, router_w, seq)
    counts = cnt.reshape(n_tok_tiles, 8, LANES)[:, 0, :N_EXPERTS].astype(jnp.int32)
    seg_rows = (counts + (SEG_ALIGN - 1)) // SEG_ALIGN * SEG_ALIGN
    group_rows = jnp.sum(seg_rows, axis=0)
    tiles = (group_rows + (tm - 1)) // tm
    tile_end = jnp.cumsum(tiles)
    group_off = (tile_end - tiles) * tm
    seg_sorted = group_off[None, :] + jnp.cumsum(seg_rows, axis=0) - seg_rows
    seg_local = jnp.cumsum(seg_rows, axis=1) - seg_rows
    segs = jnp.concatenate([seg_sorted.reshape(-1), seg_local.reshape(-1),
                            (seg_rows // SEG_ALIGN).reshape(-1)]).astype(jnp.int32)
    max_rows = 2 * n + n_tok_tiles * N_EXPERTS * (SEG_ALIGN - 1)
    n_tiles = -(-max_rows // tm) + N_EXPERTS
    n_act = tile_end[-1]
    t = jnp.minimum(jnp.arange(n_tiles, dtype=jnp.int32), n_act - 1)
    tile_e = jnp.sum(t[:, None] >= tile_end[None, :], axis=1).astype(jnp.int32)
    pad_lo = group_off + group_rows
    pads = jnp.concatenate([pad_lo, (tile_end * tm - pad_lo) // SEG_ALIGN,
                            n_act[None]]).astype(jnp.int32)
    xs = _dispatch(h2, info, segs, pads, n_tiles * tm)
    ys = _experts(xs, t, tile_e, n_act.reshape(1).astype(jnp.int32), w1, w3, w2)
    out = _combine(x2, mod, info, ys, segs, final_g, final_norm, seq)
    return out.reshape(x.shape)


def _ffn_kernel(final_norm, n_groups, x_ref, mod_ref, g_ref, w1_ref, w3_ref, w2_ref, fg_ref, y_ref,
                h_scr, acc_scr):
    e = pl.program_id(2)
    m = mod_ref[0]

    @pl.when(e == 0)
    def _():
        h_scr[...] = _rms_mod(x_ref[0], g_ref[...], m[4:5], m[3:4]).astype(BF16)
        acc_scr[...] = jnp.zeros_like(acc_scr)

    h = h_scr[...]
    a = _dot(h, w1_ref[0])
    u = (a * _sigmoid(a) * _dot(h, w3_ref[0])).astype(BF16)
    acc_scr[...] += _dot(u, w2_ref[0])

    @pl.when(e == n_groups - 1)
    def _():
        out = x_ref[0] + m[5:6] * acc_scr[...]
        if final_norm:
            ms = jnp.mean(out * out, axis=-1, keepdims=True)
            out = out * lax.rsqrt(ms + EPS) * fg_ref[...]
        y_ref[0] = out


def _ffn(x, mod, g, w1, w3, w2, final_g, final_norm):
    bsz, seq, _ = x.shape
    n_groups, _, f = w1.shape
    tm = 1024
    row = pl.BlockSpec((1, tm, D_MODEL), lambda b, i, e: (b, i, 0))
    vec = pl.BlockSpec((1, D_MODEL), lambda b, i, e: (0, 0))
    return pl.pallas_call(
        functools.partial(_ffn_kernel, final_norm, n_groups),
        grid=(bsz, seq // tm, n_groups),
        in_specs=[row,
                  pl.BlockSpec((1, 6, D_MODEL), lambda b, i, e: (b, 0, 0)),
                  vec,
                  pl.BlockSpec((1, D_MODEL, f), lambda b, i, e: (e, 0, 0)),
                  pl.BlockSpec((1, D_MODEL, f), lambda b, i, e: (e, 0, 0)),
                  pl.BlockSpec((1, f, D_MODEL), lambda b, i, e: (e, 0, 0)),
                  vec],
        out_specs=row,
        out_shape=jax.ShapeDtypeStruct(x.shape, F32),
        scratch_shapes=[pltpu.VMEM((tm, D_MODEL), BF16), pltpu.VMEM((tm, D_MODEL), F32)],
        compiler_params=pltpu.CompilerParams(
            dimension_semantics=("arbitrary", "arbitrary", "arbitrary"),
            vmem_limit_bytes=VMEM_LIMIT_BYTES),
        name="ffn",
    )(x, mod, g, w1, w3, w2, final_g)


def kernel(x, c, positions, ada_w, ada_b, norm_mix_g, norm_ffn_g, w_in, gla_w_alpha, gla_b_alpha,
           gla_norm_g, ret_gn_g, ret_gn_b, w_out, ffn_w1, ffn_w3, ffn_w2, router_w, moe_w1,
           moe_w3, moe_w2, final_g):
    bsz = x.shape[0]
    mod_all = _ada_mod(c, ada_w, ada_b).reshape(DEPTH, bsz, 6, D_MODEL)
    cos, sin = _rope_tables(positions)
    ga_lo = 2 * GLA_QK + 2 * GLA_V
    fg = final_g.reshape(1, D_MODEL)
    f_half = ffn_w1.shape[-1] // 2
    for layer in range(DEPTH):
        mod = mod_all[layer]
        w = w_in[layer]
        w_main = jnp.concatenate([w[:, :ga_lo], w[:, ga_lo + GLA_RANK:]], axis=1).astype(BF16)
        w_ga = jnp.zeros((D_MODEL, LANES), F32).at[:, :GLA_RANK].set(
            w[:, ga_lo:ga_lo + GLA_RANK]).astype(BF16)
        proj, ga = _in_proj(x, mod, norm_mix_g[layer].reshape(1, D_MODEL), w_main, w_ga)
        o = _attention(proj, ga, cos, sin, gla_w_alpha[layer], gla_b_alpha[layer],
                       gla_norm_g[layer], ret_gn_g[layer], ret_gn_b[layer])
        x = _out_proj(o, x, mod, w_out[layer].astype(BF16))
        g_ffn = norm_ffn_g[layer].reshape(1, D_MODEL)
        i = layer // 2
        last = layer == DEPTH - 1
        if layer % 2 == 0:
            w1 = ffn_w1[i].reshape(D_MODEL, 2, f_half).transpose(1, 0, 2).astype(BF16)
            w3 = ffn_w3[i].reshape(D_MODEL, 2, f_half).transpose(1, 0, 2).astype(BF16)
            w2 = ffn_w2[i].reshape(2, f_half, D_MODEL).astype(BF16)
            x = _ffn(x, mod, g_ffn, w1, w3, w2, fg, last)
        else:
            x = _moe(x, mod, g_ffn, router_w[i], moe_w1[i].astype(BF16), moe_w3[i].astype(BF16),
                     moe_w2[i].astype(BF16), fg, last)
    return x
```

```python
import functools

import numpy as np
import jax
import jax.numpy as jnp
from jax import lax
from jax.experimental import pallas as pl
from jax.experimental.pallas import tpu as pltpu

F32 = jnp.float32
BF16 = jnp.bfloat16

D_MODEL = 1024
DEPTH = 4
CHUNK = 64
GLA_HEADS = 4
GLA_DK = 64
GLA_DV = 128
GLA_RANK = 16
GLA_TAU = 16.0
RET_HEADS = 4
RET_DK = 128
RET_DV = 128
ROPE_BASE = 10000.0
N_EXPERTS = 8
EPS = 1e-6
GLA_QK = GLA_HEADS * GLA_DK
GLA_V = GLA_HEADS * GLA_DV
RET_QK = RET_HEADS * RET_DK
RET_V = RET_HEADS * RET_DV
MIX_WIDTH = GLA_V + RET_V

LANES = 128
VMEM_LIMIT_BYTES = 56 * 1024 * 1024

PROJ_WIDTH = 2 * GLA_QK + 2 * GLA_V + 2 * RET_QK + 2 * RET_V
OFF_GQ = 0
OFF_GK = OFF_GQ + GLA_QK
OFF_GV = OFF_GK + GLA_QK
OFF_GR = OFF_GV + GLA_V
OFF_RQ = OFF_GR + GLA_V
OFF_RK = OFF_RQ + RET_QK
OFF_RV = OFF_RK + RET_QK
OFF_RG = OFF_RV + RET_V

ATTN_ROWS = 256
ATTN_CHUNKS = ATTN_ROWS // CHUNK

MOD_SH1, MOD_SC1, MOD_G1, MOD_SH2, MOD_SC2, MOD_G2 = range(6)


def _sigmoid(x):
    return 1.0 / (1.0 + jnp.exp(-x))


def _rms_mod(x, g, sc, sh):
    ms = jnp.mean(x * x, axis=-1, keepdims=True)
    return (x * lax.rsqrt(ms + EPS)) * g * (1.0 + sc) + sh


def _dot(a, b):
    return jnp.dot(a, b, preferred_element_type=F32)


def _dot_nt(a, b):
    return lax.dot_general(a, b, (((1,), (1,)), ((), ())), preferred_element_type=F32)


def _dot_tn(a, b):
    return lax.dot_general(a, b, (((0,), (0,)), ((), ())), preferred_element_type=F32)


def _row(m, r):
    return m[r:r + 1]


def _layer_spec(shape, layer):
    zeros = (0,) * (len(shape) - 1)
    return pl.BlockSpec((1,) + tuple(shape[1:]), lambda *_: (layer,) + zeros)


def _ada_kernel(c_ref, w_ref, b_ref, o_ref):
    c = c_ref[...]
    cond = c * _sigmoid(c)
    o_ref[0] = jnp.dot(cond, w_ref[0], precision=lax.Precision.HIGHEST,
                       preferred_element_type=F32) + b_ref[0]


def _ada_mod(c, ada_w, ada_b):
    bsz = c.shape[0]
    tn = 1024
    n_out = ada_w.shape[-1]
    return pl.pallas_call(
        _ada_kernel,
        grid=(DEPTH, n_out // tn),
        in_specs=[
            pl.BlockSpec((bsz, D_MODEL), lambda l, j: (0, 0)),
            pl.BlockSpec((1, D_MODEL, tn), lambda l, j: (l, 0, j)),
            pl.BlockSpec((1, 1, tn), lambda l, j: (l, 0, j)),
        ],
        out_specs=pl.BlockSpec((1, bsz, tn), lambda l, j: (l, 0, j)),
        out_shape=jax.ShapeDtypeStruct((DEPTH, bsz, n_out), F32),
        name="ada_mod",
    )(c, ada_w, ada_b.reshape(DEPTH, 1, n_out))


def _rope_kernel(pos_ref, invf_ref, sign_ref, cos_ref, sin_ref):
    ang = pos_ref[...] * invf_ref[...]
    cos_ref[...] = jnp.cos(ang)
    sin_ref[...] = jnp.sin(ang) * sign_ref[...]


def _rope_tables(positions):
    n = positions.size
    half = RET_DK // 2
    inv_freq = ROPE_BASE ** (-jnp.arange(half, dtype=F32) / half)
    invf = jnp.concatenate([inv_freq, inv_freq]).reshape(1, RET_DK)
    sign = jnp.concatenate([-jnp.ones((half,), F32), jnp.ones((half,), F32)]).reshape(1, RET_DK)
    pos = positions.astype(F32).reshape(n, 1)
    tm = 2048
    row = pl.BlockSpec((1, RET_DK), lambda i: (0, 0))
    out = pl.BlockSpec((tm, RET_DK), lambda i: (i, 0))
    return pl.pallas_call(
        _rope_kernel,
        grid=(n // tm,),
        in_specs=[pl.BlockSpec((tm, 1), lambda i: (i, 0)), row, row],
        out_specs=[out, out],
        out_shape=[jax.ShapeDtypeStruct((n, RET_DK), F32)] * 2,
        name="rope_tables",
    )(pos, invf, sign)


def _inproj_kernel(x_ref, mod_ref, g_ref, w_ref, wga_ref, proj_ref, ga_ref):
    m = mod_ref[0, 0]
    h = _rms_mod(x_ref[0], g_ref[0], _row(m, MOD_SC1), _row(m, MOD_SH1)).astype(BF16)
    proj_ref[0] = _dot(h, w_ref[0])
    ga_ref[0] = _dot(h, wga_ref[0])


def _in_proj(layer, x, mod_all, g_all, w_main_all, w_ga_all):
    bsz, seq, _ = x.shape
    tm = 512
    return pl.pallas_call(
        _inproj_kernel,
        grid=(bsz, seq // tm),
        in_specs=[
            pl.BlockSpec((1, tm, D_MODEL), lambda b, i: (b, i, 0)),
            pl.BlockSpec((1, 1, 6, D_MODEL), lambda b, i: (layer, b, 0, 0)),
            _layer_spec(g_all.shape, layer),
            _layer_spec(w_main_all.shape, layer),
            _layer_spec(w_ga_all.shape, layer),
        ],
        out_specs=[
            pl.BlockSpec((1, tm, PROJ_WIDTH), lambda b, i: (b, i, 0)),
            pl.BlockSpec((1, tm, LANES), lambda b, i: (b, i, 0)),
        ],
        out_shape=[
            jax.ShapeDtypeStruct((bsz, seq, PROJ_WIDTH), F32),
            jax.ShapeDtypeStruct((bsz, seq, LANES), F32),
        ],
        compiler_params=pltpu.CompilerParams(
            dimension_semantics=("arbitrary", "arbitrary"),
            vmem_limit_bytes=VMEM_LIMIT_BYTES),
        name="in_proj",
    )(x, mod_all, g_all, w_main_all, w_ga_all)


def _attn_consts():
    r = ATTN_ROWS
    t = np.arange(r)
    same = (t[:, None] // CHUNK) == (t[None, :] // CHUNK)
    causal = t[:, None] >= t[None, :]
    tri = (same & causal).astype(np.float32)
    m_fwd = tri
    m_bwd = (same & ~causal).astype(np.float32)
    gam = 1.0 - 2.0 ** (-5.0 - np.arange(RET_HEADS, dtype=np.float64))
    lg = np.log(gam)
    dist = (t[:, None] - t[None, :]).astype(np.float64)
    d_ret = np.where(causal[None], np.exp(lg[:, None, None] * dist[None]),
                     np.where(same[None], np.exp(-lg[:, None, None] * dist[None]), 0.0))
    qdec = np.repeat(np.exp(lg[None, :] * (t[:, None] + 1.0)), RET_DK, axis=1)
    kdec = np.repeat(np.exp(lg[None, :] * (r - 1.0 - t[:, None])), RET_DK, axis=1)
    step_decay = [float(np.exp(lg[h] * r)) for h in range(RET_HEADS)]
    hmask = np.zeros((GLA_HEADS, 1, GLA_QK), np.float32)
    for h in range(GLA_HEADS):
        hmask[h, 0, h * GLA_DK:(h + 1) * GLA_DK] = 1.0
    vv = np.arange(GLA_V)[:, None] // GLA_DV
    kk = np.arange(GLA_QK)[None, :] // GLA_DK
    bdiag = (vv == kk).astype(np.float32)
    return dict(tri=tri, m_fwd=m_fwd, m_bwd=m_bwd, d_ret=d_ret.astype(np.float32),
                qdec=qdec.astype(np.float32), kdec=kdec.astype(np.float32),
                step_decay=step_decay, hmask=hmask, bdiag=bdiag)


def _attn_kernel(step_decay, proj_ref, ga_ref, cos_ref, sin_ref, walpha_ref, balpha_ref,
                 glag_ref, gng_ref, gnb_ref, tri_ref, mfwd_ref, mbwd_ref, dret_ref,
                 qdec_ref, kdec_ref, hmask_ref, bdiag_ref, o_ref, st_g, st_r):
    r = ATTN_ROWS

    @pl.when(pl.program_id(1) == 0)
    def _():
        st_g[...] = jnp.zeros_like(st_g)
        st_r[...] = jnp.zeros_like(st_r)

    z = _dot(ga_ref[0].astype(BF16), walpha_ref[0]) + balpha_ref[0]
    log_a = (jnp.minimum(z, 0.0) - jnp.log(1.0 + jnp.exp(-jnp.abs(z)))) * (1.0 / GLA_TAU)
    hi = log_a.astype(BF16)
    r1 = log_a - hi.astype(F32)
    mid = r1.astype(BF16)
    lo = (r1 - mid.astype(F32)).astype(BF16)
    tri = tri_ref[...]
    b = _dot(tri, hi) + _dot(tri, mid) + _dot(tri, lo)
    b3 = b.reshape(ATTN_CHUNKS, CHUNK, GLA_QK)
    b_last = b3[:, CHUNK - 1:CHUNK, :]
    k_upd_scale = jnp.exp(b_last - b3).reshape(r, GLA_QK)
    eb = jnp.exp(b)
    enb = jnp.exp(-b)
    q = proj_ref[0, :, OFF_GQ:OFF_GQ + GLA_QK] * (GLA_DK ** -0.5)
    k = proj_ref[0, :, OFF_GK:OFF_GK + GLA_QK]
    v_bf = proj_ref[0, :, OFF_GV:OFF_GV + GLA_V].astype(BF16)
    q_f = q * eb
    q_b = q * enb
    k_f = (k * enb).astype(BF16)
    k_b = (k * eb).astype(BF16)
    k_u = (k * k_upd_scale).astype(BF16)
    q_f_bf = q_f.astype(BF16)

    bdiag = bdiag_ref[...]
    inter = []
    for g in range(ATTN_CHUNKS):
        rows = slice(g * CHUNK, (g + 1) * CHUNK)
        st = st_g[...]
        inter.append(_dot_nt(q_f_bf[rows], st.astype(BF16)))
        upd_t = _dot_tn(v_bf[rows], k_u[rows]) * bdiag
        st_g[...] = st * jnp.exp(b_last[g]) + upd_t
    o_inter = jnp.concatenate(inter, axis=0)

    m_fwd = mfwd_ref[...] > 0.5
    m_bwd = mbwd_ref[...] > 0.5
    for h in range(GLA_HEADS):
        hm = hmask_ref[h]
        s_f = _dot_nt((q_f * hm).astype(BF16), k_f)
        s_b = _dot_nt((q_b * hm).astype(BF16), k_b)
        sc = jnp.where(m_fwd, s_f, jnp.where(m_bwd, s_b, 0.0)).astype(BF16)
        lanes = slice(h * GLA_DV, (h + 1) * GLA_DV)
        o_h = _dot(sc, v_bf[:, lanes]) + o_inter[:, lanes]
        o_h = o_h * lax.rsqrt(jnp.mean(o_h * o_h, axis=-1, keepdims=True) + EPS)
        gate = proj_ref[0, :, OFF_GR + h * GLA_DV:OFF_GR + (h + 1) * GLA_DV]
        o_h = o_h * glag_ref[0, :, lanes] * (gate * _sigmoid(gate))
        o_ref[0, :, lanes] = o_h.astype(o_ref.dtype)

    cos = cos_ref[0]
    sin = sin_ref[0]
    for h in range(RET_HEADS):
        lanes = slice(h * RET_DK, (h + 1) * RET_DK)
        qh = proj_ref[0, :, OFF_RQ + h * RET_DK:OFF_RQ + (h + 1) * RET_DK]
        kh = proj_ref[0, :, OFF_RK + h * RET_DK:OFF_RK + (h + 1) * RET_DK]
        vh = proj_ref[0, :, OFF_RV + h * RET_DV:OFF_RV + (h + 1) * RET_DV].astype(BF16)
        qh = (qh * cos + pltpu.roll(qh, RET_DK // 2, axis=1) * sin) * (RET_DK ** -0.5)
        kh = kh * cos + pltpu.roll(kh, RET_DK // 2, axis=1) * sin
        s = _dot_nt(qh.astype(BF16), kh.astype(BF16)) * dret_ref[h]
        st = st_r[h]
        o_h = _dot(s.astype(BF16), vh) + _dot((qh * qdec_ref[:, lanes]).astype(BF16), st.astype(BF16))
        st_r[h] = st * step_decay[h] + _dot_tn((kh * kdec_ref[:, lanes]).astype(BF16), vh)
        mu = jnp.mean(o_h, axis=-1, keepdims=True)
        d = o_h - mu
        var = jnp.mean(d * d, axis=-1, keepdims=True)
        o_h = d * lax.rsqrt(var + EPS) * gng_ref[0, :, lanes] + gnb_ref[0, :, lanes]
        gate = proj_ref[0, :, OFF_RG + h * RET_DV:OFF_RG + (h + 1) * RET_DV]
        o_h = o_h * (gate * _sigmoid(gate))
        o_ref[0, :, GLA_V + h * RET_DV:GLA_V + (h + 1) * RET_DV] = o_h.astype(o_ref.dtype)


def _attention(layer, proj, ga, cos, sin, w_alpha_all, b_alpha_all, gla_g_all, gn_g_all, gn_b_all):
    bsz, seq, _ = proj.shape
    r = ATTN_ROWS
    c = _attn_consts()
    const2 = lambda shape: pl.BlockSpec(shape, lambda b, i: (0,) * len(shape))
    row_blk = lambda w: pl.BlockSpec((1, r, w), lambda b, i: (b, i, 0))
    per_layer = lambda a: _layer_spec(a.shape, layer)
    return pl.pallas_call(
        functools.partial(_attn_kernel, c["step_decay"]),
        grid=(bsz, seq // r),
        in_specs=[
            row_blk(PROJ_WIDTH), row_blk(LANES), row_blk(RET_DK), row_blk(RET_DK),
            per_layer(w_alpha_all), per_layer(b_alpha_all),
            per_layer(gla_g_all), per_layer(gn_g_all), per_layer(gn_b_all),
            const2((r, r)), const2((r, r)), const2((r, r)), const2((RET_HEADS, r, r)),
            const2((r, RET_QK)), const2((r, RET_QK)),
            const2((GLA_HEADS, 1, GLA_QK)), const2((GLA_V, GLA_QK)),
        ],
        out_specs=row_blk(MIX_WIDTH),
        out_shape=jax.ShapeDtypeStruct((bsz, seq, MIX_WIDTH), BF16),
        scratch_shapes=[pltpu.VMEM((GLA_V, GLA_QK), F32),
                        pltpu.VMEM((RET_HEADS, RET_DK, RET_DV), F32)],
        compiler_params=pltpu.CompilerParams(
            dimension_semantics=("arbitrary", "arbitrary"),
            vmem_limit_bytes=VMEM_LIMIT_BYTES),
        name="mixer_core",
    )(proj, ga, cos.reshape(bsz, seq, RET_DK), sin.reshape(bsz, seq, RET_DK),
      w_alpha_all, b_alpha_all, gla_g_all, gn_g_all, gn_b_all,
      jnp.asarray(c["tri"], BF16), jnp.asarray(c["m_fwd"]), jnp.asarray(c["m_bwd"]),
      jnp.asarray(c["d_ret"]), jnp.asarray(c["qdec"]), jnp.asarray(c["kdec"]),
      jnp.asarray(c["hmask"]), jnp.asarray(c["bdiag"]))


def _outproj_kernel(o_ref, x_ref, mod_ref, w_ref, y_ref):
    y_ref[0] = x_ref[0] + _row(mod_ref[0, 0], MOD_G1) * _dot(o_ref[0], w_ref[0])


def _out_proj(layer, o, x, mod_all, w_out_all):
    bsz, seq, _ = x.shape
    tm = 512
    row = lambda w: pl.BlockSpec((1, tm, w), lambda b, i: (b, i, 0))
    return pl.pallas_call(
        _outproj_kernel,
        grid=(bsz, seq // tm),
        in_specs=[row(MIX_WIDTH), row(D_MODEL),
                  pl.BlockSpec((1, 1, 6, D_MODEL), lambda b, i: (layer, b, 0, 0)),
                  _layer_spec(w_out_all.shape, layer)],
        out_specs=row(D_MODEL),
        out_shape=jax.ShapeDtypeStruct(x.shape, F32),
        compiler_params=pltpu.CompilerParams(
            dimension_semantics=("arbitrary", "arbitrary"),
            vmem_limit_bytes=VMEM_LIMIT_BYTES),
        name="out_proj",
    )(o, x, mod_all, w_out_all)


ROUTE_ROWS = 512
EXPERT_ROWS = 512
SEG_ALIGN = 8
LOCAL_ROWS = -(-(2 * ROUTE_ROWS + N_EXPERTS * (SEG_ALIGN - 1)) // LANES) * LANES
PIECE_BITS = (max(ROUTE_ROWS, EXPERT_ROWS) // SEG_ALIGN).bit_length()
INFO_POS, INFO_PROB = 0, 2


def _route_kernel(x_ref, mod_ref, g_ref, rw_ref, tril_ref, upper_ref, h_ref, info_ref, cnt_ref):
    m = mod_ref[0, 0]
    h = _rms_mod(x_ref[...], g_ref[0], _row(m, MOD_SC2), _row(m, MOD_SH2))
    h_hi = h.astype(BF16)
    h_ref[...] = h_hi
    h_lo = (h - h_hi.astype(F32)).astype(BF16)
    hh = _dot(h_hi, rw_ref[...])
    logits = hh[:, :LANES] + hh[:, LANES:] + _dot(h_lo, rw_ref[:, :LANES])
    lane = lax.broadcasted_iota(jnp.int32, logits.shape, 1)
    neg = jnp.float32(-jnp.inf)
    lg = jnp.where(lane < N_EXPERTS, logits, neg)
    m1 = jnp.max(lg, axis=-1, keepdims=True)
    i1 = jnp.min(jnp.where(lg == m1, lane, LANES), axis=-1, keepdims=True)
    lg2 = jnp.where(lane == i1, neg, lg)
    m2 = jnp.max(lg2, axis=-1, keepdims=True)
    i2 = jnp.min(jnp.where(lg2 == m2, lane, LANES), axis=-1, keepdims=True)
    e2 = jnp.exp(m2 - m1)
    p1 = 1.0 / (1.0 + e2)
    p2 = e2 * p1
    sel1 = lane == i1
    sel2 = lane == i2
    onehot = jnp.where(sel1, 1.0, 0.0) + jnp.where(sel2, 1.0, 0.0)
    incl = _dot(tril_ref[...], onehot.astype(BF16))
    counts = incl[ROUTE_ROWS - 1:ROUTE_ROWS]
    seg_units = jnp.floor((counts + (SEG_ALIGN - 1)) * (1.0 / SEG_ALIGN))
    seg_units8 = jnp.broadcast_to(seg_units, (8, LANES)).astype(BF16)
    seg_off = _dot(seg_units8, upper_ref[...])[0:1] * SEG_ALIGN
    pos = incl - onehot + seg_off
    pos1 = jnp.sum(jnp.where(sel1, pos, 0.0), axis=-1, keepdims=True)
    pos2 = jnp.sum(jnp.where(sel2, pos, 0.0), axis=-1, keepdims=True)
    cnt_ref[...] = jnp.broadcast_to(counts, (8, LANES))
    rec = jnp.zeros(logits.shape, F32)
    for lane_id, val in ((INFO_POS, pos1), (INFO_POS + 1, pos2), (INFO_PROB, p1), (INFO_PROB + 1, p2)):
        rec = jnp.where(lane == lane_id, val, rec)
    info_ref[...] = rec


def _route(layer, x2, mod_all, g_all, router_w, rows_per_batch):
    n = x2.shape[0]
    tm = ROUTE_ROWS
    per_b = rows_per_batch // tm
    w_hi = router_w.astype(BF16)
    w_lo = (router_w - w_hi.astype(F32)).astype(BF16)
    pad = ((0, 0), (0, LANES - N_EXPERTS))
    rw = jnp.concatenate([jnp.pad(w_hi, pad), jnp.pad(w_lo, pad)], axis=1)
    tril = jnp.asarray(np.tril(np.ones((tm, tm), np.float32)), BF16)
    upper = jnp.asarray(np.triu(np.ones((LANES, LANES), np.float32), 1), BF16)
    return pl.pallas_call(
        _route_kernel,
        grid=(n // tm,),
        in_specs=[pl.BlockSpec((tm, D_MODEL), lambda i: (i, 0)),
                  pl.BlockSpec((1, 1, 6, D_MODEL), lambda i: (layer, i // per_b, 0, 0)),
                  _layer_spec(g_all.shape, layer),
                  pl.BlockSpec((D_MODEL, 2 * LANES), lambda i: (0, 0)),
                  pl.BlockSpec((tm, tm), lambda i: (0, 0)),
                  pl.BlockSpec((LANES, LANES), lambda i: (0, 0))],
        out_specs=[pl.BlockSpec((tm, D_MODEL), lambda i: (i, 0)),
                   pl.BlockSpec((tm, LANES), lambda i: (i, 0)),
                   pl.BlockSpec((8, LANES), lambda i: (i, 0))],
        out_shape=[jax.ShapeDtypeStruct((n, D_MODEL), BF16),
                   jax.ShapeDtypeStruct((n, LANES), F32),
                   jax.ShapeDtypeStruct((n // tm * 8, LANES), F32)],
        compiler_params=pltpu.CompilerParams(
            dimension_semantics=("arbitrary",), vmem_limit_bytes=VMEM_LIMIT_BYTES),
        name="route",
    )(x2, mod_all, g_all, rw, tril, upper)


def _rows(ref, row, n_rows):
    return ref.at[pl.ds(pl.multiple_of(row, SEG_ALIGN), n_rows)]


def _for_each_piece(n_units, fn):
    for bit in reversed(range(PIECE_BITS)):
        covered = (n_units >> (bit + 1)) << (bit + 1)

        @pl.when(((n_units >> bit) & 1) == 1)
        def _():
            fn(covered * SEG_ALIGN, SEG_ALIGN << bit)


def _for_each_segment_piece(seg_ref, n_seg, tile, fn):
    for e in range(N_EXPERTS):
        s = tile * N_EXPERTS + e
        sorted_row = seg_ref[s]
        local_row = seg_ref[n_seg + s]
        _for_each_piece(seg_ref[2 * n_seg + s],
                        lambda first, n_rows: fn(local_row + first, sorted_row + first, n_rows))


def _selection(info, coeff1, coeff2):
    lane = lax.broadcasted_iota(jnp.int32, (info.shape[0], LOCAL_ROWS), 1)
    pos1 = info[:, INFO_POS:INFO_POS + 1].astype(jnp.int32)
    pos2 = info[:, INFO_POS + 1:INFO_POS + 2].astype(jnp.int32)
    return jnp.where(lane == pos1, coeff1, jnp.where(lane == pos2, coeff2, 0.0)).astype(BF16)


def _dispatch_kernel(n_seg, seg_ref, pad_ref, h_ref, info_ref, xs_ref, sorted_scr, zero_scr, sems, pad_sem):
    i = pl.program_id(0)
    n_steps = pl.num_programs(0)
    slot = i % 2

    def move(tile, buf_slot, op):
        def piece(local_row, sorted_row, n_rows):
            cp = pltpu.make_async_copy(_rows(sorted_scr.at[buf_slot], local_row, n_rows),
                                       _rows(xs_ref, sorted_row, n_rows), sems.at[buf_slot])
            op(cp)
        _for_each_segment_piece(seg_ref, n_seg, tile, piece)

    @pl.when(i == 0)
    def _():
        zero_scr[...] = jnp.zeros_like(zero_scr)
        for op in (lambda cp: cp.start(), lambda cp: cp.wait()):
            for e in range(N_EXPERTS):
                lo = pad_ref[e]
                _for_each_piece(pad_ref[N_EXPERTS + e],
                                lambda first, n_rows: op(pltpu.make_async_copy(
                                    zero_scr.at[pl.ds(0, n_rows)], _rows(xs_ref, lo + first, n_rows),
                                    pad_sem)))

        def tile_copy(j):
            return pltpu.make_async_copy(zero_scr, _rows(xs_ref, j * EXPERT_ROWS, EXPERT_ROWS), pad_sem)

        def fill_tile(j, carry):
            tile_copy(j).start()
            return carry

        def drain_tile(j, carry):
            tile_copy(j).wait()
            return carry

        n_tiles = xs_ref.shape[0] // EXPERT_ROWS
        lax.fori_loop(pad_ref[2 * N_EXPERTS], n_tiles, fill_tile, 0)
        lax.fori_loop(pad_ref[2 * N_EXPERTS], n_tiles, drain_tile, 0)

    @pl.when(i >= 2)
    def _():
        move(i - 2, slot, lambda cp: cp.wait())

    sel = _selection(info_ref[...], 1.0, 1.0)
    sorted_scr[slot] = _dot_tn(sel, h_ref[...])
    move(i, slot, lambda cp: cp.start())

    @pl.when(i == n_steps - 1)
    def _():
        @pl.when(i >= 1)
        def _():
            move(i - 1, 1 - slot, lambda cp: cp.wait())
        move(i, slot, lambda cp: cp.wait())


def _dispatch(h2, info, segs, pads, n_sorted):
    n = h2.shape[0]
    tm = ROUTE_ROWS
    n_seg = n // tm * N_EXPERTS
    return pl.pallas_call(
        functools.partial(_dispatch_kernel, n_seg),
        grid_spec=pltpu.PrefetchScalarGridSpec(
            num_scalar_prefetch=2,
            grid=(n // tm,),
            in_specs=[pl.BlockSpec((tm, D_MODEL), lambda i, s, p: (i, 0)),
                      pl.BlockSpec((tm, LANES), lambda i, s, p: (i, 0))],
            out_specs=pl.BlockSpec(memory_space=pl.ANY),
            scratch_shapes=[pltpu.VMEM((2, LOCAL_ROWS, D_MODEL), F32),
                            pltpu.VMEM((EXPERT_ROWS, D_MODEL), F32),
                            pltpu.SemaphoreType.DMA((2,)), pltpu.SemaphoreType.DMA(())]),
        out_shape=jax.ShapeDtypeStruct((n_sorted, D_MODEL), F32),
        compiler_params=pltpu.CompilerParams(
            dimension_semantics=("arbitrary",), vmem_limit_bytes=VMEM_LIMIT_BYTES),
        name="dispatch",
    )(segs, pads, h2, info)


def _expert_kernel(tile_blk, tile_e, n_act, x_ref, w1_ref, w3_ref, w2_ref, y_ref):
    active = pl.program_id(0) < n_act[0]

    @pl.when(active)
    def _():
        h = x_ref[...].astype(BF16)
        a = _dot(h, w1_ref[0, 0])
        u = (a * _sigmoid(a) * _dot(h, w3_ref[0, 0])).astype(BF16)
        y_ref[...] = _dot(u, w2_ref[0, 0])

    @pl.when(jnp.logical_not(active))
    def _():
        y_ref[...] = jnp.zeros_like(y_ref)


def _experts(li, xs, tile_blk, tile_e, n_act, w1_all, w3_all, w2_all):
    n_sorted = xs.shape[0]
    tm = EXPERT_ROWS
    f = w1_all.shape[-1]
    return pl.pallas_call(
        _expert_kernel,
        grid_spec=pltpu.PrefetchScalarGridSpec(
            num_scalar_prefetch=3,
            grid=(n_sorted // tm,),
            in_specs=[pl.BlockSpec((tm, D_MODEL), lambda i, b, e, n: (b[i], 0)),
                      pl.BlockSpec((1, 1, D_MODEL, f), lambda i, b, e, n: (li, e[i], 0, 0)),
                      pl.BlockSpec((1, 1, D_MODEL, f), lambda i, b, e, n: (li, e[i], 0, 0)),
                      pl.BlockSpec((1, 1, f, D_MODEL), lambda i, b, e, n: (li, e[i], 0, 0))],
            out_specs=pl.BlockSpec((tm, D_MODEL), lambda i, b, e, n: (i, 0))),
        out_shape=jax.ShapeDtypeStruct((n_sorted, D_MODEL), F32),
        compiler_params=pltpu.CompilerParams(
            dimension_semantics=("arbitrary",), vmem_limit_bytes=VMEM_LIMIT_BYTES),
        name="experts",
    )(tile_blk, tile_e, n_act, xs, w1_all, w3_all, w2_all)


def _combine_kernel(final_norm, n_seg, seg_ref, x_ref, mod_ref, info_ref, fg_ref, ys_ref, out_ref,
                    ybuf, sems):
    i = pl.program_id(0)
    n_steps = pl.num_programs(0)
    slot = i % 2

    def move(tile, buf_slot, op):
        def piece(local_row, sorted_row, n_rows):
            cp = pltpu.make_async_copy(_rows(ys_ref, sorted_row, n_rows),
                                       _rows(ybuf.at[buf_slot], local_row, n_rows), sems.at[buf_slot])
            op(cp)
        _for_each_segment_piece(seg_ref, n_seg, tile, piece)

    @pl.when(i == 0)
    def _():
        ybuf[...] = jnp.zeros_like(ybuf)
        move(0, 0, lambda cp: cp.start())

    @pl.when(i + 1 < n_steps)
    def _():
        move(i + 1, 1 - slot, lambda cp: cp.start())

    move(i, slot, lambda cp: cp.wait())
    info = info_ref[...]
    sel = _selection(info, info[:, INFO_PROB:INFO_PROB + 1], info[:, INFO_PROB + 1:INFO_PROB + 2])
    y = _dot(sel, ybuf[slot].astype(BF16))
    out = x_ref[...] + _row(mod_ref[0, 0], MOD_G2) * y
    if final_norm:
        ms = jnp.mean(out * out, axis=-1, keepdims=True)
        out = out * lax.rsqrt(ms + EPS) * fg_ref[...]
    out_ref[...] = out


def _combine(layer, x2, mod_all, info, ys, segs, final_g, final_norm, rows_per_batch):
    n = x2.shape[0]
    tm = ROUTE_ROWS
    per_b = rows_per_batch // tm
    n_seg = n // tm * N_EXPERTS
    return pl.pallas_call(
        functools.partial(_combine_kernel, final_norm, n_seg),
        grid_spec=pltpu.PrefetchScalarGridSpec(
            num_scalar_prefetch=1,
            grid=(n // tm,),
            in_specs=[pl.BlockSpec((tm, D_MODEL), lambda i, s: (i, 0)),
                      pl.BlockSpec((1, 1, 6, D_MODEL), lambda i, s: (layer, i // per_b, 0, 0)),
                      pl.BlockSpec((tm, LANES), lambda i, s: (i, 0)),
                      pl.BlockSpec((1, D_MODEL), lambda i, s: (0, 0)),
                      pl.BlockSpec(memory_space=pl.ANY)],
            out_specs=pl.BlockSpec((tm, D_MODEL), lambda i, s: (i, 0)),
            scratch_shapes=[pltpu.VMEM((2, LOCAL_ROWS, D_MODEL), F32),
                            pltpu.SemaphoreType.DMA((2,))]),
        out_shape=jax.ShapeDtypeStruct(x2.shape, F32),
        compiler_params=pltpu.CompilerParams(
            dimension_semantics=("arbitrary",), vmem_limit_bytes=VMEM_LIMIT_BYTES),
        name="combine",
    )(segs, x2, mod_all, info, final_g, ys)


def _moe(layer, li, x, mod_all, g_all, router_w, w1_all, w3_all, w2_all, final_g, final_norm):
    bsz, seq, _ = x.shape
    n = bsz * seq
    tm = EXPERT_ROWS
    n_tok_tiles = n // ROUTE_ROWS
    x2 = x.reshape(n, D_MODEL)
    h2, info, cnt = _route(layer, x2, mod_all, g_all, router_w, seq)
    counts = cnt.reshape(n_tok_tiles, 8, LANES)[:, 0, :N_EXPERTS].astype(jnp.int32)
    seg_rows = (counts + (SEG_ALIGN - 1)) // SEG_ALIGN * SEG_ALIGN
    group_rows = jnp.sum(seg_rows, axis=0)
    tiles = (group_rows + (tm - 1)) // tm
    tile_end = jnp.cumsum(tiles)
    group_off = (tile_end - tiles) * tm
    seg_sorted = group_off[None, :] + jnp.cumsum(seg_rows, axis=0) - seg_rows
    seg_local = jnp.cumsum(seg_rows, axis=1) - seg_rows
    segs = jnp.concatenate([seg_sorted.reshape(-1), seg_local.reshape(-1),
                            (seg_rows // SEG_ALIGN).reshape(-1)]).astype(jnp.int32)
    max_rows = 2 * n + n_tok_tiles * N_EXPERTS * (SEG_ALIGN - 1)
    n_tiles = -(-max_rows // tm) + N_EXPERTS
    n_act = tile_end[-1]
    t = jnp.minimum(jnp.arange(n_tiles, dtype=jnp.int32), n_act - 1)
    tile_e = jnp.sum(t[:, None] >= tile_end[None, :], axis=1).astype(jnp.int32)
    pad_lo = group_off + group_rows
    pads = jnp.concatenate([pad_lo, (tile_end * tm - pad_lo) // SEG_ALIGN,
                            n_act[None]]).astype(jnp.int32)
    xs = _dispatch(h2, info, segs, pads, n_tiles * tm)
    ys = _experts(li, xs, t, tile_e, n_act.reshape(1).astype(jnp.int32), w1_all, w3_all, w2_all)
    out = _combine(layer, x2, mod_all, info, ys, segs, final_g, final_norm, seq)
    return out.reshape(x.shape)


def _ffn_kernel(final_norm, n_groups, x_ref, mod_ref, g_ref, w1_ref, w3_ref, w2_ref, fg_ref, y_ref,
                h_scr, acc_scr):
    e = pl.program_id(2)
    m = mod_ref[0, 0]

    @pl.when(e == 0)
    def _():
        h_scr[...] = _rms_mod(x_ref[0], g_ref[0], _row(m, MOD_SC2), _row(m, MOD_SH2)).astype(BF16)
        acc_scr[...] = jnp.zeros_like(acc_scr)

    h = h_scr[...]
    a = _dot(h, w1_ref[0])
    u = (a * _sigmoid(a) * _dot(h, w3_ref[0])).astype(BF16)
    acc_scr[...] += _dot(u, w2_ref[0])

    @pl.when(e == n_groups - 1)
    def _():
        out = x_ref[0] + _row(m, MOD_G2) * acc_scr[...]
        if final_norm:
            ms = jnp.mean(out * out, axis=-1, keepdims=True)
            out = out * lax.rsqrt(ms + EPS) * fg_ref[...]
        y_ref[0] = out


def _ffn(layer, li, x, mod_all, g_all, w1_all, w3_all, w2_all, final_g, final_norm):
    bsz, seq, _ = x.shape
    n_groups = 2
    f = w1_all.shape[-1] // n_groups
    tm = 1024
    row = pl.BlockSpec((1, tm, D_MODEL), lambda b, i, e: (b, i, 0))
    return pl.pallas_call(
        functools.partial(_ffn_kernel, final_norm, n_groups),
        grid=(bsz, seq // tm, n_groups),
        in_specs=[row,
                  pl.BlockSpec((1, 1, 6, D_MODEL), lambda b, i, e: (layer, b, 0, 0)),
                  _layer_spec(g_all.shape, layer),
                  pl.BlockSpec((1, D_MODEL, f), lambda b, i, e: (li, 0, e)),
                  pl.BlockSpec((1, D_MODEL, f), lambda b, i, e: (li, 0, e)),
                  pl.BlockSpec((1, f, D_MODEL), lambda b, i, e: (li, e, 0)),
                  pl.BlockSpec((1, D_MODEL), lambda b, i, e: (0, 0))],
        out_specs=row,
        out_shape=jax.ShapeDtypeStruct(x.shape, F32),
        scratch_shapes=[pltpu.VMEM((tm, D_MODEL), BF16), pltpu.VMEM((tm, D_MODEL), F32)],
        compiler_params=pltpu.CompilerParams(
            dimension_semantics=("arbitrary", "arbitrary", "arbitrary"),
            vmem_limit_bytes=VMEM_LIMIT_BYTES),
        name="ffn",
    )(x, mod_all, g_all, w1_all, w3_all, w2_all, final_g)


def kernel(x, c, positions, ada_w, ada_b, norm_mix_g, norm_ffn_g, w_in, gla_w_alpha, gla_b_alpha,
           gla_norm_g, ret_gn_g, ret_gn_b, w_out, ffn_w1, ffn_w3, ffn_w2, router_w, moe_w1,
           moe_w3, moe_w2, final_g):
    bsz = x.shape[0]
    mod_all = _ada_mod(c, ada_w, ada_b).reshape(DEPTH, bsz, 6, D_MODEL)
    cos, sin = _rope_tables(positions)
    ga_lo = 2 * GLA_QK + 2 * GLA_V
    w_main_all = jnp.concatenate([w_in[:, :, :ga_lo], w_in[:, :, ga_lo + GLA_RANK:]], axis=2).astype(BF16)
    w_ga_all = jnp.pad(w_in[:, :, ga_lo:ga_lo + GLA_RANK],
                       ((0, 0), (0, 0), (0, LANES - GLA_RANK))).astype(BF16)
    w_alpha_all = jnp.pad(gla_w_alpha, ((0, 0), (0, LANES - GLA_RANK), (0, 0))).astype(BF16)
    vec = lambda a: a.reshape(a.shape[0], 1, a.shape[1])
    b_alpha_all, gla_g_all, gn_g_all, gn_b_all = map(vec, (gla_b_alpha, gla_norm_g, ret_gn_g, ret_gn_b))
    g_mix_all, g_ffn_all = vec(norm_mix_g), vec(norm_ffn_g)
    w_out_all = w_out.astype(BF16)
    ffn_w = [w.astype(BF16) for w in (ffn_w1, ffn_w3, ffn_w2)]
    moe_w = [w.astype(BF16) for w in (moe_w1, moe_w3, moe_w2)]
    fg = final_g.reshape(1, D_MODEL)
    for layer in range(DEPTH):
        proj, ga = _in_proj(layer, x, mod_all, g_mix_all, w_main_all, w_ga_all)
        o = _attention(layer, proj, ga, cos, sin, w_alpha_all, b_alpha_all, gla_g_all, gn_g_all, gn_b_all)
        x = _out_proj(layer, o, x, mod_all, w_out_all)
        li = layer // 2
        last = layer == DEPTH - 1
        if layer % 2 == 0:
            x = _ffn(layer, li, x, mod_all, g_ffn_all, *ffn_w, fg, last)
        else:
            x = _moe(layer, li, x, mod_all, g_ffn_all, router_w[li], *moe_w, fg, last)
    return x
```

```python
import functools

import numpy as np
import jax
import jax.numpy as jnp
from jax import lax
from jax.experimental import pallas as pl
from jax.experimental.pallas import tpu as pltpu

F32 = jnp.float32
BF16 = jnp.bfloat16

D_MODEL = 1024
DEPTH = 4
CHUNK = 64
GLA_HEADS = 4
GLA_DK = 64
GLA_DV = 128
GLA_RANK = 16
GLA_TAU = 16.0
RET_HEADS = 4
RET_DK = 128
RET_DV = 128
ROPE_BASE = 10000.0
N_EXPERTS = 8
EPS = 1e-6
GLA_QK = GLA_HEADS * GLA_DK
GLA_V = GLA_HEADS * GLA_DV
RET_QK = RET_HEADS * RET_DK
RET_V = RET_HEADS * RET_DV
MIX_WIDTH = GLA_V + RET_V

LANES = 128
VMEM_LIMIT_BYTES = 56 * 1024 * 1024

PROJ_WIDTH = 2 * GLA_QK + 2 * GLA_V + 2 * RET_QK + 2 * RET_V
OFF_GQ = 0
OFF_GK = OFF_GQ + GLA_QK
OFF_GV = OFF_GK + GLA_QK
OFF_GR = OFF_GV + GLA_V
OFF_RQ = OFF_GR + GLA_V
OFF_RK = OFF_RQ + RET_QK
OFF_RV = OFF_RK + RET_QK
OFF_RG = OFF_RV + RET_V

ATTN_ROWS = 256
ATTN_CHUNKS = ATTN_ROWS // CHUNK

MOD_SH1, MOD_SC1, MOD_G1, MOD_SH2, MOD_SC2, MOD_G2 = range(6)


def _sigmoid(x):
    return 1.0 / (1.0 + jnp.exp(-x))


def _rms_mod(x, g, sc, sh):
    ms = jnp.mean(x * x, axis=-1, keepdims=True)
    return (x * lax.rsqrt(ms + EPS)) * g * (1.0 + sc) + sh


def _dot(a, b):
    return jnp.dot(a, b, preferred_element_type=F32)


def _dot_nt(a, b):
    return lax.dot_general(a, b, (((1,), (1,)), ((), ())), preferred_element_type=F32)


def _dot_tn(a, b):
    return lax.dot_general(a, b, (((0,), (0,)), ((), ())), preferred_element_type=F32)


def _row(m, r):
    return m[r:r + 1]


def _layer_spec(shape, layer):
    zeros = (0,) * (len(shape) - 1)
    return pl.BlockSpec((1,) + tuple(shape[1:]), lambda *_: (layer,) + zeros)


def _ada_kernel(c_ref, w_ref, b_ref, o_ref):
    c = c_ref[...]
    cond = c * _sigmoid(c)
    o_ref[0] = jnp.dot(cond, w_ref[0], precision=lax.Precision.HIGHEST,
                       preferred_element_type=F32) + b_ref[0]


def _ada_mod(c, ada_w, ada_b):
    bsz = c.shape[0]
    tn = 1024
    n_out = ada_w.shape[-1]
    return pl.pallas_call(
        _ada_kernel,
        grid=(DEPTH, n_out // tn),
        in_specs=[
            pl.BlockSpec((bsz, D_MODEL), lambda l, j: (0, 0)),
            pl.BlockSpec((1, D_MODEL, tn), lambda l, j: (l, 0, j)),
            pl.BlockSpec((1, 1, tn), lambda l, j: (l, 0, j)),
        ],
        out_specs=pl.BlockSpec((1, bsz, tn), lambda l, j: (l, 0, j)),
        out_shape=jax.ShapeDtypeStruct((DEPTH, bsz, n_out), F32),
        name="ada_mod",
    )(c, ada_w, ada_b.reshape(DEPTH, 1, n_out))


def _rope_kernel(pos_ref, invf_ref, sign_ref, cos_ref, sin_ref):
    ang = pos_ref[...] * invf_ref[...]
    cos_ref[...] = jnp.cos(ang)
    sin_ref[...] = jnp.sin(ang) * sign_ref[...]


def _rope_tables(positions):
    n = positions.size
    half = RET_DK // 2
    inv_freq = ROPE_BASE ** (-jnp.arange(half, dtype=F32) / half)
    invf = jnp.concatenate([inv_freq, inv_freq]).reshape(1, RET_DK)
    sign = jnp.concatenate([-jnp.ones((half,), F32), jnp.ones((half,), F32)]).reshape(1, RET_DK)
    pos = positions.astype(F32).reshape(n, 1)
    tm = 2048
    row = pl.BlockSpec((1, RET_DK), lambda i: (0, 0))
    out = pl.BlockSpec((tm, RET_DK), lambda i: (i, 0))
    return pl.pallas_call(
        _rope_kernel,
        grid=(n // tm,),
        in_specs=[pl.BlockSpec((tm, 1), lambda i: (i, 0)), row, row],
        out_specs=[out, out],
        out_shape=[jax.ShapeDtypeStruct((n, RET_DK), F32)] * 2,
        name="rope_tables",
    )(pos, invf, sign)


OFF_GA = PROJ_WIDTH
PROJ_BLOCK = 256


def _attn_consts():
    r = ATTN_ROWS
    t = np.arange(r)
    same = (t[:, None] // CHUNK) == (t[None, :] // CHUNK)
    causal = t[:, None] >= t[None, :]
    tri = (same & causal).astype(np.float32)
    m_fwd = tri
    m_bwd = (same & ~causal).astype(np.float32)
    gam = 1.0 - 2.0 ** (-5.0 - np.arange(RET_HEADS, dtype=np.float64))
    lg = np.log(gam)
    dist = (t[:, None] - t[None, :]).astype(np.float64)
    d_ret = np.where(causal[None], np.exp(lg[:, None, None] * dist[None]),
                     np.where(same[None], np.exp(-lg[:, None, None] * dist[None]), 0.0))
    qdec = np.repeat(np.exp(lg[None, :] * (t[:, None] + 1.0)), RET_DK, axis=1)
    kdec = np.repeat(np.exp(lg[None, :] * (r - 1.0 - t[:, None])), RET_DK, axis=1)
    step_decay = [float(np.exp(lg[h] * r)) for h in range(RET_HEADS)]
    hmask = np.zeros((GLA_HEADS, 1, GLA_QK), np.float32)
    for h in range(GLA_HEADS):
        hmask[h, 0, h * GLA_DK:(h + 1) * GLA_DK] = 1.0
    vv = np.arange(GLA_V)[:, None] // GLA_DV
    kk = np.arange(GLA_QK)[None, :] // GLA_DK
    bdiag = (vv == kk).astype(np.float32)
    return dict(tri=tri, m_fwd=m_fwd, m_bwd=m_bwd, d_ret=d_ret.astype(np.float32),
                qdec=qdec.astype(np.float32), kdec=kdec.astype(np.float32),
                step_decay=step_decay, hmask=hmask, bdiag=bdiag)


def _mixer_kernel(step_decay, steps_per_seq, x_ref, mod_ref, g_ref, w_ref, wga_ref, *rest):
    proj_a, proj_b, o_scr, st_g, st_r = rest[-5:]
    j = pl.program_id(0)

    @pl.when(j == 0)
    def _():
        proj_b[...] = jnp.zeros_like(proj_b)

    @pl.when((j - 1) % steps_per_seq == 0)
    def _():
        st_g[...] = jnp.zeros_like(st_g)
        st_r[...] = jnp.zeros_like(st_r)

    def step(proj_w, proj_r):
        m = mod_ref[0, 0]
        h = _rms_mod(x_ref[0], g_ref[0], _row(m, MOD_SC1), _row(m, MOD_SH1)).astype(BF16)

        def block(c0):
            def emit():
                proj_w[:, c0:c0 + PROJ_BLOCK] = _dot(h, w_ref[0, :, c0:c0 + PROJ_BLOCK])
            return emit

        def gate_block():
            proj_w[:, OFF_GA:] = _dot(h, wga_ref[0])

        pending = [block(c0) for c0 in range(0, PROJ_WIDTH, PROJ_BLOCK)] + [gate_block]
        _attn_body(step_decay, pending, proj_r, *rest[:-5], o_scr, st_g, st_r)

    @pl.when(j % 2 == 0)
    def _():
        step(proj_a, proj_b)

    @pl.when(j % 2 == 1)
    def _():
        step(proj_b, proj_a)


def _attn_body(step_decay, pending, proj_ref, cos_ref, sin_ref, walpha_ref, balpha_ref,
               glag_ref, gng_ref, gnb_ref, tri_ref, mfwd_ref, mbwd_ref, dret_ref,
               qdec_ref, kdec_ref, hmask_ref, bdiag_ref, xres_ref, modres_ref, wout_ref, y_ref,
               o_scr, st_g, st_r):
    r = ATTN_ROWS
    pending = list(pending)

    def interleave(n=1):
        for _ in range(min(n, len(pending))):
            pending.pop(0)()


    z = _dot(proj_ref[:, OFF_GA:].astype(BF16), walpha_ref[0]) + balpha_ref[0]
    interleave(2)
    log_a = (jnp.minimum(z, 0.0) - jnp.log(1.0 + jnp.exp(-jnp.abs(z)))) * (1.0 / GLA_TAU)
    hi = log_a.astype(BF16)
    r1 = log_a - hi.astype(F32)
    mid = r1.astype(BF16)
    lo = (r1 - mid.astype(F32)).astype(BF16)
    tri = tri_ref[...]
    b = _dot(tri, hi) + _dot(tri, mid) + _dot(tri, lo)
    interleave(2)
    b3 = b.reshape(ATTN_CHUNKS, CHUNK, GLA_QK)
    b_last = b3[:, CHUNK - 1:CHUNK, :]
    k_upd_scale = jnp.exp(b_last - b3).reshape(r, GLA_QK)
    eb = jnp.exp(b)
    enb = jnp.exp(-b)
    q = proj_ref[:,OFF_GQ:OFF_GQ + GLA_QK] * (GLA_DK ** -0.5)
    k = proj_ref[:,OFF_GK:OFF_GK + GLA_QK]
    v_bf = proj_ref[:,OFF_GV:OFF_GV + GLA_V].astype(BF16)
    q_f = q * eb
    q_b = q * enb
    k_f = (k * enb).astype(BF16)
    k_b = (k * eb).astype(BF16)
    k_u = (k * k_upd_scale).astype(BF16)
    q_f_bf = q_f.astype(BF16)

    bdiag = bdiag_ref[...]
    inter = []
    for g in range(ATTN_CHUNKS):
        rows = slice(g * CHUNK, (g + 1) * CHUNK)
        st = st_g[...]
        inter.append(_dot_nt(q_f_bf[rows], st.astype(BF16)))
        upd_t = _dot_tn(v_bf[rows], k_u[rows]) * bdiag
        st_g[...] = st * jnp.exp(b_last[g]) + upd_t
        interleave()
    o_inter = jnp.concatenate(inter, axis=0)

    m_fwd = mfwd_ref[...] > 0.5
    m_bwd = mbwd_ref[...] > 0.5
    for h in range(GLA_HEADS):
        hm = hmask_ref[h]
        s_f = _dot_nt((q_f * hm).astype(BF16), k_f)
        s_b = _dot_nt((q_b * hm).astype(BF16), k_b)
        sc = jnp.where(m_fwd, s_f, jnp.where(m_bwd, s_b, 0.0)).astype(BF16)
        lanes = slice(h * GLA_DV, (h + 1) * GLA_DV)
        o_h = _dot(sc, v_bf[:, lanes]) + o_inter[:, lanes]
        interleave()
        o_h = o_h * lax.rsqrt(jnp.mean(o_h * o_h, axis=-1, keepdims=True) + EPS)
        gate = proj_ref[:,OFF_GR + h * GLA_DV:OFF_GR + (h + 1) * GLA_DV]
        o_h = o_h * glag_ref[0, :, lanes] * (gate * _sigmoid(gate))
        o_scr[:, lanes] = o_h.astype(BF16)

    cos = cos_ref[0]
    sin = sin_ref[0]
    for h in range(RET_HEADS):
        lanes = slice(h * RET_DK, (h + 1) * RET_DK)
        qh = proj_ref[:,OFF_RQ + h * RET_DK:OFF_RQ + (h + 1) * RET_DK]
        kh = proj_ref[:,OFF_RK + h * RET_DK:OFF_RK + (h + 1) * RET_DK]
        vh = proj_ref[:,OFF_RV + h * RET_DV:OFF_RV + (h + 1) * RET_DV].astype(BF16)
        qh = (qh * cos + pltpu.roll(qh, RET_DK // 2, axis=1) * sin) * (RET_DK ** -0.5)
        kh = kh * cos + pltpu.roll(kh, RET_DK // 2, axis=1) * sin
        s = _dot_nt(qh.astype(BF16), kh.astype(BF16)) * dret_ref[h]
        st = st_r[h]
        o_h = _dot(s.astype(BF16), vh) + _dot((qh * qdec_ref[:, lanes]).astype(BF16), st.astype(BF16))
        st_r[h] = st * step_decay[h] + _dot_tn((kh * kdec_ref[:, lanes]).astype(BF16), vh)
        interleave()
        mu =jnp.mean(o_h, axis=-1, keepdims=True)
        d = o_h - mu
        var = jnp.mean(d * d, axis=-1, keepdims=True)
        o_h = d * lax.rsqrt(var + EPS) * gng_ref[0, :, lanes] + gnb_ref[0, :, lanes]
        gate = proj_ref[:,OFF_RG + h * RET_DV:OFF_RG + (h + 1) * RET_DV]
        o_h = o_h * (gate * _sigmoid(gate))
        o_scr[:, GLA_V + h * RET_DV:GLA_V + (h + 1) * RET_DV] = o_h.astype(BF16)
    interleave(len(pending))
    y_ref[0] = xres_ref[0] + _row(modres_ref[0, 0], MOD_G1) * _dot(o_scr[...], wout_ref[0])


def _mixer(layer, x, mod_all, g_all, w_main_all, w_ga_all, cos, sin, w_alpha_all, b_alpha_all,
           gla_g_all, gn_g_all, gn_b_all, w_out_all):
    bsz, seq, _ = x.shape
    r = ATTN_ROWS
    steps_per_seq = seq // r
    n_tiles = bsz * steps_per_seq
    c = _attn_consts()
    const2 = lambda shape: pl.BlockSpec(shape, lambda j: (0,) * len(shape))
    per_layer = lambda a: _layer_spec(a.shape, layer)
    proj_tile = lambda j: jnp.minimum(j, n_tiles - 1)
    mix_tile = lambda j: jnp.maximum(j - 1, 0)
    mix_blk = lambda w: pl.BlockSpec((1, r, w), lambda j: (mix_tile(j), 0, 0))
    out = pl.pallas_call(
        functools.partial(_mixer_kernel, c["step_decay"], steps_per_seq),
        grid=(n_tiles + 1,),
        in_specs=[
            pl.BlockSpec((1, r, D_MODEL), lambda j: (proj_tile(j), 0, 0)),
            pl.BlockSpec((1, 1, 6, D_MODEL), lambda j: (layer, proj_tile(j) // steps_per_seq, 0, 0)),
            per_layer(g_all), per_layer(w_main_all), per_layer(w_ga_all),
            mix_blk(RET_DK), mix_blk(RET_DK),
            per_layer(w_alpha_all), per_layer(b_alpha_all),
            per_layer(gla_g_all), per_layer(gn_g_all), per_layer(gn_b_all),
            const2((r, r)), const2((r, r)), const2((r, r)), const2((RET_HEADS, r, r)),
            const2((r, RET_QK)), const2((r, RET_QK)),
            const2((GLA_HEADS, 1, GLA_QK)), const2((GLA_V, GLA_QK)),
            mix_blk(D_MODEL),
            pl.BlockSpec((1, 1, 6, D_MODEL), lambda j: (layer, mix_tile(j) // steps_per_seq, 0, 0)),
            per_layer(w_out_all),
        ],
        out_specs=mix_blk(D_MODEL),
        out_shape=jax.ShapeDtypeStruct((n_tiles, r, D_MODEL), F32),
        scratch_shapes=[pltpu.VMEM((r, PROJ_WIDTH + LANES), F32),
                        pltpu.VMEM((r, PROJ_WIDTH + LANES), F32),
                        pltpu.VMEM((r, MIX_WIDTH), BF16),
                        pltpu.VMEM((GLA_V, GLA_QK), F32),
                        pltpu.VMEM((RET_HEADS, RET_DK, RET_DV), F32)],
        compiler_params=pltpu.CompilerParams(
            dimension_semantics=("arbitrary",), vmem_limit_bytes=VMEM_LIMIT_BYTES),
        name="mixer",
    )(x.reshape(n_tiles, r, D_MODEL), mod_all, g_all, w_main_all, w_ga_all,
      cos.reshape(n_tiles, r, RET_DK), sin.reshape(n_tiles, r, RET_DK),
      w_alpha_all, b_alpha_all, gla_g_all, gn_g_all, gn_b_all,
      jnp.asarray(c["tri"], BF16), jnp.asarray(c["m_fwd"]), jnp.asarray(c["m_bwd"]),
      jnp.asarray(c["d_ret"]), jnp.asarray(c["qdec"]), jnp.asarray(c["kdec"]),
      jnp.asarray(c["hmask"]), jnp.asarray(c["bdiag"]),
      x.reshape(n_tiles, r, D_MODEL), mod_all, w_out_all)
    return out.reshape(x.shape)


ROUTE_ROWS = 512
EXPERT_ROWS = 512
SEG_ALIGN = 8
LOCAL_ROWS = -(-(2 * ROUTE_ROWS + N_EXPERTS * (SEG_ALIGN - 1)) // LANES) * LANES
PIECE_BITS = (max(ROUTE_ROWS, EXPERT_ROWS) // SEG_ALIGN).bit_length()
INFO_POS, INFO_PROB = 0, 2


def _route_kernel(x_ref, mod_ref, g_ref, rw_ref, tril_ref, upper_ref, h_ref, info_ref, cnt_ref):
    m = mod_ref[0, 0]
    h = _rms_mod(x_ref[...], g_ref[0], _row(m, MOD_SC2), _row(m, MOD_SH2))
    h_hi = h.astype(BF16)
    h_ref[...] = h_hi
    h_lo = (h - h_hi.astype(F32)).astype(BF16)
    hh = _dot(h_hi, rw_ref[...])
    logits = hh[:, :LANES] + hh[:, LANES:] + _dot(h_lo, rw_ref[:, :LANES])
    lane = lax.broadcasted_iota(jnp.int32, logits.shape, 1)
    neg = jnp.float32(-jnp.inf)
    lg = jnp.where(lane < N_EXPERTS, logits, neg)
    m1 = jnp.max(lg, axis=-1, keepdims=True)
    i1 = jnp.min(jnp.where(lg == m1, lane, LANES), axis=-1, keepdims=True)
    lg2 = jnp.where(lane == i1, neg, lg)
    m2 = jnp.max(lg2, axis=-1, keepdims=True)
    i2 = jnp.min(jnp.where(lg2 == m2, lane, LANES), axis=-1, keepdims=True)
    e2 = jnp.exp(m2 - m1)
    p1 = 1.0 / (1.0 + e2)
    p2 = e2 * p1
    sel1 = lane == i1
    sel2 = lane == i2
    onehot = jnp.where(sel1, 1.0, 0.0) + jnp.where(sel2, 1.0, 0.0)
    incl = _dot(tril_ref[...], onehot.astype(BF16))
    counts = incl[ROUTE_ROWS - 1:ROUTE_ROWS]
    seg_units = jnp.floor((counts + (SEG_ALIGN - 1)) * (1.0 / SEG_ALIGN))
    seg_units8 = jnp.broadcast_to(seg_units, (8, LANES)).astype(BF16)
    seg_off = _dot(seg_units8, upper_ref[...])[0:1] * SEG_ALIGN
    pos = incl - onehot + seg_off
    pos1 = jnp.sum(jnp.where(sel1, pos, 0.0), axis=-1, keepdims=True)
    pos2 = jnp.sum(jnp.where(sel2, pos, 0.0), axis=-1, keepdims=True)
    cnt_ref[...] = jnp.broadcast_to(counts, (8, LANES))
    rec = jnp.zeros(logits.shape, F32)
    for lane_id, val in ((INFO_POS, pos1), (INFO_POS + 1, pos2), (INFO_PROB, p1), (INFO_PROB + 1, p2)):
        rec = jnp.where(lane == lane_id, val, rec)
    info_ref[...] = rec


def _route(layer, x2, mod_all, g_all, router_w, rows_per_batch):
    n = x2.shape[0]
    tm = ROUTE_ROWS
    per_b = rows_per_batch // tm
    w_hi = router_w.astype(BF16)
    w_lo = (router_w - w_hi.astype(F32)).astype(BF16)
    pad = ((0, 0), (0, LANES - N_EXPERTS))
    rw = jnp.concatenate([jnp.pad(w_hi, pad), jnp.pad(w_lo, pad)], axis=1)
    tril = jnp.asarray(np.tril(np.ones((tm, tm), np.float32)), BF16)
    upper = jnp.asarray(np.triu(np.ones((LANES, LANES), np.float32), 1), BF16)
    return pl.pallas_call(
        _route_kernel,
        grid=(n // tm,),
        in_specs=[pl.BlockSpec((tm, D_MODEL), lambda i: (i, 0)),
                  pl.BlockSpec((1, 1, 6, D_MODEL), lambda i: (layer, i // per_b, 0, 0)),
                  _layer_spec(g_all.shape, layer),
                  pl.BlockSpec((D_MODEL, 2 * LANES), lambda i: (0, 0)),
                  pl.BlockSpec((tm, tm), lambda i: (0, 0)),
                  pl.BlockSpec((LANES, LANES), lambda i: (0, 0))],
        out_specs=[pl.BlockSpec((tm, D_MODEL), lambda i: (i, 0)),
                   pl.BlockSpec((tm, LANES), lambda i: (i, 0)),
                   pl.BlockSpec((8, LANES), lambda i: (i, 0))],
        out_shape=[jax.ShapeDtypeStruct((n, D_MODEL), BF16),
                   jax.ShapeDtypeStruct((n, LANES), F32),
                   jax.ShapeDtypeStruct((n // tm * 8, LANES), F32)],
        compiler_params=pltpu.CompilerParams(
            dimension_semantics=("arbitrary",), vmem_limit_bytes=VMEM_LIMIT_BYTES),
        name="route",
    )(x2, mod_all, g_all, rw, tril, upper)


def _rows(ref, row, n_rows):
    return ref.at[pl.ds(pl.multiple_of(row, SEG_ALIGN), n_rows)]


def _for_each_piece(n_units, fn):
    for bit in reversed(range(PIECE_BITS)):
        covered = (n_units >> (bit + 1)) << (bit + 1)

        @pl.when(((n_units >> bit) & 1) == 1)
        def _():
            fn(covered * SEG_ALIGN, SEG_ALIGN << bit)


def _for_each_segment_piece(seg_ref, n_seg, tile, fn):
    for e in range(N_EXPERTS):
        s = tile * N_EXPERTS + e
        sorted_row = seg_ref[s]
        local_row = seg_ref[n_seg + s]
        _for_each_piece(seg_ref[2 * n_seg + s],
                        lambda first, n_rows: fn(local_row + first, sorted_row + first, n_rows))


def _selection(info, coeff1, coeff2):
    lane = lax.broadcasted_iota(jnp.int32, (info.shape[0], LOCAL_ROWS), 1)
    pos1 = info[:, INFO_POS:INFO_POS + 1].astype(jnp.int32)
    pos2 = info[:, INFO_POS + 1:INFO_POS + 2].astype(jnp.int32)
    return jnp.where(lane == pos1, coeff1, jnp.where(lane == pos2, coeff2, 0.0)).astype(BF16)


def _dispatch_kernel(n_seg, seg_ref, pad_ref, h_ref, info_ref, xs_ref, sorted_scr, zero_scr, sems, pad_sem):
    i = pl.program_id(0)
    n_steps = pl.num_programs(0)
    slot = i % 2

    def move(tile, buf_slot, op):
        def piece(local_row, sorted_row, n_rows):
            cp = pltpu.make_async_copy(_rows(sorted_scr.at[buf_slot], local_row, n_rows),
                                       _rows(xs_ref, sorted_row, n_rows), sems.at[buf_slot])
            op(cp)
        _for_each_segment_piece(seg_ref, n_seg, tile, piece)

    @pl.when(i == 0)
    def _():
        zero_scr[...] = jnp.zeros_like(zero_scr)
        for op in (lambda cp: cp.start(), lambda cp: cp.wait()):
            for e in range(N_EXPERTS):
                lo = pad_ref[e]
                _for_each_piece(pad_ref[N_EXPERTS + e],
                                lambda first, n_rows: op(pltpu.make_async_copy(
                                    zero_scr.at[pl.ds(0, n_rows)], _rows(xs_ref, lo + first, n_rows),
                                    pad_sem)))

        def tile_copy(j):
            return pltpu.make_async_copy(zero_scr, _rows(xs_ref, j * EXPERT_ROWS, EXPERT_ROWS), pad_sem)

        def fill_tile(j, carry):
            tile_copy(j).start()
            return carry

        def drain_tile(j, carry):
            tile_copy(j).wait()
            return carry

        n_tiles = xs_ref.shape[0] // EXPERT_ROWS
        lax.fori_loop(pad_ref[2 * N_EXPERTS], n_tiles, fill_tile, 0)
        lax.fori_loop(pad_ref[2 * N_EXPERTS], n_tiles, drain_tile, 0)

    @pl.when(i >= 2)
    def _():
        move(i - 2, slot, lambda cp: cp.wait())

    sel = _selection(info_ref[...], 1.0, 1.0)
    sorted_scr[slot] = _dot_tn(sel, h_ref[...])
    move(i, slot, lambda cp: cp.start())

    @pl.when(i == n_steps - 1)
    def _():
        @pl.when(i >= 1)
        def _():
            move(i - 1, 1 - slot, lambda cp: cp.wait())
        move(i, slot, lambda cp: cp.wait())


def _dispatch(h2, info, segs, pads, n_sorted):
    n = h2.shape[0]
    tm = ROUTE_ROWS
    n_seg = n // tm * N_EXPERTS
    return pl.pallas_call(
        functools.partial(_dispatch_kernel, n_seg),
        grid_spec=pltpu.PrefetchScalarGridSpec(
            num_scalar_prefetch=2,
            grid=(n // tm,),
            in_specs=[pl.BlockSpec((tm, D_MODEL), lambda i, s, p: (i, 0)),
                      pl.BlockSpec((tm, LANES), lambda i, s, p: (i, 0))],
            out_specs=pl.BlockSpec(memory_space=pl.ANY),
            scratch_shapes=[pltpu.VMEM((2, LOCAL_ROWS, D_MODEL), F32),
                            pltpu.VMEM((EXPERT_ROWS, D_MODEL), F32),
                            pltpu.SemaphoreType.DMA((2,)), pltpu.SemaphoreType.DMA(())]),
        out_shape=jax.ShapeDtypeStruct((n_sorted, D_MODEL), F32),
        compiler_params=pltpu.CompilerParams(
            dimension_semantics=("arbitrary",), vmem_limit_bytes=VMEM_LIMIT_BYTES),
        name="dispatch",
    )(segs, pads, h2, info)


def _expert_kernel(tile_blk, tile_e, n_act, x_ref, w1_ref, w3_ref, w2_ref, y_ref):
    active = pl.program_id(0) < n_act[0]

    @pl.when(active)
    def _():
        h = x_ref[...].astype(BF16)
        a = _dot(h, w1_ref[0, 0])
        u = (a * _sigmoid(a) * _dot(h, w3_ref[0, 0])).astype(BF16)
        y_ref[...] = _dot(u, w2_ref[0, 0])

    @pl.when(jnp.logical_not(active))
    def _():
        y_ref[...] = jnp.zeros_like(y_ref)


def _experts(li, xs, tile_blk, tile_e, n_act, w1_all, w3_all, w2_all):
    n_sorted = xs.shape[0]
    tm = EXPERT_ROWS
    f = w1_all.shape[-1]
    return pl.pallas_call(
        _expert_kernel,
        grid_spec=pltpu.PrefetchScalarGridSpec(
            num_scalar_prefetch=3,
            grid=(n_sorted // tm,),
            in_specs=[pl.BlockSpec((tm, D_MODEL), lambda i, b, e, n: (b[i], 0)),
                      pl.BlockSpec((1, 1, D_MODEL, f), lambda i, b, e, n: (li, e[i], 0, 0)),
                      pl.BlockSpec((1, 1, D_MODEL, f), lambda i, b, e, n: (li, e[i], 0, 0)),
                      pl.BlockSpec((1, 1, f, D_MODEL), lambda i, b, e, n: (li, e[i], 0, 0))],
            out_specs=pl.BlockSpec((tm, D_MODEL), lambda i, b, e, n: (i, 0))),
        out_shape=jax.ShapeDtypeStruct((n_sorted, D_MODEL), F32),
        compiler_params=pltpu.CompilerParams(
            dimension_semantics=("arbitrary",), vmem_limit_bytes=VMEM_LIMIT_BYTES),
        name="experts",
    )(tile_blk, tile_e, n_act, xs, w1_all, w3_all, w2_all)


def _combine_kernel(final_norm, n_seg, seg_ref, x_ref, mod_ref, info_ref, fg_ref, ys_ref, out_ref,
                    ybuf, sems):
    i = pl.program_id(0)
    n_steps = pl.num_programs(0)
    slot = i % 2

    def move(tile, buf_slot, op):
        def piece(local_row, sorted_row, n_rows):
            cp = pltpu.make_async_copy(_rows(ys_ref, sorted_row, n_rows),
                                       _rows(ybuf.at[buf_slot], local_row, n_rows), sems.at[buf_slot])
            op(cp)
        _for_each_segment_piece(seg_ref, n_seg, tile, piece)

    @pl.when(i == 0)
    def _():
        ybuf[...] = jnp.zeros_like(ybuf)
        move(0, 0, lambda cp: cp.start())

    @pl.when(i + 1 < n_steps)
    def _():
        move(i + 1, 1 - slot, lambda cp: cp.start())

    move(i, slot, lambda cp: cp.wait())
    info = info_ref[...]
    sel = _selection(info, info[:, INFO_PROB:INFO_PROB + 1], info[:, INFO_PROB + 1:INFO_PROB + 2])
    y = _dot(sel, ybuf[slot].astype(BF16))
    out = x_ref[...] + _row(mod_ref[0, 0], MOD_G2) * y
    if final_norm:
        ms = jnp.mean(out * out, axis=-1, keepdims=True)
        out = out * lax.rsqrt(ms + EPS) * fg_ref[...]
    out_ref[...] = out


def _combine(layer, x2, mod_all, info, ys, segs, final_g, final_norm, rows_per_batch):
    n = x2.shape[0]
    tm = ROUTE_ROWS
    per_b = rows_per_batch // tm
    n_seg = n // tm * N_EXPERTS
    return pl.pallas_call(
        functools.partial(_combine_kernel, final_norm, n_seg),
        grid_spec=pltpu.PrefetchScalarGridSpec(
            num_scalar_prefetch=1,
            grid=(n // tm,),
            in_specs=[pl.BlockSpec((tm, D_MODEL), lambda i, s: (i, 0)),
                      pl.BlockSpec((1, 1, 6, D_MODEL), lambda i, s: (layer, i // per_b, 0, 0)),
                      pl.BlockSpec((tm, LANES), lambda i, s: (i, 0)),
                      pl.BlockSpec((1, D_MODEL), lambda i, s: (0, 0)),
                      pl.BlockSpec(memory_space=pl.ANY)],
            out_specs=pl.BlockSpec((tm, D_MODEL), lambda i, s: (i, 0)),
            scratch_shapes=[pltpu.VMEM((2, LOCAL_ROWS, D_MODEL), F32),
                            pltpu.SemaphoreType.DMA((2,))]),
        out_shape=jax.ShapeDtypeStruct(x2.shape, F32),
        compiler_params=pltpu.CompilerParams(
            dimension_semantics=("arbitrary",), vmem_limit_bytes=VMEM_LIMIT_BYTES),
        name="combine",
    )(segs, x2, mod_all, info, final_g, ys)


def _moe(layer, li, x, mod_all, g_all, router_w, w1_all, w3_all, w2_all, final_g, final_norm):
    bsz, seq, _ = x.shape
    n = bsz * seq
    tm = EXPERT_ROWS
    n_tok_tiles = n // ROUTE_ROWS
    x2 = x.reshape(n, D_MODEL)
    h2, info, cnt = _route(layer, x2, mod_all, g_all, router_w, seq)
    counts = cnt.reshape(n_tok_tiles, 8, LANES)[:, 0, :N_EXPERTS].astype(jnp.int32)
    seg_rows = (counts + (SEG_ALIGN - 1)) // SEG_ALIGN * SEG_ALIGN
    group_rows = jnp.sum(seg_rows, axis=0)
    tiles = (group_rows + (tm - 1)) // tm
    tile_end = jnp.cumsum(tiles)
    group_off = (tile_end - tiles) * tm
    seg_sorted = group_off[None, :] + jnp.cumsum(seg_rows, axis=0) - seg_rows
    seg_local = jnp.cumsum(seg_rows, axis=1) - seg_rows
    segs = jnp.concatenate([seg_sorted.reshape(-1), seg_local.reshape(-1),
                            (seg_rows // SEG_ALIGN).reshape(-1)]).astype(jnp.int32)
    max_rows = 2 * n + n_tok_tiles * N_EXPERTS * (SEG_ALIGN - 1)
    n_tiles = -(-max_rows // tm) + N_EXPERTS
    n_act = tile_end[-1]
    t = jnp.minimum(jnp.arange(n_tiles, dtype=jnp.int32), n_act - 1)
    tile_e = jnp.sum(t[:, None] >= tile_end[None, :], axis=1).astype(jnp.int32)
    pad_lo = group_off + group_rows
    pads = jnp.concatenate([pad_lo, (tile_end * tm - pad_lo) // SEG_ALIGN,
                            n_act[None]]).astype(jnp.int32)
    xs = _dispatch(h2, info, segs, pads, n_tiles * tm)
    ys = _experts(li, xs, t, tile_e, n_act.reshape(1).astype(jnp.int32), w1_all, w3_all, w2_all)
    out = _combine(layer, x2, mod_all, info, ys, segs, final_g, final_norm, seq)
    return out.reshape(x.shape)


def _ffn_kernel(final_norm, n_groups, x_ref, mod_ref, g_ref, w1_ref, w3_ref, w2_ref, fg_ref, y_ref,
                h_scr, acc_scr):
    e = pl.program_id(2)
    m = mod_ref[0, 0]

    @pl.when(e == 0)
    def _():
        h_scr[...] = _rms_mod(x_ref[0], g_ref[0], _row(m, MOD_SC2), _row(m, MOD_SH2)).astype(BF16)
        acc_scr[...] = jnp.zeros_like(acc_scr)

    h = h_scr[...]
    a = _dot(h, w1_ref[0])
    u = (a * _sigmoid(a) * _dot(h, w3_ref[0])).astype(BF16)
    acc_scr[...] += _dot(u, w2_ref[0])

    @pl.when(e == n_groups - 1)
    def _():
        out = x_ref[0] + _row(m, MOD_G2) * acc_scr[...]
        if final_norm:
            ms = jnp.mean(out * out, axis=-1, keepdims=True)
            out = out * lax.rsqrt(ms + EPS) * fg_ref[...]
        y_ref[0] = out


def _ffn(layer, li, x, mod_all, g_all, w1_all, w3_all, w2_all, final_g, final_norm):
    bsz, seq, _ = x.shape
    n_groups = 2
    f = w1_all.shape[-1] // n_groups
    tm = 1024
    row = pl.BlockSpec((1, tm, D_MODEL), lambda b, i, e: (b, i, 0))
    return pl.pallas_call(
        functools.partial(_ffn_kernel, final_norm, n_groups),
        grid=(bsz, seq // tm, n_groups),
        in_specs=[row,
                  pl.BlockSpec((1, 1, 6, D_MODEL), lambda b, i, e: (layer, b, 0, 0)),
                  _layer_spec(g_all.shape, layer),
                  pl.BlockSpec((1, D_MODEL, f), lambda b, i, e: (li, 0, e)),
                  pl.BlockSpec((1, D_MODEL, f), lambda b, i, e: (li, 0, e)),
                  pl.BlockSpec((1, f, D_MODEL), lambda b, i, e: (li, e, 0)),
                  pl.BlockSpec((1, D_MODEL), lambda b, i, e: (0, 0))],
        out_specs=row,
        out_shape=jax.ShapeDtypeStruct(x.shape, F32),
        scratch_shapes=[pltpu.VMEM((tm, D_MODEL), BF16), pltpu.VMEM((tm, D_MODEL), F32)],
        compiler_params=pltpu.CompilerParams(
            dimension_semantics=("arbitrary", "arbitrary", "arbitrary"),
            vmem_limit_bytes=VMEM_LIMIT_BYTES),
        name="ffn",
    )(x, mod_all, g_all, w1_all, w3_all, w2_all, final_g)


def kernel(x, c, positions, ada_w, ada_b, norm_mix_g, norm_ffn_g, w_in, gla_w_alpha, gla_b_alpha,
           gla_norm_g, ret_gn_g, ret_gn_b, w_out, ffn_w1, ffn_w3, ffn_w2, router_w, moe_w1,
           moe_w3, moe_w2, final_g):
    bsz = x.shape[0]
    mod_all = _ada_mod(c, ada_w, ada_b).reshape(DEPTH, bsz, 6, D_MODEL)
    cos, sin = _rope_tables(positions)
    ga_lo = 2 * GLA_QK + 2 * GLA_V
    w_main_all = jnp.concatenate([w_in[:, :, :ga_lo], w_in[:, :, ga_lo + GLA_RANK:]], axis=2).astype(BF16)
    w_ga_all = jnp.pad(w_in[:, :, ga_lo:ga_lo + GLA_RANK],
                       ((0, 0), (0, 0), (0, LANES - GLA_RANK))).astype(BF16)
    w_alpha_all = jnp.pad(gla_w_alpha, ((0, 0), (0, LANES - GLA_RANK), (0, 0))).astype(BF16)
    vec = lambda a: a.reshape(a.shape[0], 1, a.shape[1])
    b_alpha_all, gla_g_all, gn_g_all, gn_b_all = map(vec, (gla_b_alpha, gla_norm_g, ret_gn_g, ret_gn_b))
    g_mix_all, g_ffn_all = vec(norm_mix_g), vec(norm_ffn_g)
    w_out_all = w_out.astype(BF16)
    ffn_w = [w.astype(BF16) for w in (ffn_w1, ffn_w3, ffn_w2)]
    moe_w = [w.astype(BF16) for w in (moe_w1, moe_w3, moe_w2)]
    fg = final_g.reshape(1, D_MODEL)
    for layer in range(DEPTH):
        x = _mixer(layer, x, mod_all, g_mix_all, w_main_all, w_ga_all, cos, sin, w_alpha_all,
                   b_alpha_all, gla_g_all, gn_g_all, gn_b_all, w_out_all)
        li = layer // 2
        last = layer == DEPTH - 1
        if layer % 2 == 0:
            x = _ffn(layer, li, x, mod_all, g_ffn_all, *ffn_w, fg, last)
        else:
            x = _moe(layer, li, x, mod_all, g_ffn_all, router_w[li], *moe_w, fg, last)
    return x
```

```python
import functools

import numpy as np
import jax
import jax.numpy as jnp
from jax import lax
from jax.experimental import pallas as pl
from jax.experimental.pallas import tpu as pltpu

F32 = jnp.float32
BF16 = jnp.bfloat16

D_MODEL = 1024
DEPTH = 4
CHUNK = 64
GLA_HEADS = 4
GLA_DK = 64
GLA_DV = 128
GLA_RANK = 16
GLA_TAU = 16.0
RET_HEADS = 4
RET_DK = 128
RET_DV = 128
ROPE_BASE = 10000.0
N_EXPERTS = 8
EPS = 1e-6
GLA_QK = GLA_HEADS * GLA_DK
GLA_V = GLA_HEADS * GLA_DV
RET_QK = RET_HEADS * RET_DK
RET_V = RET_HEADS * RET_DV
MIX_WIDTH = GLA_V + RET_V

LANES = 128
VMEM_LIMIT_BYTES = 56 * 1024 * 1024

PROJ_WIDTH = 2 * GLA_QK + 2 * GLA_V + 2 * RET_QK + 2 * RET_V
OFF_GQ = 0
OFF_GK = OFF_GQ + GLA_QK
OFF_GV = OFF_GK + GLA_QK
OFF_GR = OFF_GV + GLA_V
OFF_RQ = OFF_GR + GLA_V
OFF_RK = OFF_RQ + RET_QK
OFF_RV = OFF_RK + RET_QK
OFF_RG = OFF_RV + RET_V

ATTN_ROWS = 256
ATTN_CHUNKS = ATTN_ROWS // CHUNK

MOD_SH1, MOD_SC1, MOD_G1, MOD_SH2, MOD_SC2, MOD_G2 = range(6)


def _sigmoid(x):
    return 1.0 / (1.0 + jnp.exp(-x))


def _rms_mod(x, g, sc, sh):
    ms = jnp.mean(x * x, axis=-1, keepdims=True)
    return (x * lax.rsqrt(ms + EPS)) * g * (1.0 + sc) + sh


def _dot(a, b):
    return jnp.dot(a, b, preferred_element_type=F32)


def _dot_nt(a, b):
    return lax.dot_general(a, b, (((1,), (1,)), ((), ())), preferred_element_type=F32)


def _dot_tn(a, b):
    return lax.dot_general(a, b, (((0,), (0,)), ((), ())), preferred_element_type=F32)


def _row(m, r):
    return m[r:r + 1]


def _layer_spec(shape, layer):
    zeros = (0,) * (len(shape) - 1)
    return pl.BlockSpec((1,) + tuple(shape[1:]), lambda *_: (layer,) + zeros)


def _ada_kernel(c_ref, w_ref, b_ref, o_ref):
    c = c_ref[...]
    cond = c * _sigmoid(c)
    c_hi = cond.astype(BF16)
    c_lo = (cond - c_hi.astype(F32)).astype(BF16)
    w = w_ref[0]
    w_hi = w.astype(BF16)
    w_lo = (w - w_hi.astype(F32)).astype(BF16)
    o_ref[0] = _dot(c_hi, w_hi) + _dot(c_hi, w_lo) + _dot(c_lo, w_hi) + b_ref[0]


def _ada_mod(c, ada_w, ada_b):
    bsz = c.shape[0]
    tn = 1024
    n_out = ada_w.shape[-1]
    return pl.pallas_call(
        _ada_kernel,
        grid=(DEPTH, n_out // tn),
        in_specs=[
            pl.BlockSpec((bsz, D_MODEL), lambda l, j: (0, 0)),
            pl.BlockSpec((1, D_MODEL, tn), lambda l, j: (l, 0, j)),
            pl.BlockSpec((1, 1, tn), lambda l, j: (l, 0, j)),
        ],
        out_specs=pl.BlockSpec((1, bsz, tn), lambda l, j: (l, 0, j)),
        out_shape=jax.ShapeDtypeStruct((DEPTH, bsz, n_out), F32),
        name="ada_mod",
    )(c, ada_w, ada_b.reshape(DEPTH, 1, n_out))


def _rope_kernel(pos_ref, invf_ref, sign_ref, cos_ref, sin_ref):
    ang = pos_ref[...] * invf_ref[...]
    cos_ref[...] = jnp.cos(ang)
    sin_ref[...] = jnp.sin(ang) * sign_ref[...]


def _rope_tables(positions):
    n = positions.size
    half = RET_DK // 2
    inv_freq = ROPE_BASE ** (-jnp.arange(half, dtype=F32) / half)
    invf = jnp.concatenate([inv_freq, inv_freq]).reshape(1, RET_DK)
    sign = jnp.concatenate([-jnp.ones((half,), F32), jnp.ones((half,), F32)]).reshape(1, RET_DK)
    pos = positions.astype(F32).reshape(n, 1)
    tm = 2048
    row = pl.BlockSpec((1, RET_DK), lambda i: (0, 0))
    out = pl.BlockSpec((tm, RET_DK), lambda i: (i, 0))
    return pl.pallas_call(
        _rope_kernel,
        grid=(n // tm,),
        in_specs=[pl.BlockSpec((tm, 1), lambda i: (i, 0)), row, row],
        out_specs=[out, out],
        out_shape=[jax.ShapeDtypeStruct((n, RET_DK), F32)] * 2,
        name="rope_tables",
    )(pos, invf, sign)


OFF_GA = PROJ_WIDTH
PROJ_BLOCK = 256
FFN_CHUNKS = 16
MOE_CHUNKS_PER_EXPERT = 8


def _attn_consts():
    r = ATTN_ROWS
    t = np.arange(r)
    same = (t[:, None] // CHUNK) == (t[None, :] // CHUNK)
    causal = t[:, None] >= t[None, :]
    tri = (same & causal).astype(np.float32)
    m_fwd = tri
    m_bwd = (same & ~causal).astype(np.float32)
    gam = 1.0 - 2.0 ** (-5.0 - np.arange(RET_HEADS, dtype=np.float64))
    lg = np.log(gam)
    dist = (t[:, None] - t[None, :]).astype(np.float64)
    d_ret = np.where(causal[None], np.exp(lg[:, None, None] * dist[None]),
                     np.where(same[None], np.exp(-lg[:, None, None] * dist[None]), 0.0))
    qdec = np.repeat(np.exp(lg[None, :] * (t[:, None] + 1.0)), RET_DK, axis=1)
    kdec = np.repeat(np.exp(lg[None, :] * (r - 1.0 - t[:, None])), RET_DK, axis=1)
    step_decay = [float(np.exp(lg[h] * r)) for h in range(RET_HEADS)]
    hmask = np.zeros((GLA_HEADS, 1, GLA_QK), np.float32)
    for h in range(GLA_HEADS):
        hmask[h, 0, h * GLA_DK:(h + 1) * GLA_DK] = 1.0
    vv = np.arange(GLA_V)[:, None] // GLA_DV
    kk = np.arange(GLA_QK)[None, :] // GLA_DK
    bdiag = (vv == kk).astype(np.float32)
    return dict(tri=tri, m_fwd=m_fwd, m_bwd=m_bwd, d_ret=d_ret.astype(np.float32),
                qdec=qdec.astype(np.float32), kdec=kdec.astype(np.float32),
                step_decay=step_decay, hmask=hmask, bdiag=bdiag)


def _mixer_kernel(step_decay, steps_per_seq, n_cast, x_ref, mod_ref, g_ref, w_ref, wga_ref, *rest):
    proj_a, proj_b, o_scr, st_g, st_r = rest[-5:]
    n_mix_in = len(rest) - 5 - 2 * n_cast - 1
    mix_in = rest[:n_mix_in]
    cast_src = rest[n_mix_in:n_mix_in + n_cast]
    y_ref = rest[n_mix_in + n_cast]
    cast_dst = rest[n_mix_in + n_cast + 1:n_mix_in + 2 * n_cast + 1]
    j = pl.program_id(0)

    @pl.when(j == 0)
    def _():
        proj_b[...] = jnp.zeros_like(proj_b)

    @pl.when((j - 1) % steps_per_seq == 0)
    def _():
        st_g[...] = jnp.zeros_like(st_g)
        st_r[...] = jnp.zeros_like(st_r)

    def step(proj_w, proj_r):
        m = mod_ref[0, 0]
        h = _rms_mod(x_ref[0], g_ref[0], _row(m, MOD_SC1), _row(m, MOD_SH1)).astype(BF16)

        def block(c0):
            def emit():
                proj_w[:, c0:c0 + PROJ_BLOCK] = _dot(h, w_ref[0, :, c0:c0 + PROJ_BLOCK])
            return emit

        def gate_block():
            proj_w[:, OFF_GA:] = _dot(h, wga_ref[0])

        for src, dst in zip(cast_src, cast_dst):
            dst[...] = src[...].astype(BF16)

        pending = [block(c0) for c0 in range(0, PROJ_WIDTH, PROJ_BLOCK)] + [gate_block]
        _attn_body(step_decay, pending, proj_r, *mix_in, y_ref, o_scr, st_g, st_r)

    @pl.when(j % 2 == 0)
    def _():
        step(proj_a, proj_b)

    @pl.when(j % 2 == 1)
    def _():
        step(proj_b, proj_a)


def _attn_body(step_decay, pending, proj_ref, cos_ref, sin_ref, walpha_ref, balpha_ref,
               glag_ref, gng_ref, gnb_ref, tri_ref, mfwd_ref, mbwd_ref, dret_ref,
               qdec_ref, kdec_ref, hmask_ref, bdiag_ref, xres_ref, modres_ref, wout_ref, y_ref,
               o_scr, st_g, st_r):
    r = ATTN_ROWS
    pending = list(pending)

    def interleave(n=1):
        for _ in range(min(n, len(pending))):
            pending.pop(0)()


    z = _dot(proj_ref[:, OFF_GA:].astype(BF16), walpha_ref[0]) + balpha_ref[0]
    interleave(2)
    log_a = (jnp.minimum(z, 0.0) - jnp.log(1.0 + jnp.exp(-jnp.abs(z)))) * (1.0 / GLA_TAU)
    hi = log_a.astype(BF16)
    r1 = log_a - hi.astype(F32)
    mid = r1.astype(BF16)
    lo = (r1 - mid.astype(F32)).astype(BF16)
    tri = tri_ref[...]
    b = _dot(tri, hi) + _dot(tri, mid) + _dot(tri, lo)
    interleave(2)
    b3 = b.reshape(ATTN_CHUNKS, CHUNK, GLA_QK)
    b_last = b3[:, CHUNK - 1:CHUNK, :]
    k_upd_scale = jnp.exp(b_last - b3).reshape(r, GLA_QK)
    eb = jnp.exp(b)
    enb = jnp.exp(-b)
    q = proj_ref[:,OFF_GQ:OFF_GQ + GLA_QK] * (GLA_DK ** -0.5)
    k = proj_ref[:,OFF_GK:OFF_GK + GLA_QK]
    v_bf = proj_ref[:,OFF_GV:OFF_GV + GLA_V].astype(BF16)
    q_f = q * eb
    q_b = q * enb
    k_f = (k * enb).astype(BF16)
    k_b = (k * eb).astype(BF16)
    k_u = (k * k_upd_scale).astype(BF16)
    q_f_bf = q_f.astype(BF16)

    bdiag = bdiag_ref[...]
    inter = []
    for g in range(ATTN_CHUNKS):
        rows = slice(g * CHUNK, (g + 1) * CHUNK)
        st = st_g[...]
        inter.append(_dot_nt(q_f_bf[rows], st.astype(BF16)))
        upd_t = _dot_tn(v_bf[rows], k_u[rows]) * bdiag
        st_g[...] = st * jnp.exp(b_last[g]) + upd_t
        interleave()
    o_inter = jnp.concatenate(inter, axis=0)

    m_fwd = mfwd_ref[...] > 0.5
    m_bwd = mbwd_ref[...] > 0.5
    for h in range(GLA_HEADS):
        hm = hmask_ref[h]
        s_f = _dot_nt((q_f * hm).astype(BF16), k_f)
        s_b = _dot_nt((q_b * hm).astype(BF16), k_b)
        sc = jnp.where(m_fwd, s_f, jnp.where(m_bwd, s_b, 0.0)).astype(BF16)
        lanes = slice(h * GLA_DV, (h + 1) * GLA_DV)
        o_h = _dot(sc, v_bf[:, lanes]) + o_inter[:, lanes]
        interleave()
        o_h = o_h * lax.rsqrt(jnp.mean(o_h * o_h, axis=-1, keepdims=True) + EPS)
        gate = proj_ref[:,OFF_GR + h * GLA_DV:OFF_GR + (h + 1) * GLA_DV]
        o_h = o_h * glag_ref[0, :, lanes] * (gate * _sigmoid(gate))
        o_scr[:, lanes] = o_h.astype(BF16)

    cos = cos_ref[0]
    sin = sin_ref[0]
    for h in range(RET_HEADS):
        lanes = slice(h * RET_DK, (h + 1) * RET_DK)
        qh = proj_ref[:,OFF_RQ + h * RET_DK:OFF_RQ + (h + 1) * RET_DK]
        kh = proj_ref[:,OFF_RK + h * RET_DK:OFF_RK + (h + 1) * RET_DK]
        vh = proj_ref[:,OFF_RV + h * RET_DV:OFF_RV + (h + 1) * RET_DV].astype(BF16)
        qh = (qh * cos + pltpu.roll(qh, RET_DK // 2, axis=1) * sin) * (RET_DK ** -0.5)
        kh = kh * cos + pltpu.roll(kh, RET_DK // 2, axis=1) * sin
        s = _dot_nt(qh.astype(BF16), kh.astype(BF16)) * dret_ref[h]
        st = st_r[h]
        o_h = _dot(s.astype(BF16), vh) + _dot((qh * qdec_ref[:, lanes]).astype(BF16), st.astype(BF16))
        st_r[h] = st * step_decay[h] + _dot_tn((kh * kdec_ref[:, lanes]).astype(BF16), vh)
        interleave()
        mu =jnp.mean(o_h, axis=-1, keepdims=True)
        d = o_h - mu
        var = jnp.mean(d * d, axis=-1, keepdims=True)
        o_h = d * lax.rsqrt(var + EPS) * gng_ref[0, :, lanes] + gnb_ref[0, :, lanes]
        gate = proj_ref[:,OFF_RG + h * RET_DV:OFF_RG + (h + 1) * RET_DV]
        o_h = o_h * (gate * _sigmoid(gate))
        o_scr[:, GLA_V + h * RET_DV:GLA_V + (h + 1) * RET_DV] = o_h.astype(BF16)
    interleave(len(pending))
    y_ref[0] = xres_ref[0] + _row(modres_ref[0, 0], MOD_G1) * _dot(o_scr[...], wout_ref[0])


def _mixer(layer, x, mod_all, g_all, w_main_all, w_ga_all, cos, sin, w_alpha_all, b_alpha_all,
           gla_g_all, gn_g_all, gn_b_all, w_out_all, cast_jobs):
    bsz, seq, _ = x.shape
    r = ATTN_ROWS
    steps_per_seq = seq // r
    n_tiles = bsz * steps_per_seq
    c = _attn_consts()
    const2 = lambda shape: pl.BlockSpec(shape, lambda j: (0,) * len(shape))
    per_layer = lambda a: _layer_spec(a.shape, layer)
    proj_tile = lambda j: jnp.minimum(j, n_tiles - 1)
    mix_tile = lambda j: jnp.maximum(j - 1, 0)
    mix_blk = lambda w: pl.BlockSpec((1, r, w), lambda j: (mix_tile(j), 0, 0))

    def chunk_spec(a, chunks, first):
        return pl.BlockSpec((1,) + a.shape[1:], lambda j: (first + jnp.minimum(j, chunks - 1), 0, 0))

    cast_in = [chunk_spec(a, chunks, li * chunks) for a, chunks, li in cast_jobs]
    cast_out = [chunk_spec(a, chunks, 0) for a, chunks, li in cast_jobs]
    cast_shapes = [jax.ShapeDtypeStruct((chunks,) + a.shape[1:], BF16) for a, chunks, li in cast_jobs]
    out = pl.pallas_call(
        functools.partial(_mixer_kernel, c["step_decay"], steps_per_seq, len(cast_jobs)),
        grid=(n_tiles + 1,),
        in_specs=[
            pl.BlockSpec((1, r, D_MODEL), lambda j: (proj_tile(j), 0, 0)),
            pl.BlockSpec((1, 1, 6, D_MODEL), lambda j: (layer, proj_tile(j) // steps_per_seq, 0, 0)),
            per_layer(g_all), per_layer(w_main_all), per_layer(w_ga_all),
            mix_blk(RET_DK), mix_blk(RET_DK),
            per_layer(w_alpha_all), per_layer(b_alpha_all),
            per_layer(gla_g_all), per_layer(gn_g_all), per_layer(gn_b_all),
            const2((r, r)), const2((r, r)), const2((r, r)), const2((RET_HEADS, r, r)),
            const2((r, RET_QK)), const2((r, RET_QK)),
            const2((GLA_HEADS, 1, GLA_QK)), const2((GLA_V, GLA_QK)),
            mix_blk(D_MODEL),
            pl.BlockSpec((1, 1, 6, D_MODEL), lambda j: (layer, mix_tile(j) // steps_per_seq, 0, 0)),
            per_layer(w_out_all),
        ] + cast_in,
        out_specs=[mix_blk(D_MODEL)] + cast_out,
        out_shape=[jax.ShapeDtypeStruct((n_tiles, r, D_MODEL), F32)] + cast_shapes,
        scratch_shapes=[pltpu.VMEM((r, PROJ_WIDTH + LANES), F32),
                        pltpu.VMEM((r, PROJ_WIDTH + LANES), F32),
                        pltpu.VMEM((r, MIX_WIDTH), BF16),
                        pltpu.VMEM((GLA_V, GLA_QK), F32),
                        pltpu.VMEM((RET_HEADS, RET_DK, RET_DV), F32)],
        compiler_params=pltpu.CompilerParams(
            dimension_semantics=("arbitrary",), vmem_limit_bytes=VMEM_LIMIT_BYTES),
        name="mixer",
    )(x.reshape(n_tiles, r, D_MODEL), mod_all, g_all, w_main_all, w_ga_all,
      cos.reshape(n_tiles, r, RET_DK), sin.reshape(n_tiles, r, RET_DK),
      w_alpha_all, b_alpha_all, gla_g_all, gn_g_all, gn_b_all,
      jnp.asarray(c["tri"], BF16), jnp.asarray(c["m_fwd"]), jnp.asarray(c["m_bwd"]),
      jnp.asarray(c["d_ret"]), jnp.asarray(c["qdec"]), jnp.asarray(c["kdec"]),
      jnp.asarray(c["hmask"]), jnp.asarray(c["bdiag"]),
      x.reshape(n_tiles, r, D_MODEL), mod_all, w_out_all, *[a for a, _, _ in cast_jobs])
    return out[0].reshape(x.shape), out[1:]


ROUTE_ROWS = 512
EXPERT_ROWS = 512
SEG_ALIGN = 8
LOCAL_ROWS = -(-(2 * ROUTE_ROWS + N_EXPERTS * (SEG_ALIGN - 1)) // LANES) * LANES
PIECE_BITS = (max(ROUTE_ROWS, EXPERT_ROWS) // SEG_ALIGN).bit_length()
INFO_POS, INFO_PROB = 0, 2


def _route_kernel(x_ref, mod_ref, g_ref, rw_ref, tril_ref, upper_ref, h_ref, info_ref, cnt_ref):
    m = mod_ref[0, 0]
    h = _rms_mod(x_ref[...], g_ref[0], _row(m, MOD_SC2), _row(m, MOD_SH2))
    h_hi = h.astype(BF16)
    h_ref[...] = h_hi
    h_lo = (h - h_hi.astype(F32)).astype(BF16)
    hh = _dot(h_hi, rw_ref[...])
    logits = hh[:, :LANES] + hh[:, LANES:] + _dot(h_lo, rw_ref[:, :LANES])
    lane = lax.broadcasted_iota(jnp.int32, logits.shape, 1)
    neg = jnp.float32(-jnp.inf)
    lg = jnp.where(lane < N_EXPERTS, logits, neg)
    m1 = jnp.max(lg, axis=-1, keepdims=True)
    i1 = jnp.min(jnp.where(lg == m1, lane, LANES), axis=-1, keepdims=True)
    lg2 = jnp.where(lane == i1, neg, lg)
    m2 = jnp.max(lg2, axis=-1, keepdims=True)
    i2 = jnp.min(jnp.where(lg2 == m2, lane, LANES), axis=-1, keepdims=True)
    e2 = jnp.exp(m2 - m1)
    p1 = 1.0 / (1.0 + e2)
    p2 = e2 * p1
    sel1 = lane == i1
    sel2 = lane == i2
    onehot = jnp.where(sel1, 1.0, 0.0) + jnp.where(sel2, 1.0, 0.0)
    incl = _dot(tril_ref[...], onehot.astype(BF16))
    counts = incl[ROUTE_ROWS - 1:ROUTE_ROWS]
    seg_units = jnp.floor((counts + (SEG_ALIGN - 1)) * (1.0 / SEG_ALIGN))
    seg_units8 = jnp.broadcast_to(seg_units, (8, LANES)).astype(BF16)
    seg_off = _dot(seg_units8, upper_ref[...])[0:1] * SEG_ALIGN
    pos = incl - onehot + seg_off
    pos1 = jnp.sum(jnp.where(sel1, pos, 0.0), axis=-1, keepdims=True)
    pos2 = jnp.sum(jnp.where(sel2, pos, 0.0), axis=-1, keepdims=True)
    cnt_ref[...] = jnp.broadcast_to(counts, (8, LANES))
    rec = jnp.zeros(logits.shape, F32)
    for lane_id, val in ((INFO_POS, pos1), (INFO_POS + 1, pos2), (INFO_PROB, p1), (INFO_PROB + 1, p2)):
        rec = jnp.where(lane == lane_id, val, rec)
    info_ref[...] = rec


def _route(layer, x2, mod_all, g_all, router_w, rows_per_batch):
    n = x2.shape[0]
    tm = ROUTE_ROWS
    per_b = rows_per_batch // tm
    w_hi = router_w.astype(BF16)
    w_lo = (router_w - w_hi.astype(F32)).astype(BF16)
    pad = ((0, 0), (0, LANES - N_EXPERTS))
    rw = jnp.concatenate([jnp.pad(w_hi, pad), jnp.pad(w_lo, pad)], axis=1)
    tril = jnp.asarray(np.tril(np.ones((tm, tm), np.float32)), BF16)
    upper = jnp.asarray(np.triu(np.ones((LANES, LANES), np.float32), 1), BF16)
    return pl.pallas_call(
        _route_kernel,
        grid=(n // tm,),
        in_specs=[pl.BlockSpec((tm, D_MODEL), lambda i: (i, 0)),
                  pl.BlockSpec((1, 1, 6, D_MODEL), lambda i: (layer, i // per_b, 0, 0)),
                  _layer_spec(g_all.shape, layer),
                  pl.BlockSpec((D_MODEL, 2 * LANES), lambda i: (0, 0)),
                  pl.BlockSpec((tm, tm), lambda i: (0, 0)),
                  pl.BlockSpec((LANES, LANES), lambda i: (0, 0))],
        out_specs=[pl.BlockSpec((tm, D_MODEL), lambda i: (i, 0)),
                   pl.BlockSpec((tm, LANES), lambda i: (i, 0)),
                   pl.BlockSpec((8, LANES), lambda i: (i, 0))],
        out_shape=[jax.ShapeDtypeStruct((n, D_MODEL), BF16),
                   jax.ShapeDtypeStruct((n, LANES), F32),
                   jax.ShapeDtypeStruct((n // tm * 8, LANES), F32)],
        compiler_params=pltpu.CompilerParams(
            dimension_semantics=("arbitrary",), vmem_limit_bytes=VMEM_LIMIT_BYTES),
        name="route",
    )(x2, mod_all, g_all, rw, tril, upper)


def _rows(ref, row, n_rows):
    return ref.at[pl.ds(pl.multiple_of(row, SEG_ALIGN), n_rows)]


def _for_each_piece(n_units, fn):
    for bit in reversed(range(PIECE_BITS)):
        covered = (n_units >> (bit + 1)) << (bit + 1)

        @pl.when(((n_units >> bit) & 1) == 1)
        def _():
            fn(covered * SEG_ALIGN, SEG_ALIGN << bit)


def _for_each_segment_piece(seg_ref, n_seg, tile, fn):
    for e in range(N_EXPERTS):
        s = tile * N_EXPERTS + e
        sorted_row = seg_ref[s]
        local_row = seg_ref[n_seg + s]
        _for_each_piece(seg_ref[2 * n_seg + s],
                        lambda first, n_rows: fn(local_row + first, sorted_row + first, n_rows))


def _selection(info, coeff1, coeff2):
    lane = lax.broadcasted_iota(jnp.int32, (info.shape[0], LOCAL_ROWS), 1)
    pos1 = info[:, INFO_POS:INFO_POS + 1].astype(jnp.int32)
    pos2 = info[:, INFO_POS + 1:INFO_POS + 2].astype(jnp.int32)
    return jnp.where(lane == pos1, coeff1, jnp.where(lane == pos2, coeff2, 0.0)).astype(BF16)


def _dispatch_kernel(n_seg, seg_ref, pad_ref, h_ref, info_ref, xs_ref, sorted_scr, zero_scr, sems, pad_sem):
    i = pl.program_id(0)
    n_steps = pl.num_programs(0)
    slot = i % 2

    def move(tile, buf_slot, op):
        def piece(local_row, sorted_row, n_rows):
            cp = pltpu.make_async_copy(_rows(sorted_scr.at[buf_slot], local_row, n_rows),
                                       _rows(xs_ref, sorted_row, n_rows), sems.at[buf_slot])
            op(cp)
        _for_each_segment_piece(seg_ref, n_seg, tile, piece)

    @pl.when(i == 0)
    def _():
        zero_scr[...] = jnp.zeros_like(zero_scr)
        for op in (lambda cp: cp.start(), lambda cp: cp.wait()):
            for e in range(N_EXPERTS):
                lo = pad_ref[e]
                _for_each_piece(pad_ref[N_EXPERTS + e],
                                lambda first, n_rows: op(pltpu.make_async_copy(
                                    zero_scr.at[pl.ds(0, n_rows)], _rows(xs_ref, lo + first, n_rows),
                                    pad_sem)))

        def tile_copy(j):
            return pltpu.make_async_copy(zero_scr, _rows(xs_ref, j * EXPERT_ROWS, EXPERT_ROWS), pad_sem)

        def fill_tile(j, carry):
            tile_copy(j).start()
            return carry

        def drain_tile(j, carry):
            tile_copy(j).wait()
            return carry

        n_tiles = xs_ref.shape[0] // EXPERT_ROWS
        lax.fori_loop(pad_ref[2 * N_EXPERTS], n_tiles, fill_tile, 0)
        lax.fori_loop(pad_ref[2 * N_EXPERTS], n_tiles, drain_tile, 0)

    @pl.when(i >= 2)
    def _():
        move(i - 2, slot, lambda cp: cp.wait())

    sel = _selection(info_ref[...], 1.0, 1.0)
    sorted_scr[slot] = _dot_tn(sel, h_ref[...])
    move(i, slot, lambda cp: cp.start())

    @pl.when(i == n_steps - 1)
    def _():
        @pl.when(i >= 1)
        def _():
            move(i - 1, 1 - slot, lambda cp: cp.wait())
        move(i, slot, lambda cp: cp.wait())


def _dispatch(h2, info, segs, pads, n_sorted):
    n = h2.shape[0]
    tm = ROUTE_ROWS
    n_seg = n // tm * N_EXPERTS
    return pl.pallas_call(
        functools.partial(_dispatch_kernel, n_seg),
        grid_spec=pltpu.PrefetchScalarGridSpec(
            num_scalar_prefetch=2,
            grid=(n // tm,),
            in_specs=[pl.BlockSpec((tm, D_MODEL), lambda i, s, p: (i, 0)),
                      pl.BlockSpec((tm, LANES), lambda i, s, p: (i, 0))],
            out_specs=pl.BlockSpec(memory_space=pl.ANY),
            scratch_shapes=[pltpu.VMEM((2, LOCAL_ROWS, D_MODEL), F32),
                            pltpu.VMEM((EXPERT_ROWS, D_MODEL), F32),
                            pltpu.SemaphoreType.DMA((2,)), pltpu.SemaphoreType.DMA(())]),
        out_shape=jax.ShapeDtypeStruct((n_sorted, D_MODEL), F32),
        compiler_params=pltpu.CompilerParams(
            dimension_semantics=("arbitrary",), vmem_limit_bytes=VMEM_LIMIT_BYTES),
        name="dispatch",
    )(segs, pads, h2, info)


def _expert_kernel(tile_blk, tile_e, n_act, x_ref, w1_ref, w3_ref, w2_ref, y_ref):
    active = pl.program_id(0) < n_act[0]

    @pl.when(active)
    def _():
        h = x_ref[...].astype(BF16)
        a = _dot(h, w1_ref[0])
        u = (a * _sigmoid(a) * _dot(h, w3_ref[0])).astype(BF16)
        y_ref[...] = _dot(u, w2_ref[0])

    @pl.when(jnp.logical_not(active))
    def _():
        y_ref[...] = jnp.zeros_like(y_ref)


def _experts(xs, tile_blk, tile_e, n_act, w1, w3, w2):
    n_sorted = xs.shape[0]
    tm = EXPERT_ROWS
    f = w1.shape[-1]
    return pl.pallas_call(
        _expert_kernel,
        grid_spec=pltpu.PrefetchScalarGridSpec(
            num_scalar_prefetch=3,
            grid=(n_sorted // tm,),
            in_specs=[pl.BlockSpec((tm, D_MODEL), lambda i, b, e, n: (b[i], 0)),
                      pl.BlockSpec((1, D_MODEL, f), lambda i, b, e, n: (e[i], 0, 0)),
                      pl.BlockSpec((1, D_MODEL, f), lambda i, b, e, n: (e[i], 0, 0)),
                      pl.BlockSpec((1, f, D_MODEL), lambda i, b, e, n: (e[i], 0, 0))],
            out_specs=pl.BlockSpec((tm, D_MODEL), lambda i, b, e, n: (i, 0))),
        out_shape=jax.ShapeDtypeStruct((n_sorted, D_MODEL), F32),
        compiler_params=pltpu.CompilerParams(
            dimension_semantics=("arbitrary",), vmem_limit_bytes=VMEM_LIMIT_BYTES),
        name="experts",
    )(tile_blk, tile_e, n_act, xs, w1, w3, w2)


def _combine_kernel(final_norm, n_seg, seg_ref, x_ref, mod_ref, info_ref, fg_ref, ys_ref, out_ref,
                    ybuf, sems):
    i = pl.program_id(0)
    n_steps = pl.num_programs(0)
    slot = i % 2

    def move(tile, buf_slot, op):
        def piece(local_row, sorted_row, n_rows):
            cp = pltpu.make_async_copy(_rows(ys_ref, sorted_row, n_rows),
                                       _rows(ybuf.at[buf_slot], local_row, n_rows), sems.at[buf_slot])
            op(cp)
        _for_each_segment_piece(seg_ref, n_seg, tile, piece)

    @pl.when(i == 0)
    def _():
        ybuf[...] = jnp.zeros_like(ybuf)
        move(0, 0, lambda cp: cp.start())

    @pl.when(i + 1 < n_steps)
    def _():
        move(i + 1, 1 - slot, lambda cp: cp.start())

    move(i, slot, lambda cp: cp.wait())
    info = info_ref[...]
    sel = _selection(info, info[:, INFO_PROB:INFO_PROB + 1], info[:, INFO_PROB + 1:INFO_PROB + 2])
    y = _dot(sel, ybuf[slot].astype(BF16))
    out = x_ref[...] + _row(mod_ref[0, 0], MOD_G2) * y
    if final_norm:
        ms = jnp.mean(out * out, axis=-1, keepdims=True)
        out = out * lax.rsqrt(ms + EPS) * fg_ref[...]
    out_ref[...] = out


def _combine(layer, x2, mod_all, info, ys, segs, final_g, final_norm, rows_per_batch):
    n = x2.shape[0]
    tm = ROUTE_ROWS
    per_b = rows_per_batch // tm
    n_seg = n // tm * N_EXPERTS
    return pl.pallas_call(
        functools.partial(_combine_kernel, final_norm, n_seg),
        grid_spec=pltpu.PrefetchScalarGridSpec(
            num_scalar_prefetch=1,
            grid=(n // tm,),
            in_specs=[pl.BlockSpec((tm, D_MODEL), lambda i, s: (i, 0)),
                      pl.BlockSpec((1, 1, 6, D_MODEL), lambda i, s: (layer, i // per_b, 0, 0)),
                      pl.BlockSpec((tm, LANES), lambda i, s: (i, 0)),
                      pl.BlockSpec((1, D_MODEL), lambda i, s: (0, 0)),
                      pl.BlockSpec(memory_space=pl.ANY)],
            out_specs=pl.BlockSpec((tm, D_MODEL), lambda i, s: (i, 0)),
            scratch_shapes=[pltpu.VMEM((2, LOCAL_ROWS, D_MODEL), F32),
                            pltpu.SemaphoreType.DMA((2,))]),
        out_shape=jax.ShapeDtypeStruct(x2.shape, F32),
        compiler_params=pltpu.CompilerParams(
            dimension_semantics=("arbitrary",), vmem_limit_bytes=VMEM_LIMIT_BYTES),
        name="combine",
    )(segs, x2, mod_all, info, final_g, ys)


def _moe(layer, x, mod_all, g_all, router_w, w1, w3, w2, final_g, final_norm):
    bsz, seq, _ = x.shape
    n = bsz * seq
    tm = EXPERT_ROWS
    n_tok_tiles = n // ROUTE_ROWS
    x2 = x.reshape(n, D_MODEL)
    h2, info, cnt = _route(layer, x2, mod_all, g_all, router_w, seq)
    counts = cnt.reshape(n_tok_tiles, 8, LANES)[:, 0, :N_EXPERTS].astype(jnp.int32)
    seg_rows = (counts + (SEG_ALIGN - 1)) // SEG_ALIGN * SEG_ALIGN
    group_rows = jnp.sum(seg_rows, axis=0)
    tiles = (group_rows + (tm - 1)) // tm
    tile_end = jnp.cumsum(tiles)
    group_off = (tile_end - tiles) * tm
    seg_sorted = group_off[None, :] + jnp.cumsum(seg_rows, axis=0) - seg_rows
    seg_local = jnp.cumsum(seg_rows, axis=1) - seg_rows
    segs = jnp.concatenate([seg_sorted.reshape(-1), seg_local.reshape(-1),
                            (seg_rows // SEG_ALIGN).reshape(-1)]).astype(jnp.int32)
    max_rows = 2 * n + n_tok_tiles * N_EXPERTS * (SEG_ALIGN - 1)
    n_tiles = -(-max_rows // tm) + N_EXPERTS
    n_act = tile_end[-1]
    t = jnp.minimum(jnp.arange(n_tiles, dtype=jnp.int32), n_act - 1)
    tile_e = jnp.sum(t[:, None] >= tile_end[None, :], axis=1).astype(jnp.int32)
    pad_lo = group_off + group_rows
    pads = jnp.concatenate([pad_lo, (tile_end * tm - pad_lo) // SEG_ALIGN,
                            n_act[None]]).astype(jnp.int32)
    xs = _dispatch(h2, info, segs, pads, n_tiles * tm)
    ys = _experts(xs, t, tile_e, n_act.reshape(1).astype(jnp.int32), w1, w3, w2)
    out = _combine(layer, x2, mod_all, info, ys, segs, final_g, final_norm, seq)
    return out.reshape(x.shape)


def _ffn_kernel(final_norm, n_groups, x_ref, mod_ref, g_ref, w1_ref, w3_ref, w2_ref, fg_ref, y_ref,
                h_scr, acc_scr):
    e = pl.program_id(2)
    m = mod_ref[0, 0]

    @pl.when(e == 0)
    def _():
        h_scr[...] = _rms_mod(x_ref[0], g_ref[0], _row(m, MOD_SC2), _row(m, MOD_SH2)).astype(BF16)
        acc_scr[...] = jnp.zeros_like(acc_scr)

    h = h_scr[...]
    a = _dot(h, w1_ref[...])
    u = (a * _sigmoid(a) * _dot(h, w3_ref[...])).astype(BF16)
    acc_scr[...] += _dot(u, w2_ref[...])

    @pl.when(e == n_groups - 1)
    def _():
        out = x_ref[0] + _row(m, MOD_G2) * acc_scr[...]
        if final_norm:
            ms = jnp.mean(out * out, axis=-1, keepdims=True)
            out = out * lax.rsqrt(ms + EPS) * fg_ref[...]
        y_ref[0] = out


def _ffn(layer, x, mod_all, g_all, w1, w3, w2, final_g, final_norm):
    bsz, seq, _ = x.shape
    n_groups = 2
    f = w1.shape[-1] // n_groups
    tm = 1024
    row = pl.BlockSpec((1, tm, D_MODEL), lambda b, i, e: (b, i, 0))
    return pl.pallas_call(
        functools.partial(_ffn_kernel, final_norm, n_groups),
        grid=(bsz, seq // tm, n_groups),
        in_specs=[row,
                  pl.BlockSpec((1, 1, 6, D_MODEL), lambda b, i, e: (layer, b, 0, 0)),
                  _layer_spec(g_all.shape, layer),
                  pl.BlockSpec((D_MODEL, f), lambda b, i, e: (0, e)),
                  pl.BlockSpec((D_MODEL, f), lambda b, i, e: (0, e)),
                  pl.BlockSpec((f, D_MODEL), lambda b, i, e: (e, 0)),
                  pl.BlockSpec((1, D_MODEL), lambda b, i, e: (0, 0))],
        out_specs=row,
        out_shape=jax.ShapeDtypeStruct(x.shape, F32),
        scratch_shapes=[pltpu.VMEM((tm, D_MODEL), BF16), pltpu.VMEM((tm, D_MODEL), F32)],
        compiler_params=pltpu.CompilerParams(
            dimension_semantics=("arbitrary", "arbitrary", "arbitrary"),
            vmem_limit_bytes=VMEM_LIMIT_BYTES),
        name="ffn",
    )(x, mod_all, g_all, w1, w3, w2, final_g)


def kernel(x, c, positions, ada_w, ada_b, norm_mix_g, norm_ffn_g, w_in, gla_w_alpha, gla_b_alpha,
           gla_norm_g, ret_gn_g, ret_gn_b, w_out, ffn_w1, ffn_w3, ffn_w2, router_w, moe_w1,
           moe_w3, moe_w2, final_g):
    bsz = x.shape[0]
    mod_all = _ada_mod(c, ada_w, ada_b).reshape(DEPTH, bsz, 6, D_MODEL)
    cos, sin = _rope_tables(positions)
    ga_lo = 2 * GLA_QK + 2 * GLA_V
    w_main_all = jnp.concatenate([w_in[:, :, :ga_lo], w_in[:, :, ga_lo + GLA_RANK:]], axis=2).astype(BF16)
    w_ga_all = jnp.pad(w_in[:, :, ga_lo:ga_lo + GLA_RANK],
                       ((0, 0), (0, 0), (0, LANES - GLA_RANK))).astype(BF16)
    w_alpha_all = jnp.pad(gla_w_alpha, ((0, 0), (0, LANES - GLA_RANK), (0, 0))).astype(BF16)
    vec = lambda a: a.reshape(a.shape[0], 1, a.shape[1])
    b_alpha_all, gla_g_all, gn_g_all, gn_b_all = map(vec, (gla_b_alpha, gla_norm_g, ret_gn_g, ret_gn_b))
    g_mix_all, g_ffn_all = vec(norm_mix_g), vec(norm_ffn_g)
    w_out_all = w_out.astype(BF16)
    fg = final_g.reshape(1, D_MODEL)
    d_ff = ffn_w1.shape[-1]
    d_fe = moe_w1.shape[-1]
    n_dense, n_moe = ffn_w1.shape[0], moe_w1.shape[0]
    dense_src = (ffn_w1.reshape(n_dense * FFN_CHUNKS, D_MODEL // FFN_CHUNKS, d_ff),
                 ffn_w3.reshape(n_dense * FFN_CHUNKS, D_MODEL // FFN_CHUNKS, d_ff),
                 ffn_w2.reshape(n_dense * FFN_CHUNKS, d_ff // FFN_CHUNKS, D_MODEL))
    moe_chunks = N_EXPERTS * MOE_CHUNKS_PER_EXPERT
    moe_src = (moe_w1.reshape(n_moe * moe_chunks, D_MODEL // MOE_CHUNKS_PER_EXPERT, d_fe),
               moe_w3.reshape(n_moe * moe_chunks, D_MODEL // MOE_CHUNKS_PER_EXPERT, d_fe),
               moe_w2.reshape(n_moe * moe_chunks, d_fe // MOE_CHUNKS_PER_EXPERT, D_MODEL))
    for layer in range(DEPTH):
        li = layer // 2
        dense = layer % 2 == 0
        jobs = [(a, FFN_CHUNKS if dense else moe_chunks, li) for a in (dense_src if dense else moe_src)]
        x, (w1, w3, w2) = _mixer(layer, x, mod_all, g_mix_all, w_main_all, w_ga_all, cos, sin,
                                 w_alpha_all, b_alpha_all, gla_g_all, gn_g_all, gn_b_all, w_out_all, jobs)
        last = layer == DEPTH - 1
        if dense:
            x = _ffn(layer, x, mod_all, g_ffn_all, w1.reshape(D_MODEL, d_ff), w3.reshape(D_MODEL, d_ff),
                     w2.reshape(d_ff, D_MODEL), fg, last)
        else:
            x = _moe(layer, x, mod_all, g_ffn_all, router_w[li], w1.reshape(N_EXPERTS, D_MODEL, d_fe),
                     w3.reshape(N_EXPERTS, D_MODEL, d_fe), w2.reshape(N_EXPERTS, d_fe, D_MODEL), fg, last)
    return x
```

```python
import functools

import numpy as np
import jax
import jax.numpy as jnp
from jax import lax
from jax.experimental import pallas as pl
from jax.experimental.pallas import tpu as pltpu

F32 = jnp.float32
BF16 = jnp.bfloat16

D_MODEL = 1024
DEPTH = 4
CHUNK = 64
GLA_HEADS = 4
GLA_DK = 64
GLA_DV = 128
GLA_RANK = 16
GLA_TAU = 16.0
RET_HEADS = 4
RET_DK = 128
RET_DV = 128
ROPE_BASE = 10000.0
N_EXPERTS = 8
EPS = 1e-6
GLA_QK = GLA_HEADS * GLA_DK
GLA_V = GLA_HEADS * GLA_DV
RET_QK = RET_HEADS * RET_DK
RET_V = RET_HEADS * RET_DV
MIX_WIDTH = GLA_V + RET_V

LANES = 128
VMEM_LIMIT_BYTES = 56 * 1024 * 1024

PROJ_WIDTH = 2 * GLA_QK + 2 * GLA_V + 2 * RET_QK + 2 * RET_V
OFF_GQ = 0
OFF_GK = OFF_GQ + GLA_QK
OFF_GV = OFF_GK + GLA_QK
OFF_GR = OFF_GV + GLA_V
OFF_RQ = OFF_GR + GLA_V
OFF_RK = OFF_RQ + RET_QK
OFF_RV = OFF_RK + RET_QK
OFF_RG = OFF_RV + RET_V

ATTN_ROWS = 256
ATTN_CHUNKS = ATTN_ROWS // CHUNK

MOD_SH1, MOD_SC1, MOD_G1, MOD_SH2, MOD_SC2, MOD_G2 = range(6)


def _sigmoid(x):
    return 1.0 / (1.0 + jnp.exp(-x))


def _rms_mod(x, g, sc, sh):
    ms = jnp.mean(x * x, axis=-1, keepdims=True)
    return (x * lax.rsqrt(ms + EPS)) * g * (1.0 + sc) + sh


def _dot(a, b):
    return jnp.dot(a, b, preferred_element_type=F32)


def _dot_nt(a, b):
    return lax.dot_general(a, b, (((1,), (1,)), ((), ())), preferred_element_type=F32)


def _dot_tn(a, b):
    return lax.dot_general(a, b, (((0,), (0,)), ((), ())), preferred_element_type=F32)


def _row(m, r):
    return m[r:r + 1]


def _layer_spec(shape, layer):
    zeros = (0,) * (len(shape) - 1)
    return pl.BlockSpec((1,) + tuple(shape[1:]), lambda *_: (layer,) + zeros)


def _ada_kernel(c_ref, w_ref, b_ref, o_ref):
    c = c_ref[...]
    cond = c * _sigmoid(c)
    c_hi = cond.astype(BF16)
    c_lo = (cond - c_hi.astype(F32)).astype(BF16)
    w = w_ref[0]
    w_hi = w.astype(BF16)
    w_lo = (w - w_hi.astype(F32)).astype(BF16)
    o_ref[0] = _dot(c_hi, w_hi) + _dot(c_hi, w_lo) + _dot(c_lo, w_hi) + b_ref[0]


def _ada_mod(c, ada_w, ada_b):
    bsz = c.shape[0]
    tn = 1024
    n_out = ada_w.shape[-1]
    return pl.pallas_call(
        _ada_kernel,
        grid=(DEPTH, n_out // tn),
        in_specs=[
            pl.BlockSpec((bsz, D_MODEL), lambda l, j: (0, 0)),
            pl.BlockSpec((1, D_MODEL, tn), lambda l, j: (l, 0, j)),
            pl.BlockSpec((1, 1, tn), lambda l, j: (l, 0, j)),
        ],
        out_specs=pl.BlockSpec((1, bsz, tn), lambda l, j: (l, 0, j)),
        out_shape=jax.ShapeDtypeStruct((DEPTH, bsz, n_out), F32),
        name="ada_mod",
    )(c, ada_w, ada_b.reshape(DEPTH, 1, n_out))


def _rope_kernel(pos_ref, invf_ref, sign_ref, cos_ref, sin_ref):
    ang = pos_ref[...] * invf_ref[...]
    cos_ref[...] = jnp.cos(ang)
    sin_ref[...] = jnp.sin(ang) * sign_ref[...]


def _rope_tables(positions):
    n = positions.size
    half = RET_DK // 2
    inv_freq = ROPE_BASE ** (-jnp.arange(half, dtype=F32) / half)
    invf = jnp.concatenate([inv_freq, inv_freq]).reshape(1, RET_DK)
    sign = jnp.concatenate([-jnp.ones((half,), F32), jnp.ones((half,), F32)]).reshape(1, RET_DK)
    pos = positions.astype(F32).reshape(n, 1)
    tm = 2048
    row = pl.BlockSpec((1, RET_DK), lambda i: (0, 0))
    out = pl.BlockSpec((tm, RET_DK), lambda i: (i, 0))
    return pl.pallas_call(
        _rope_kernel,
        grid=(n // tm,),
        in_specs=[pl.BlockSpec((tm, 1), lambda i: (i, 0)), row, row],
        out_specs=[out, out],
        out_shape=[jax.ShapeDtypeStruct((n, RET_DK), F32)] * 2,
        name="rope_tables",
    )(pos, invf, sign)


OFF_GA = PROJ_WIDTH
PROJ_BLOCK = 256
TAIL_JOBS = 2
FFN_CHUNKS = 16
MOE_CHUNKS_PER_EXPERT = 8


def _attn_consts():
    r = ATTN_ROWS
    t = np.arange(r)
    same = (t[:, None] // CHUNK) == (t[None, :] // CHUNK)
    causal = t[:, None] >= t[None, :]
    tri = (same & causal).astype(np.float32)
    m_fwd = tri
    m_bwd = (same & ~causal).astype(np.float32)
    gam = 1.0 - 2.0 ** (-5.0 - np.arange(RET_HEADS, dtype=np.float64))
    lg = np.log(gam)
    dist = (t[:, None] - t[None, :]).astype(np.float64)
    d_ret = np.where(causal[None], np.exp(lg[:, None, None] * dist[None]),
                     np.where(same[None], np.exp(-lg[:, None, None] * dist[None]), 0.0))
    qdec = np.repeat(np.exp(lg[None, :] * (t[:, None] + 1.0)), RET_DK, axis=1)
    kdec = np.repeat(np.exp(lg[None, :] * (r - 1.0 - t[:, None])), RET_DK, axis=1)
    step_decay = [float(np.exp(lg[h] * r)) for h in range(RET_HEADS)]
    hmask = np.zeros((GLA_HEADS, 1, GLA_QK), np.float32)
    for h in range(GLA_HEADS):
        hmask[h, 0, h * GLA_DK:(h + 1) * GLA_DK] = 1.0
    vv = np.arange(GLA_V)[:, None] // GLA_DV
    kk = np.arange(GLA_QK)[None, :] // GLA_DK
    bdiag = (vv == kk).astype(np.float32)
    return dict(tri=tri, m_fwd=m_fwd, m_bwd=m_bwd, d_ret=d_ret.astype(np.float32),
                qdec=qdec.astype(np.float32), kdec=kdec.astype(np.float32),
                step_decay=step_decay, hmask=hmask, bdiag=bdiag)


def _mixer_kernel(step_decay, steps_per_seq, n_cast, x_ref, mod_ref, g_ref, wa_ref, wb_ref, wga_ref,
                  *rest):
    proj_a, proj_b, o_scr, st_g, st_r = rest[-5:]
    n_mix_in = len(rest) - 5 - 2 * n_cast - 2
    mix_in = rest[:n_mix_in]
    cast_src = rest[n_mix_in:n_mix_in + n_cast]
    y_ref, h2_ref = rest[n_mix_in + n_cast:n_mix_in + n_cast + 2]
    cast_dst = rest[n_mix_in + n_cast + 2:n_mix_in + 2 * n_cast + 2]
    j = pl.program_id(0)

    @pl.when(j == 0)
    def _():
        proj_b[...] = jnp.zeros_like(proj_b)

    @pl.when((j - 1) % steps_per_seq == 0)
    def _():
        st_g[...] = jnp.zeros_like(st_g)
        st_r[...] = jnp.zeros_like(st_r)

    def step(proj_w, proj_r):
        m = mod_ref[0, 0]
        h = _rms_mod(x_ref[0], g_ref[0], _row(m, MOD_SC1), _row(m, MOD_SH1)).astype(BF16)

        def block(w_ref, c0, out0):
            def emit():
                proj_w[:, out0:out0 + PROJ_BLOCK] = _dot(h, w_ref[0, :, c0:c0 + PROJ_BLOCK])
            return emit

        def gate_block():
            proj_w[:, OFF_GA:] = _dot(h, wga_ref[0])

        for src, dst in zip(cast_src, cast_dst):
            dst[...] = src[...].astype(BF16)

        split = wa_ref.shape[-1]
        pending = ([block(wa_ref, c0, c0) for c0 in range(0, split, PROJ_BLOCK)]
                   + [gate_block]
                   + [block(wb_ref, c0, split + c0) for c0 in range(0, PROJ_WIDTH - split, PROJ_BLOCK)])
        _attn_body(step_decay, pending, proj_r, *mix_in, y_ref, h2_ref, o_scr, st_g, st_r)

    @pl.when(j % 2 == 0)
    def _():
        step(proj_a, proj_b)

    @pl.when(j % 2 == 1)
    def _():
        step(proj_b, proj_a)


def _attn_body(step_decay, pending, proj_ref, cos_ref, sin_ref, walpha_ref, balpha_ref,
               glag_ref, gng_ref, gnb_ref, tri_ref, mfwd_ref, mbwd_ref, dret_ref,
               qdec_ref, kdec_ref, hmask_ref, bdiag_ref, xres_ref, modres_ref, wout_ref, gffn_ref,
               y_ref, h2_ref, o_scr, st_g, st_r):
    r = ATTN_ROWS
    pending = list(pending)

    def interleave(n=1):
        for _ in range(min(n, len(pending) - TAIL_JOBS)):
            pending.pop(0)()

    z = _dot(proj_ref[:, OFF_GA:].astype(BF16), walpha_ref[0]) + balpha_ref[0]
    interleave(2)
    log_a = (jnp.minimum(z, 0.0) - jnp.log(1.0 + jnp.exp(-jnp.abs(z)))) * (1.0 / GLA_TAU)
    hi = log_a.astype(BF16)
    r1 = log_a - hi.astype(F32)
    mid = r1.astype(BF16)
    lo = (r1 - mid.astype(F32)).astype(BF16)
    tri = tri_ref[...]
    b = _dot(tri, hi) + _dot(tri, mid) + _dot(tri, lo)
    interleave(2)
    b3 = b.reshape(ATTN_CHUNKS, CHUNK, GLA_QK)
    b_last = b3[:, CHUNK - 1:CHUNK, :]
    k_upd_scale = jnp.exp(b_last - b3).reshape(r, GLA_QK)
    eb = jnp.exp(b)
    enb = jnp.exp(-b)
    q = proj_ref[:,OFF_GQ:OFF_GQ + GLA_QK] * (GLA_DK ** -0.5)
    k = proj_ref[:,OFF_GK:OFF_GK + GLA_QK]
    v_bf = proj_ref[:,OFF_GV:OFF_GV + GLA_V].astype(BF16)
    q_f = q * eb
    q_b = q * enb
    k_f = (k * enb).astype(BF16)
    k_b = (k * eb).astype(BF16)
    k_u = (k * k_upd_scale).astype(BF16)
    q_f_bf = q_f.astype(BF16)

    bdiag = bdiag_ref[...]
    inter = []
    for g in range(ATTN_CHUNKS):
        rows = slice(g * CHUNK, (g + 1) * CHUNK)
        st = st_g[...]
        inter.append(_dot_nt(q_f_bf[rows], st.astype(BF16)))
        upd_t = _dot_tn(v_bf[rows], k_u[rows]) * bdiag
        st_g[...] = st * jnp.exp(b_last[g]) + upd_t
        interleave()
    o_inter = jnp.concatenate(inter, axis=0)

    m_fwd = mfwd_ref[...] > 0.5
    m_bwd = mbwd_ref[...] > 0.5
    for h in range(GLA_HEADS):
        hm = hmask_ref[h]
        s_f = _dot_nt((q_f * hm).astype(BF16), k_f)
        s_b = _dot_nt((q_b * hm).astype(BF16), k_b)
        sc = jnp.where(m_fwd, s_f, jnp.where(m_bwd, s_b, 0.0)).astype(BF16)
        lanes = slice(h * GLA_DV, (h + 1) * GLA_DV)
        o_h = _dot(sc, v_bf[:, lanes]) + o_inter[:, lanes]
        interleave()
        o_h = o_h * lax.rsqrt(jnp.mean(o_h * o_h, axis=-1, keepdims=True) + EPS)
        gate = proj_ref[:,OFF_GR + h * GLA_DV:OFF_GR + (h + 1) * GLA_DV]
        o_h = o_h * glag_ref[0, :, lanes] * (gate * _sigmoid(gate))
        o_scr[:, lanes] = o_h.astype(BF16)

    cos = cos_ref[0]
    sin = sin_ref[0]
    for h in range(RET_HEADS):
        lanes = slice(h * RET_DK, (h + 1) * RET_DK)
        qh = proj_ref[:,OFF_RQ + h * RET_DK:OFF_RQ + (h + 1) * RET_DK]
        kh = proj_ref[:,OFF_RK + h * RET_DK:OFF_RK + (h + 1) * RET_DK]
        vh = proj_ref[:,OFF_RV + h * RET_DV:OFF_RV + (h + 1) * RET_DV].astype(BF16)
        qh = (qh * cos + pltpu.roll(qh, RET_DK // 2, axis=1) * sin) * (RET_DK ** -0.5)
        kh = kh * cos + pltpu.roll(kh, RET_DK // 2, axis=1) * sin
        s = _dot_nt(qh.astype(BF16), kh.astype(BF16)) * dret_ref[h]
        st = st_r[h]
        o_h = _dot(s.astype(BF16), vh) + _dot((qh * qdec_ref[:, lanes]).astype(BF16), st.astype(BF16))
        st_r[h] = st * step_decay[h] + _dot_tn((kh * kdec_ref[:, lanes]).astype(BF16), vh)
        interleave()
        mu =jnp.mean(o_h, axis=-1, keepdims=True)
        d = o_h - mu
        var = jnp.mean(d * d, axis=-1, keepdims=True)
        o_h = d * lax.rsqrt(var + EPS) * gng_ref[0, :, lanes] + gnb_ref[0, :, lanes]
        gate = proj_ref[:,OFF_RG + h * RET_DV:OFF_RG + (h + 1) * RET_DV]
        o_h = o_h * (gate * _sigmoid(gate))
        o_scr[:, GLA_V + h * RET_DV:GLA_V + (h + 1) * RET_DV] = o_h.astype(BF16)
    interleave(len(pending))
    m = modres_ref[0, 0]
    y = xres_ref[0] + _row(m, MOD_G1) * _dot(o_scr[...], wout_ref[0])
    for job in pending:
        job()
    y_ref[0] = y
    h2_ref[0] = _rms_mod(y, gffn_ref[0], _row(m, MOD_SC2), _row(m, MOD_SH2)).astype(h2_ref.dtype)


def _mixer(layer, x, mod_all, g_all, w_a_all, w_b_all, w_ga_all, cos, sin, w_alpha_all, b_alpha_all,
           gla_g_all, gn_g_all, gn_b_all, w_out_all, g_ffn_all, h2_dtype, cast_jobs):
    bsz, seq, _ = x.shape
    r = ATTN_ROWS
    steps_per_seq = seq // r
    n_tiles = bsz * steps_per_seq
    c = _attn_consts()
    const2 = lambda shape: pl.BlockSpec(shape, lambda j: (0,) * len(shape))
    per_layer = lambda a: _layer_spec(a.shape, layer)
    proj_tile = lambda j: jnp.minimum(j, n_tiles - 1)
    mix_tile = lambda j: jnp.maximum(j - 1, 0)
    mix_blk = lambda w: pl.BlockSpec((1, r, w), lambda j: (mix_tile(j), 0, 0))

    def chunk_spec(a, chunks, first):
        return pl.BlockSpec((1,) + a.shape[1:], lambda j: (first + jnp.minimum(j, chunks - 1), 0, 0))

    cast_in = [chunk_spec(a, chunks, li * chunks) for a, chunks, li in cast_jobs]
    cast_out = [chunk_spec(a, chunks, 0) for a, chunks, li in cast_jobs]
    cast_shapes = [jax.ShapeDtypeStruct((chunks,) + a.shape[1:], BF16) for a, chunks, li in cast_jobs]
    out = pl.pallas_call(
        functools.partial(_mixer_kernel, c["step_decay"], steps_per_seq, len(cast_jobs)),
        grid=(n_tiles + 1,),
        in_specs=[
            pl.BlockSpec((1, r, D_MODEL), lambda j: (proj_tile(j), 0, 0)),
            pl.BlockSpec((1, 1, 6, D_MODEL), lambda j: (layer, proj_tile(j) // steps_per_seq, 0, 0)),
            per_layer(g_all), per_layer(w_a_all), per_layer(w_b_all), per_layer(w_ga_all),
            mix_blk(RET_DK), mix_blk(RET_DK),
            per_layer(w_alpha_all), per_layer(b_alpha_all),
            per_layer(gla_g_all), per_layer(gn_g_all), per_layer(gn_b_all),
            const2((r, r)), const2((r, r)), const2((r, r)), const2((RET_HEADS, r, r)),
            const2((r, RET_QK)), const2((r, RET_QK)),
            const2((GLA_HEADS, 1, GLA_QK)), const2((GLA_V, GLA_QK)),
            mix_blk(D_MODEL),
            pl.BlockSpec((1, 1, 6, D_MODEL), lambda j: (layer, mix_tile(j) // steps_per_seq, 0, 0)),
            per_layer(w_out_all), per_layer(g_ffn_all),
        ] + cast_in,
        out_specs=[mix_blk(D_MODEL), mix_blk(D_MODEL)] + cast_out,
        out_shape=[jax.ShapeDtypeStruct((n_tiles, r, D_MODEL), F32),
                   jax.ShapeDtypeStruct((n_tiles, r, D_MODEL), h2_dtype)] + cast_shapes,
        scratch_shapes=[pltpu.VMEM((r, PROJ_WIDTH + LANES), F32),
                        pltpu.VMEM((r, PROJ_WIDTH + LANES), F32),
                        pltpu.VMEM((r, MIX_WIDTH), BF16),
                        pltpu.VMEM((GLA_V, GLA_QK), F32),
                        pltpu.VMEM((RET_HEADS, RET_DK, RET_DV), F32)],
        compiler_params=pltpu.CompilerParams(
            dimension_semantics=("arbitrary",), vmem_limit_bytes=VMEM_LIMIT_BYTES),
        name="mixer",
    )(x.reshape(n_tiles, r, D_MODEL), mod_all, g_all, w_a_all, w_b_all, w_ga_all,
      cos.reshape(n_tiles, r, RET_DK), sin.reshape(n_tiles, r, RET_DK),
      w_alpha_all, b_alpha_all, gla_g_all, gn_g_all, gn_b_all,
      jnp.asarray(c["tri"], BF16), jnp.asarray(c["m_fwd"]), jnp.asarray(c["m_bwd"]),
      jnp.asarray(c["d_ret"]), jnp.asarray(c["qdec"]), jnp.asarray(c["kdec"]),
      jnp.asarray(c["hmask"]), jnp.asarray(c["bdiag"]),
      x.reshape(n_tiles, r, D_MODEL), mod_all, w_out_all, g_ffn_all, *[a for a, _, _ in cast_jobs])
    return out[0].reshape(x.shape), out[1].reshape(x.shape), out[2:]


ROUTE_ROWS = 512
EXPERT_ROWS = 512
SEG_ALIGN = 8
LOCAL_ROWS = -(-(2 * ROUTE_ROWS + N_EXPERTS * (SEG_ALIGN - 1)) // LANES) * LANES
PIECE_BITS = (max(ROUTE_ROWS, EXPERT_ROWS) // SEG_ALIGN).bit_length()
INFO_POS, INFO_PROB = 0, 2


def _route_kernel(hin_ref, rw_ref, tril_ref, upper_ref, h_ref, info_ref, cnt_ref):
    h = hin_ref[...]
    h_hi = h.astype(BF16)
    h_ref[...] = h_hi
    h_lo = (h - h_hi.astype(F32)).astype(BF16)
    hh = _dot(h_hi, rw_ref[...])
    logits = hh[:, :LANES] + hh[:, LANES:] + _dot(h_lo, rw_ref[:, :LANES])
    lane = lax.broadcasted_iota(jnp.int32, logits.shape, 1)
    neg = jnp.float32(-jnp.inf)
    lg = jnp.where(lane < N_EXPERTS, logits, neg)
    m1 = jnp.max(lg, axis=-1, keepdims=True)
    i1 = jnp.min(jnp.where(lg == m1, lane, LANES), axis=-1, keepdims=True)
    lg2 = jnp.where(lane == i1, neg, lg)
    m2 = jnp.max(lg2, axis=-1, keepdims=True)
    i2 = jnp.min(jnp.where(lg2 == m2, lane, LANES), axis=-1, keepdims=True)
    e2 = jnp.exp(m2 - m1)
    p1 = 1.0 / (1.0 + e2)
    p2 = e2 * p1
    sel1 = lane == i1
    sel2 = lane == i2
    onehot = jnp.where(sel1, 1.0, 0.0) + jnp.where(sel2, 1.0, 0.0)
    incl = _dot(tril_ref[...], onehot.astype(BF16))
    counts = incl[ROUTE_ROWS - 1:ROUTE_ROWS]
    seg_units = jnp.floor((counts + (SEG_ALIGN - 1)) * (1.0 / SEG_ALIGN))
    seg_units8 = jnp.broadcast_to(seg_units, (8, LANES)).astype(BF16)
    seg_off = _dot(seg_units8, upper_ref[...])[0:1] * SEG_ALIGN
    pos = incl - onehot + seg_off
    pos1 = jnp.sum(jnp.where(sel1, pos, 0.0), axis=-1, keepdims=True)
    pos2 = jnp.sum(jnp.where(sel2, pos, 0.0), axis=-1, keepdims=True)
    cnt_ref[...] = jnp.broadcast_to(counts, (8, LANES))
    rec = jnp.zeros(logits.shape, F32)
    for lane_id, val in ((INFO_POS, pos1), (INFO_POS + 1, pos2), (INFO_PROB, p1), (INFO_PROB + 1, p2)):
        rec = jnp.where(lane == lane_id, val, rec)
    info_ref[...] = rec


def _route(h_f32, router_w):
    n = h_f32.shape[0]
    tm = ROUTE_ROWS
    w_hi = router_w.astype(BF16)
    w_lo = (router_w - w_hi.astype(F32)).astype(BF16)
    pad = ((0, 0), (0, LANES - N_EXPERTS))
    rw = jnp.concatenate([jnp.pad(w_hi, pad), jnp.pad(w_lo, pad)], axis=1)
    tril = jnp.asarray(np.tril(np.ones((tm, tm), np.float32)), BF16)
    upper = jnp.asarray(np.triu(np.ones((LANES, LANES), np.float32), 1), BF16)
    return pl.pallas_call(
        _route_kernel,
        grid=(n // tm,),
        in_specs=[pl.BlockSpec((tm, D_MODEL), lambda i: (i, 0)),
                  pl.BlockSpec((D_MODEL, 2 * LANES), lambda i: (0, 0)),
                  pl.BlockSpec((tm, tm), lambda i: (0, 0)),
                  pl.BlockSpec((LANES, LANES), lambda i: (0, 0))],
        out_specs=[pl.BlockSpec((tm, D_MODEL), lambda i: (i, 0)),
                   pl.BlockSpec((tm, LANES), lambda i: (i, 0)),
                   pl.BlockSpec((8, LANES), lambda i: (i, 0))],
        out_shape=[jax.ShapeDtypeStruct((n, D_MODEL), BF16),
                   jax.ShapeDtypeStruct((n, LANES), F32),
                   jax.ShapeDtypeStruct((n // tm * 8, LANES), F32)],
        compiler_params=pltpu.CompilerParams(
            dimension_semantics=("arbitrary",), vmem_limit_bytes=VMEM_LIMIT_BYTES),
        name="route",
    )(h_f32, rw, tril, upper)


def _rows(ref, row, n_rows):
    return ref.at[pl.ds(pl.multiple_of(row, SEG_ALIGN), n_rows)]


def _for_each_piece(n_units, fn):
    for bit in reversed(range(PIECE_BITS)):
        covered = (n_units >> (bit + 1)) << (bit + 1)

        @pl.when(((n_units >> bit) & 1) == 1)
        def _():
            fn(covered * SEG_ALIGN, SEG_ALIGN << bit)


def _for_each_segment_piece(seg_ref, n_seg, tile, fn):
    for e in range(N_EXPERTS):
        s = tile * N_EXPERTS + e
        sorted_row = seg_ref[s]
        local_row = seg_ref[n_seg + s]
        _for_each_piece(seg_ref[2 * n_seg + s],
                        lambda first, n_rows: fn(local_row + first, sorted_row + first, n_rows))


def _selection(info, coeff1, coeff2):
    lane = lax.broadcasted_iota(jnp.int32, (info.shape[0], LOCAL_ROWS), 1)
    pos1 = info[:, INFO_POS:INFO_POS + 1].astype(jnp.int32)
    pos2 = info[:, INFO_POS + 1:INFO_POS + 2].astype(jnp.int32)
    return jnp.where(lane == pos1, coeff1, jnp.where(lane == pos2, coeff2, 0.0)).astype(BF16)


def _dispatch_kernel(n_seg, seg_ref, pad_ref, h_ref, info_ref, xs_ref, sorted_scr, zero_scr, sems, pad_sem):
    i = pl.program_id(0)
    n_steps = pl.num_programs(0)
    slot = i % 2

    def move(tile, buf_slot, op):
        def piece(local_row, sorted_row, n_rows):
            cp = pltpu.make_async_copy(_rows(sorted_scr.at[buf_slot], local_row, n_rows),
                                       _rows(xs_ref, sorted_row, n_rows), sems.at[buf_slot])
            op(cp)
        _for_each_segment_piece(seg_ref, n_seg, tile, piece)

    @pl.when(i == 0)
    def _():
        zero_scr[...] = jnp.zeros_like(zero_scr)
        for op in (lambda cp: cp.start(), lambda cp: cp.wait()):
            for e in range(N_EXPERTS):
                lo = pad_ref[e]
                _for_each_piece(pad_ref[N_EXPERTS + e],
                                lambda first, n_rows: op(pltpu.make_async_copy(
                                    zero_scr.at[pl.ds(0, n_rows)], _rows(xs_ref, lo + first, n_rows),
                                    pad_sem)))

        def tile_copy(j):
            return pltpu.make_async_copy(zero_scr, _rows(xs_ref, j * EXPERT_ROWS, EXPERT_ROWS), pad_sem)

        def fill_tile(j, carry):
            tile_copy(j).start()
            return carry

        def drain_tile(j, carry):
            tile_copy(j).wait()
            return carry

        n_tiles = xs_ref.shape[0] // EXPERT_ROWS
        lax.fori_loop(pad_ref[2 * N_EXPERTS], n_tiles, fill_tile, 0)
        lax.fori_loop(pad_ref[2 * N_EXPERTS], n_tiles, drain_tile, 0)

    @pl.when(i >= 2)
    def _():
        move(i - 2, slot, lambda cp: cp.wait())

    sel = _selection(info_ref[...], 1.0, 1.0)
    sorted_scr[slot] = _dot_tn(sel, h_ref[...])
    move(i, slot, lambda cp: cp.start())

    @pl.when(i == n_steps - 1)
    def _():
        @pl.when(i >= 1)
        def _():
            move(i - 1, 1 - slot, lambda cp: cp.wait())
        move(i, slot, lambda cp: cp.wait())


def _dispatch(h2, info, segs, pads, n_sorted):
    n = h2.shape[0]
    tm = ROUTE_ROWS
    n_seg = n // tm * N_EXPERTS
    return pl.pallas_call(
        functools.partial(_dispatch_kernel, n_seg),
        grid_spec=pltpu.PrefetchScalarGridSpec(
            num_scalar_prefetch=2,
            grid=(n // tm,),
            in_specs=[pl.BlockSpec((tm, D_MODEL), lambda i, s, p: (i, 0)),
                      pl.BlockSpec((tm, LANES), lambda i, s, p: (i, 0))],
            out_specs=pl.BlockSpec(memory_space=pl.ANY),
            scratch_shapes=[pltpu.VMEM((2, LOCAL_ROWS, D_MODEL), F32),
                            pltpu.VMEM((EXPERT_ROWS, D_MODEL), F32),
                            pltpu.SemaphoreType.DMA((2,)), pltpu.SemaphoreType.DMA(())]),
        out_shape=jax.ShapeDtypeStruct((n_sorted, D_MODEL), F32),
        compiler_params=pltpu.CompilerParams(
            dimension_semantics=("arbitrary",), vmem_limit_bytes=VMEM_LIMIT_BYTES),
        name="dispatch",
    )(segs, pads, h2, info)


def _expert_kernel(tile_blk, tile_e, n_act, x_ref, w1_ref, w3_ref, w2_ref, y_ref):
    active = pl.program_id(0) < n_act[0]

    @pl.when(active)
    def _():
        h = x_ref[...].astype(BF16)
        a = _dot(h, w1_ref[0])
        u = (a * _sigmoid(a) * _dot(h, w3_ref[0])).astype(BF16)
        y_ref[...] = _dot(u, w2_ref[0])

    @pl.when(jnp.logical_not(active))
    def _():
        y_ref[...] = jnp.zeros_like(y_ref)


def _experts(xs, tile_blk, tile_e, n_act, w1, w3, w2):
    n_sorted = xs.shape[0]
    tm = EXPERT_ROWS
    f = w1.shape[-1]
    return pl.pallas_call(
        _expert_kernel,
        grid_spec=pltpu.PrefetchScalarGridSpec(
            num_scalar_prefetch=3,
            grid=(n_sorted // tm,),
            in_specs=[pl.BlockSpec((tm, D_MODEL), lambda i, b, e, n: (b[i], 0)),
                      pl.BlockSpec((1, D_MODEL, f), lambda i, b, e, n: (e[i], 0, 0)),
                      pl.BlockSpec((1, D_MODEL, f), lambda i, b, e, n: (e[i], 0, 0)),
                      pl.BlockSpec((1, f, D_MODEL), lambda i, b, e, n: (e[i], 0, 0))],
            out_specs=pl.BlockSpec((tm, D_MODEL), lambda i, b, e, n: (i, 0))),
        out_shape=jax.ShapeDtypeStruct((n_sorted, D_MODEL), F32),
        compiler_params=pltpu.CompilerParams(
            dimension_semantics=("arbitrary",), vmem_limit_bytes=VMEM_LIMIT_BYTES),
        name="experts",
    )(tile_blk, tile_e, n_act, xs, w1, w3, w2)


def _combine_kernel(final_norm, n_seg, seg_ref, x_ref, mod_ref, info_ref, fg_ref, ys_ref, out_ref,
                    ybuf, sems):
    i = pl.program_id(0)
    n_steps = pl.num_programs(0)
    slot = i % 2

    def move(tile, buf_slot, op):
        def piece(local_row, sorted_row, n_rows):
            cp = pltpu.make_async_copy(_rows(ys_ref, sorted_row, n_rows),
                                       _rows(ybuf.at[buf_slot], local_row, n_rows), sems.at[buf_slot])
            op(cp)
        _for_each_segment_piece(seg_ref, n_seg, tile, piece)

    @pl.when(i == 0)
    def _():
        ybuf[...] = jnp.zeros_like(ybuf)
        move(0, 0, lambda cp: cp.start())

    @pl.when(i + 1 < n_steps)
    def _():
        move(i + 1, 1 - slot, lambda cp: cp.start())

    move(i, slot, lambda cp: cp.wait())
    info = info_ref[...]
    sel = _selection(info, info[:, INFO_PROB:INFO_PROB + 1], info[:, INFO_PROB + 1:INFO_PROB + 2])
    y = _dot(sel, ybuf[slot].astype(BF16))
    out = x_ref[...] + _row(mod_ref[0, 0], MOD_G2) * y
    if final_norm:
        ms = jnp.mean(out * out, axis=-1, keepdims=True)
        out = out * lax.rsqrt(ms + EPS) * fg_ref[...]
    out_ref[...] = out


def _combine(layer, x2, mod_all, info, ys, segs, final_g, final_norm, rows_per_batch):
    n = x2.shape[0]
    tm = ROUTE_ROWS
    per_b = rows_per_batch // tm
    n_seg = n // tm * N_EXPERTS
    return pl.pallas_call(
        functools.partial(_combine_kernel, final_norm, n_seg),
        grid_spec=pltpu.PrefetchScalarGridSpec(
            num_scalar_prefetch=1,
            grid=(n // tm,),
            in_specs=[pl.BlockSpec((tm, D_MODEL), lambda i, s: (i, 0)),
                      pl.BlockSpec((1, 1, 6, D_MODEL), lambda i, s: (layer, i // per_b, 0, 0)),
                      pl.BlockSpec((tm, LANES), lambda i, s: (i, 0)),
                      pl.BlockSpec((1, D_MODEL), lambda i, s: (0, 0)),
                      pl.BlockSpec(memory_space=pl.ANY)],
            out_specs=pl.BlockSpec((tm, D_MODEL), lambda i, s: (i, 0)),
            scratch_shapes=[pltpu.VMEM((2, LOCAL_ROWS, D_MODEL), F32),
                            pltpu.SemaphoreType.DMA((2,))]),
        out_shape=jax.ShapeDtypeStruct(x2.shape, F32),
        compiler_params=pltpu.CompilerParams(
            dimension_semantics=("arbitrary",), vmem_limit_bytes=VMEM_LIMIT_BYTES),
        name="combine",
    )(segs, x2, mod_all, info, final_g, ys)


def _moe(layer, x, h_f32, mod_all, router_w, w1, w3, w2, final_g, final_norm):
    bsz, seq, _ = x.shape
    n = bsz * seq
    tm = EXPERT_ROWS
    n_tok_tiles = n // ROUTE_ROWS
    x2 = x.reshape(n, D_MODEL)
    h2, info, cnt = _route(h_f32.reshape(n, D_MODEL), router_w)
    counts = cnt.reshape(n_tok_tiles, 8, LANES)[:, 0, :N_EXPERTS].astype(jnp.int32)
    seg_rows = (counts + (SEG_ALIGN - 1)) // SEG_ALIGN * SEG_ALIGN
    group_rows = jnp.sum(seg_rows, axis=0)
    tiles = (group_rows + (tm - 1)) // tm
    tile_end = jnp.cumsum(tiles)
    group_off = (tile_end - tiles) * tm
    seg_sorted = group_off[None, :] + jnp.cumsum(seg_rows, axis=0) - seg_rows
    seg_local = jnp.cumsum(seg_rows, axis=1) - seg_rows
    segs = jnp.concatenate([seg_sorted.reshape(-1), seg_local.reshape(-1),
                            (seg_rows // SEG_ALIGN).reshape(-1)]).astype(jnp.int32)
    max_rows = 2 * n + n_tok_tiles * N_EXPERTS * (SEG_ALIGN - 1)
    n_tiles = -(-max_rows // tm) + N_EXPERTS
    n_act = tile_end[-1]
    t = jnp.minimum(jnp.arange(n_tiles, dtype=jnp.int32), n_act - 1)
    tile_e = jnp.sum(t[:, None] >= tile_end[None, :], axis=1).astype(jnp.int32)
    pad_lo = group_off + group_rows
    pads = jnp.concatenate([pad_lo, (tile_end * tm - pad_lo) // SEG_ALIGN,
                            n_act[None]]).astype(jnp.int32)
    xs = _dispatch(h2, info, segs, pads, n_tiles * tm)
    ys = _experts(xs, t, tile_e, n_act.reshape(1).astype(jnp.int32), w1, w3, w2)
    out = _combine(layer, x2, mod_all, info, ys, segs, final_g, final_norm, seq)
    return out.reshape(x.shape)


def _ffn_kernel(final_norm, n_groups, x_ref, h_ref, mod_ref, w1_ref, w3_ref, w2_ref, fg_ref, y_ref,
                acc_scr):
    e = pl.program_id(2)

    def group_product():
        h = h_ref[0]
        a = _dot(h, w1_ref[...])
        u = (a * _sigmoid(a) * _dot(h, w3_ref[...])).astype(BF16)
        return _dot(u, w2_ref[...])

    assert n_groups == 2

    @pl.when(e == 0)
    def _():
        acc_scr[...] = group_product()

    @pl.when(e == 1)
    def _():
        out = x_ref[0] + _row(mod_ref[0, 0], MOD_G2) * (acc_scr[...] + group_product())
        if final_norm:
            ms = jnp.mean(out * out, axis=-1, keepdims=True)
            out = out * lax.rsqrt(ms + EPS) * fg_ref[...]
        y_ref[0] = out


def _ffn(layer, x, h2, mod_all, w1, w3, w2, final_g, final_norm):
    bsz, seq, _ = x.shape
    n_groups = 2
    f = w1.shape[-1] // n_groups
    tm = 1024
    row = pl.BlockSpec((1, tm, D_MODEL), lambda b, i, e: (b, i, 0))
    return pl.pallas_call(
        functools.partial(_ffn_kernel, final_norm, n_groups),
        grid=(bsz, seq // tm, n_groups),
        in_specs=[row, row,
                  pl.BlockSpec((1, 1, 6, D_MODEL), lambda b, i, e: (layer, b, 0, 0)),
                  pl.BlockSpec((D_MODEL, f), lambda b, i, e: (0, e)),
                  pl.BlockSpec((D_MODEL, f), lambda b, i, e: (0, e)),
                  pl.BlockSpec((f, D_MODEL), lambda b, i, e: (e, 0)),
                  pl.BlockSpec((1, D_MODEL), lambda b, i, e: (0, 0))],
        out_specs=row,
        out_shape=jax.ShapeDtypeStruct(x.shape, F32),
        scratch_shapes=[pltpu.VMEM((tm, D_MODEL), F32)],
        compiler_params=pltpu.CompilerParams(
            dimension_semantics=("arbitrary", "arbitrary", "arbitrary"),
            vmem_limit_bytes=VMEM_LIMIT_BYTES),
        name="ffn",
    )(x, h2, mod_all, w1, w3, w2, final_g)


def kernel(x, c, positions, ada_w, ada_b, norm_mix_g, norm_ffn_g, w_in, gla_w_alpha, gla_b_alpha,
           gla_norm_g, ret_gn_g, ret_gn_b, w_out, ffn_w1, ffn_w3, ffn_w2, router_w, moe_w1,
           moe_w3, moe_w2, final_g):
    bsz = x.shape[0]
    mod_all = _ada_mod(c, ada_w, ada_b).reshape(DEPTH, bsz, 6, D_MODEL)
    cos, sin = _rope_tables(positions)
    ga_lo = 2 * GLA_QK + 2 * GLA_V
    w_a_all = w_in[:, :, :ga_lo].astype(BF16)
    w_b_all = w_in[:, :, ga_lo + GLA_RANK:].astype(BF16)
    w_ga_all = jnp.pad(w_in[:, :, ga_lo:ga_lo + GLA_RANK],
                       ((0, 0), (0, 0), (0, LANES - GLA_RANK))).astype(BF16)
    w_alpha_all = jnp.pad(gla_w_alpha, ((0, 0), (0, LANES - GLA_RANK), (0, 0))).astype(BF16)
    vec = lambda a: a.reshape(a.shape[0], 1, a.shape[1])
    b_alpha_all, gla_g_all, gn_g_all, gn_b_all = map(vec, (gla_b_alpha, gla_norm_g, ret_gn_g, ret_gn_b))
    g_mix_all, g_ffn_all = vec(norm_mix_g), vec(norm_ffn_g)
    w_out_all = w_out.astype(BF16)
    fg = final_g.reshape(1, D_MODEL)
    d_ff = ffn_w1.shape[-1]
    d_fe = moe_w1.shape[-1]
    n_dense, n_moe = ffn_w1.shape[0], moe_w1.shape[0]
    dense_src = (ffn_w1.reshape(n_dense * FFN_CHUNKS, D_MODEL // FFN_CHUNKS, d_ff),
                 ffn_w3.reshape(n_dense * FFN_CHUNKS, D_MODEL // FFN_CHUNKS, d_ff),
                 ffn_w2.reshape(n_dense * FFN_CHUNKS, d_ff // FFN_CHUNKS, D_MODEL))
    moe_chunks = N_EXPERTS * MOE_CHUNKS_PER_EXPERT
    moe_src = (moe_w1.reshape(n_moe * moe_chunks, D_MODEL // MOE_CHUNKS_PER_EXPERT, d_fe),
               moe_w3.reshape(n_moe * moe_chunks, D_MODEL // MOE_CHUNKS_PER_EXPERT, d_fe),
               moe_w2.reshape(n_moe * moe_chunks, d_fe // MOE_CHUNKS_PER_EXPERT, D_MODEL))
    for layer in range(DEPTH):
        li = layer // 2
        dense = layer % 2 == 0
        jobs = [(a, FFN_CHUNKS if dense else moe_chunks, li) for a in (dense_src if dense else moe_src)]
        x, h2, (w1, w3, w2) = _mixer(layer, x, mod_all, g_mix_all, w_a_all, w_b_all, w_ga_all, cos, sin,
                                     w_alpha_all, b_alpha_all, gla_g_all, gn_g_all, gn_b_all, w_out_all,
                                     g_ffn_all, BF16 if dense else F32, jobs)
        last = layer == DEPTH - 1
        if dense:
            x = _ffn(layer, x, h2, mod_all, w1.reshape(D_MODEL, d_ff), w3.reshape(D_MODEL, d_ff),
                     w2.reshape(d_ff, D_MODEL), fg, last)
        else:
            x = _moe(layer, x, h2, mod_all, router_w[li], w1.reshape(N_EXPERTS, D_MODEL, d_fe),
                     w3.reshape(N_EXPERTS, D_MODEL, d_fe), w2.reshape(N_EXPERTS, d_fe, D_MODEL), fg, last)
    return x
```

```python
import functools

import numpy as np
import jax
import jax.numpy as jnp
from jax import lax
from jax.experimental import pallas as pl
from jax.experimental.pallas import tpu as pltpu

F32 = jnp.float32
BF16 = jnp.bfloat16

D_MODEL = 1024
DEPTH = 4
CHUNK = 64
GLA_HEADS = 4
GLA_DK = 64
GLA_DV = 128
GLA_RANK = 16
GLA_TAU = 16.0
RET_HEADS = 4
RET_DK = 128
RET_DV = 128
ROPE_BASE = 10000.0
N_EXPERTS = 8
EPS = 1e-6
GLA_QK = GLA_HEADS * GLA_DK
GLA_V = GLA_HEADS * GLA_DV
RET_QK = RET_HEADS * RET_DK
RET_V = RET_HEADS * RET_DV
MIX_WIDTH = GLA_V + RET_V

LANES = 128
VMEM_LIMIT_BYTES = 56 * 1024 * 1024

PROJ_WIDTH = 2 * GLA_QK + 2 * GLA_V + 2 * RET_QK + 2 * RET_V
OFF_GQ = 0
OFF_GK = OFF_GQ + GLA_QK
OFF_GV = OFF_GK + GLA_QK
OFF_GR = OFF_GV + GLA_V
OFF_RQ = OFF_GR + GLA_V
OFF_RK = OFF_RQ + RET_QK
OFF_RV = OFF_RK + RET_QK
OFF_RG = OFF_RV + RET_V

ATTN_ROWS = 256
ATTN_CHUNKS = ATTN_ROWS // CHUNK

MOD_SH1, MOD_SC1, MOD_G1, MOD_SH2, MOD_SC2, MOD_G2 = range(6)


def _sigmoid(x):
    return 1.0 / (1.0 + jnp.exp(-x))


def _rms_mod(x, g, sc, sh):
    ms = jnp.mean(x * x, axis=-1, keepdims=True)
    return (x * lax.rsqrt(ms + EPS)) * g * (1.0 + sc) + sh


def _dot(a, b):
    return jnp.dot(a, b, preferred_element_type=F32)


def _dot_nt(a, b):
    return lax.dot_general(a, b, (((1,), (1,)), ((), ())), preferred_element_type=F32)


def _dot_tn(a, b):
    return lax.dot_general(a, b, (((0,), (0,)), ((), ())), preferred_element_type=F32)


def _row(m, r):
    return m[r:r + 1]


def _layer_spec(shape, layer):
    zeros = (0,) * (len(shape) - 1)
    return pl.BlockSpec((1,) + tuple(shape[1:]), lambda *_: (layer,) + zeros)


def _ada_kernel(c_ref, w_ref, b_ref, o_ref):
    c = c_ref[...]
    cond = c * _sigmoid(c)
    c_hi = cond.astype(BF16)
    c_lo = (cond - c_hi.astype(F32)).astype(BF16)
    w = w_ref[0]
    w_hi = w.astype(BF16)
    w_lo = (w - w_hi.astype(F32)).astype(BF16)
    o_ref[0] = _dot(c_hi, w_hi) + _dot(c_hi, w_lo) + _dot(c_lo, w_hi) + b_ref[0]


def _ada_mod(c, ada_w, ada_b):
    bsz = c.shape[0]
    tn = 1024
    n_out = ada_w.shape[-1]
    return pl.pallas_call(
        _ada_kernel,
        grid=(DEPTH, n_out // tn),
        in_specs=[
            pl.BlockSpec((bsz, D_MODEL), lambda l, j: (0, 0)),
            pl.BlockSpec((1, D_MODEL, tn), lambda l, j: (l, 0, j)),
            pl.BlockSpec((1, 1, tn), lambda l, j: (l, 0, j)),
        ],
        out_specs=pl.BlockSpec((1, bsz, tn), lambda l, j: (l, 0, j)),
        out_shape=jax.ShapeDtypeStruct((DEPTH, bsz, n_out), F32),
        name="ada_mod",
    )(c, ada_w, ada_b.reshape(DEPTH, 1, n_out))


def _rope_kernel(pos_ref, invf_ref, sign_ref, cos_ref, sin_ref):
    ang = pos_ref[...] * invf_ref[...]
    cos_ref[...] = jnp.cos(ang)
    sin_ref[...] = jnp.sin(ang) * sign_ref[...]


def _rope_tables(positions):
    n = positions.size
    half = RET_DK // 2
    inv_freq = ROPE_BASE ** (-jnp.arange(half, dtype=F32) / half)
    invf = jnp.concatenate([inv_freq, inv_freq]).reshape(1, RET_DK)
    sign = jnp.concatenate([-jnp.ones((half,), F32), jnp.ones((half,), F32)]).reshape(1, RET_DK)
    pos = positions.astype(F32).reshape(n, 1)
    tm = 2048
    row = pl.BlockSpec((1, RET_DK), lambda i: (0, 0))
    out = pl.BlockSpec((tm, RET_DK), lambda i: (i, 0))
    return pl.pallas_call(
        _rope_kernel,
        grid=(n // tm,),
        in_specs=[pl.BlockSpec((tm, 1), lambda i: (i, 0)), row, row],
        out_specs=[out, out],
        out_shape=[jax.ShapeDtypeStruct((n, RET_DK), F32)] * 2,
        name="rope_tables",
    )(pos, invf, sign)


GA_LO = 2 * GLA_QK + 2 * GLA_V


def _prep_w_in_kernel(w_ref, wa_ref, wb_ref, wga_ref):
    wa_ref[0] = w_ref[0, :, :GA_LO].astype(BF16)
    wb_ref[0] = w_ref[0, :, GA_LO + GLA_RANK:].astype(BF16)
    head = w_ref[0, :, GA_LO:GA_LO + LANES]
    lane = lax.broadcasted_iota(jnp.int32, head.shape, 1)
    wga_ref[0] = jnp.where(lane < GLA_RANK, head, 0.0).astype(BF16)


def _prep_w_in(w_in):
    depth, d, width = w_in.shape
    rows = 128
    wb_width = width - GA_LO - GLA_RANK
    blk = lambda w: pl.BlockSpec((1, rows, w), lambda l, i: (l, i, 0))
    return pl.pallas_call(
        _prep_w_in_kernel,
        grid=(depth, d // rows),
        in_specs=[blk(width)],
        out_specs=[blk(GA_LO), blk(wb_width), blk(LANES)],
        out_shape=[jax.ShapeDtypeStruct((depth, d, GA_LO), BF16),
                   jax.ShapeDtypeStruct((depth, d, wb_width), BF16),
                   jax.ShapeDtypeStruct((depth, d, LANES), BF16)],
        name="prep_w_in",
    )(w_in)


OFF_GA = PROJ_WIDTH
PROJ_BLOCK = 256
TAIL_JOBS = 2
FFN_CHUNKS = 16
MOE_CHUNKS_PER_EXPERT = 8


def _attn_consts():
    r = ATTN_ROWS
    t = np.arange(r)
    same = (t[:, None] // CHUNK) == (t[None, :] // CHUNK)
    causal = t[:, None] >= t[None, :]
    tri = (same & causal).astype(np.float32)
    m_fwd = tri
    m_bwd = (same & ~causal).astype(np.float32)
    gam = 1.0 - 2.0 ** (-5.0 - np.arange(RET_HEADS, dtype=np.float64))
    lg = np.log(gam)
    dist = (t[:, None] - t[None, :]).astype(np.float64)
    d_ret = np.where(causal[None], np.exp(lg[:, None, None] * dist[None]),
                     np.where(same[None], np.exp(-lg[:, None, None] * dist[None]), 0.0))
    qdec = np.repeat(np.exp(lg[None, :] * (t[:, None] + 1.0)), RET_DK, axis=1)
    kdec = np.repeat(np.exp(lg[None, :] * (r - 1.0 - t[:, None])), RET_DK, axis=1)
    step_decay = [float(np.exp(lg[h] * r)) for h in range(RET_HEADS)]
    hmask = np.zeros((GLA_HEADS, 1, GLA_QK), np.float32)
    for h in range(GLA_HEADS):
        hmask[h, 0, h * GLA_DK:(h + 1) * GLA_DK] = 1.0
    return dict(tri=tri, m_fwd=m_fwd, m_bwd=m_bwd, d_ret=d_ret.astype(np.float32),
                qdec=qdec.astype(np.float32), kdec=kdec.astype(np.float32),
                step_decay=step_decay, hmask=hmask)


def _mixer_kernel(step_decay, steps_per_seq, n_cast, x_ref, mod_ref, g_ref, wa_ref, wb_ref, wga_ref,
                  *rest):
    proj_a, proj_b, o_scr, st_g, st_g_bf, st_r = rest[-6:]
    n_mix_in = len(rest) - 6 - 2 * n_cast - 2
    mix_in = rest[:n_mix_in]
    cast_src = rest[n_mix_in:n_mix_in + n_cast]
    y_ref, h2_ref = rest[n_mix_in + n_cast:n_mix_in + n_cast + 2]
    cast_dst = rest[n_mix_in + n_cast + 2:n_mix_in + 2 * n_cast + 2]
    j = pl.program_id(0)

    @pl.when(j == 0)
    def _():
        proj_b[...] = jnp.zeros_like(proj_b)

    @pl.when((j - 1) % steps_per_seq == 0)
    def _():
        st_g[...] = jnp.zeros_like(st_g)
        st_g_bf[...] = jnp.zeros_like(st_g_bf)
        st_r[...] = jnp.zeros_like(st_r)

    def step(proj_w, proj_r):
        m = mod_ref[0, 0]
        h = _rms_mod(x_ref[0], g_ref[0], _row(m, MOD_SC1), _row(m, MOD_SH1)).astype(BF16)

        def block(w_ref, c0, out0):
            def emit():
                proj_w[:, out0:out0 + PROJ_BLOCK] = _dot(h, w_ref[0, :, c0:c0 + PROJ_BLOCK])
            return emit

        def gate_block():
            proj_w[:, OFF_GA:] = _dot(h, wga_ref[0])

        for src, dst in zip(cast_src, cast_dst):
            dst[...] = src[...].astype(BF16)

        split = wa_ref.shape[-1]
        pending = ([block(wa_ref, c0, c0) for c0 in range(0, split, PROJ_BLOCK)]
                   + [gate_block]
                   + [block(wb_ref, c0, split + c0) for c0 in range(0, PROJ_WIDTH - split, PROJ_BLOCK)])
        _attn_body(step_decay, pending, proj_r, *mix_in, y_ref, h2_ref, o_scr, st_g, st_g_bf, st_r)

    @pl.when(j % 2 == 0)
    def _():
        step(proj_a, proj_b)

    @pl.when(j % 2 == 1)
    def _():
        step(proj_b, proj_a)


def _attn_body(step_decay, pending, proj_ref, cos_ref, sin_ref, walpha_ref, balpha_ref,
               glag_ref, gng_ref, gnb_ref, tri_ref, mfwd_ref, mbwd_ref, dret_ref,
               qdec_ref, kdec_ref, hmask_ref, xres_ref, modres_ref, wout_ref, gffn_ref,
               y_ref, h2_ref, o_scr, st_g, st_g_bf, st_r):
    r = ATTN_ROWS
    pending = list(pending)

    def interleave(n=1):
        for _ in range(min(n, len(pending) - TAIL_JOBS)):
            pending.pop(0)()

    z = _dot(proj_ref[:, OFF_GA:].astype(BF16), walpha_ref[0]) + balpha_ref[0]
    interleave(2)
    log_a = (jnp.minimum(z, 0.0) - jnp.log(1.0 + jnp.exp(-jnp.abs(z)))) * (1.0 / GLA_TAU)
    hi = log_a.astype(BF16)
    r1 = log_a - hi.astype(F32)
    mid = r1.astype(BF16)
    lo = (r1 - mid.astype(F32)).astype(BF16)
    tri = tri_ref[...]
    b = _dot(tri, hi) + _dot(tri, mid) + _dot(tri, lo)
    interleave(2)
    b3 = b.reshape(ATTN_CHUNKS, CHUNK, GLA_QK)
    b_last = b3[:, CHUNK - 1:CHUNK, :]
    k_upd_scale = jnp.exp(b_last - b3).reshape(r, GLA_QK)
    eb = jnp.exp(b)
    enb = jnp.exp(-b)
    q = proj_ref[:,OFF_GQ:OFF_GQ + GLA_QK] * (GLA_DK ** -0.5)
    k = proj_ref[:,OFF_GK:OFF_GK + GLA_QK]
    v_bf = proj_ref[:,OFF_GV:OFF_GV + GLA_V].astype(BF16)
    q_f = q * eb
    q_b = q * enb
    k_f = (k * enb).astype(BF16)
    k_b = (k * eb).astype(BF16)
    k_u = (k * k_upd_scale).astype(BF16)
    q_f_bf = q_f.astype(BF16)

    b_t = b.T
    heads_per_group = LANES // GLA_DK
    inter = []
    for g in range(ATTN_CHUNKS):
        rows = slice(g * CHUNK, (g + 1) * CHUNK)
        inter.append(_dot(q_f_bf[rows], st_g_bf[...]))
        decay = jnp.exp(b_t[:, (g + 1) * CHUNK - 1:(g + 1) * CHUNK])
        for grp in range(GLA_QK // LANES):
            v_lanes = slice(grp * heads_per_group * GLA_DV, (grp + 1) * heads_per_group * GLA_DV)
            upd = _dot_tn(k_u[rows, grp * LANES:(grp + 1) * LANES], v_bf[rows, v_lanes])
            for i in range(heads_per_group):
                h = grp * heads_per_group + i
                keys = slice(h * GLA_DK, (h + 1) * GLA_DK)
                st = (st_g[h] * decay[keys]
                      + upd[i * GLA_DK:(i + 1) * GLA_DK, i * GLA_DV:(i + 1) * GLA_DV])
                st_g[h] = st
                st_g_bf[keys, h * GLA_DV:(h + 1) * GLA_DV] = st.astype(BF16)
        interleave()
    o_inter = jnp.concatenate(inter, axis=0)

    m_fwd = mfwd_ref[...] > 0.5
    m_bwd = mbwd_ref[...] > 0.5
    for h in range(GLA_HEADS):
        hm = hmask_ref[h]
        s_f = _dot_nt((q_f * hm).astype(BF16), k_f)
        s_b = _dot_nt((q_b * hm).astype(BF16), k_b)
        sc = jnp.where(m_fwd, s_f, jnp.where(m_bwd, s_b, 0.0)).astype(BF16)
        lanes = slice(h * GLA_DV, (h + 1) * GLA_DV)
        o_h = _dot(sc, v_bf[:, lanes]) + o_inter[:, lanes]
        interleave()
        o_h = o_h * lax.rsqrt(jnp.mean(o_h * o_h, axis=-1, keepdims=True) + EPS)
        gate = proj_ref[:,OFF_GR + h * GLA_DV:OFF_GR + (h + 1) * GLA_DV]
        o_h = o_h * glag_ref[0, :, lanes] * (gate * _sigmoid(gate))
        o_scr[:, lanes] = o_h.astype(BF16)

    cos = cos_ref[0]
    sin = sin_ref[0]
    for h in range(RET_HEADS):
        lanes = slice(h * RET_DK, (h + 1) * RET_DK)
        qh = proj_ref[:,OFF_RQ + h * RET_DK:OFF_RQ + (h + 1) * RET_DK]
        kh = proj_ref[:,OFF_RK + h * RET_DK:OFF_RK + (h + 1) * RET_DK]
        vh = proj_ref[:,OFF_RV + h * RET_DV:OFF_RV + (h + 1) * RET_DV].astype(BF16)
        qh = (qh * cos + pltpu.roll(qh, RET_DK // 2, axis=1) * sin) * (RET_DK ** -0.5)
        kh = kh * cos + pltpu.roll(kh, RET_DK // 2, axis=1) * sin
        s = _dot_nt(qh.astype(BF16), kh.astype(BF16)) * dret_ref[h]
        st = st_r[h]
        o_h = _dot(s.astype(BF16), vh) + _dot((qh * qdec_ref[:, lanes]).astype(BF16), st.astype(BF16))
        st_r[h] = st * step_decay[h] + _dot_tn((kh * kdec_ref[:, lanes]).astype(BF16), vh)
        interleave()
        mu =jnp.mean(o_h, axis=-1, keepdims=True)
        d = o_h - mu
        var = jnp.mean(d * d, axis=-1, keepdims=True)
        o_h = d * lax.rsqrt(var + EPS) * gng_ref[0, :, lanes] + gnb_ref[0, :, lanes]
        gate = proj_ref[:,OFF_RG + h * RET_DV:OFF_RG + (h + 1) * RET_DV]
        o_h = o_h * (gate * _sigmoid(gate))
        o_scr[:, GLA_V + h * RET_DV:GLA_V + (h + 1) * RET_DV] = o_h.astype(BF16)
    interleave(len(pending))
    m = modres_ref[0, 0]
    y = xres_ref[0] + _row(m, MOD_G1) * _dot(o_scr[...], wout_ref[0])
    for job in pending:
        job()
    y_ref[0] = y
    h2_ref[0] = _rms_mod(y, gffn_ref[0], _row(m, MOD_SC2), _row(m, MOD_SH2)).astype(h2_ref.dtype)


def _mixer(layer, x, mod_all, g_all, w_a_all, w_b_all, w_ga_all, cos, sin, w_alpha_all, b_alpha_all,
           gla_g_all, gn_g_all, gn_b_all, w_out_all, g_ffn_all, h2_dtype, cast_jobs):
    bsz, seq, _ = x.shape
    r = ATTN_ROWS
    steps_per_seq = seq // r
    n_tiles = bsz * steps_per_seq
    c = _attn_consts()
    const2 = lambda shape: pl.BlockSpec(shape, lambda j: (0,) * len(shape))
    per_layer = lambda a: _layer_spec(a.shape, layer)
    proj_tile = lambda j: jnp.minimum(j, n_tiles - 1)
    mix_tile = lambda j: jnp.maximum(j - 1, 0)
    mix_blk = lambda w: pl.BlockSpec((1, r, w), lambda j: (mix_tile(j), 0, 0))

    def chunk_spec(a, chunks, first):
        return pl.BlockSpec((1,) + a.shape[1:], lambda j: (first + jnp.minimum(j, chunks - 1), 0, 0))

    cast_in = [chunk_spec(a, chunks, li * chunks) for a, chunks, li in cast_jobs]
    cast_out = [chunk_spec(a, chunks, 0) for a, chunks, li in cast_jobs]
    cast_shapes = [jax.ShapeDtypeStruct((chunks,) + a.shape[1:], BF16) for a, chunks, li in cast_jobs]
    out = pl.pallas_call(
        functools.partial(_mixer_kernel, c["step_decay"], steps_per_seq, len(cast_jobs)),
        grid=(n_tiles + 1,),
        in_specs=[
            pl.BlockSpec((1, r, D_MODEL), lambda j: (proj_tile(j), 0, 0)),
            pl.BlockSpec((1, 1, 6, D_MODEL), lambda j: (layer, proj_tile(j) // steps_per_seq, 0, 0)),
            per_layer(g_all), per_layer(w_a_all), per_layer(w_b_all), per_layer(w_ga_all),
            mix_blk(RET_DK), mix_blk(RET_DK),
            per_layer(w_alpha_all), per_layer(b_alpha_all),
            per_layer(gla_g_all), per_layer(gn_g_all), per_layer(gn_b_all),
            const2((r, r)), const2((r, r)), const2((r, r)), const2((RET_HEADS, r, r)),
            const2((r, RET_QK)), const2((r, RET_QK)),
            const2((GLA_HEADS, 1, GLA_QK)),
            mix_blk(D_MODEL),
            pl.BlockSpec((1, 1, 6, D_MODEL), lambda j: (layer, mix_tile(j) // steps_per_seq, 0, 0)),
            per_layer(w_out_all), per_layer(g_ffn_all),
        ] + cast_in,
        out_specs=[mix_blk(D_MODEL), mix_blk(D_MODEL)] + cast_out,
        out_shape=[jax.ShapeDtypeStruct((n_tiles, r, D_MODEL), F32),
                   jax.ShapeDtypeStruct((n_tiles, r, D_MODEL), h2_dtype)] + cast_shapes,
        scratch_shapes=[pltpu.VMEM((r, PROJ_WIDTH + LANES), F32),
                        pltpu.VMEM((r, PROJ_WIDTH + LANES), F32),
                        pltpu.VMEM((r, MIX_WIDTH), BF16),
                        pltpu.VMEM((GLA_HEADS, GLA_DK, GLA_DV), F32),
                        pltpu.VMEM((GLA_QK, GLA_V), BF16),
                        pltpu.VMEM((RET_HEADS, RET_DK, RET_DV), F32)],
        compiler_params=pltpu.CompilerParams(
            dimension_semantics=("arbitrary",), vmem_limit_bytes=VMEM_LIMIT_BYTES),
        name="mixer",
    )(x.reshape(n_tiles, r, D_MODEL), mod_all, g_all, w_a_all, w_b_all, w_ga_all,
      cos.reshape(n_tiles, r, RET_DK), sin.reshape(n_tiles, r, RET_DK),
      w_alpha_all, b_alpha_all, gla_g_all, gn_g_all, gn_b_all,
      jnp.asarray(c["tri"], BF16), jnp.asarray(c["m_fwd"]), jnp.asarray(c["m_bwd"]),
      jnp.asarray(c["d_ret"]), jnp.asarray(c["qdec"]), jnp.asarray(c["kdec"]),
      jnp.asarray(c["hmask"]),
      x.reshape(n_tiles, r, D_MODEL), mod_all, w_out_all, g_ffn_all, *[a for a, _, _ in cast_jobs])
    return out[0].reshape(x.shape), out[1].reshape(x.shape), out[2:]


ROUTE_ROWS = 512
EXPERT_ROWS = 512
SEG_ALIGN = 8
LOCAL_ROWS = -(-(2 * ROUTE_ROWS + N_EXPERTS * (SEG_ALIGN - 1)) // LANES) * LANES
PIECE_BITS = (max(ROUTE_ROWS, EXPERT_ROWS) // SEG_ALIGN).bit_length()
INFO_POS, INFO_PROB = 0, 2


def _route_kernel(hin_ref, rw_ref, tril_ref, upper_ref, h_ref, info_ref, cnt_ref):
    h = hin_ref[...]
    h_hi = h.astype(BF16)
    h_ref[...] = h_hi
    h_lo = (h - h_hi.astype(F32)).astype(BF16)
    hh = _dot(h_hi, rw_ref[...])
    logits = hh[:, :LANES] + hh[:, LANES:] + _dot(h_lo, rw_ref[:, :LANES])
    lane = lax.broadcasted_iota(jnp.int32, logits.shape, 1)
    neg = jnp.float32(-jnp.inf)
    lg = jnp.where(lane < N_EXPERTS, logits, neg)
    m1 = jnp.max(lg, axis=-1, keepdims=True)
    i1 = jnp.min(jnp.where(lg == m1, lane, LANES), axis=-1, keepdims=True)
    lg2 = jnp.where(lane == i1, neg, lg)
    m2 = jnp.max(lg2, axis=-1, keepdims=True)
    i2 = jnp.min(jnp.where(lg2 == m2, lane, LANES), axis=-1, keepdims=True)
    e2 = jnp.exp(m2 - m1)
    p1 = 1.0 / (1.0 + e2)
    p2 = e2 * p1
    sel1 = lane == i1
    sel2 = lane == i2
    onehot = jnp.where(sel1, 1.0, 0.0) + jnp.where(sel2, 1.0, 0.0)
    incl = _dot(tril_ref[...], onehot.astype(BF16))
    counts = incl[ROUTE_ROWS - 1:ROUTE_ROWS]
    seg_units = jnp.floor((counts + (SEG_ALIGN - 1)) * (1.0 / SEG_ALIGN))
    seg_units8 = jnp.broadcast_to(seg_units, (8, LANES)).astype(BF16)
    seg_off = _dot(seg_units8, upper_ref[...])[0:1] * SEG_ALIGN
    pos = incl - onehot + seg_off
    pos1 = jnp.sum(jnp.where(sel1, pos, 0.0), axis=-1, keepdims=True)
    pos2 = jnp.sum(jnp.where(sel2, pos, 0.0), axis=-1, keepdims=True)
    cnt_ref[...] = jnp.broadcast_to(counts, (8, LANES))
    rec = jnp.zeros(logits.shape, F32)
    for lane_id, val in ((INFO_POS, pos1), (INFO_POS + 1, pos2), (INFO_PROB, p1), (INFO_PROB + 1, p2)):
        rec = jnp.where(lane == lane_id, val, rec)
    info_ref[...] = rec


def _route(h_f32, router_w):
    n = h_f32.shape[0]
    tm = ROUTE_ROWS
    w_hi = router_w.astype(BF16)
    w_lo = (router_w - w_hi.astype(F32)).astype(BF16)
    pad = ((0, 0), (0, LANES - N_EXPERTS))
    rw = jnp.concatenate([jnp.pad(w_hi, pad), jnp.pad(w_lo, pad)], axis=1)
    tril = jnp.asarray(np.tril(np.ones((tm, tm), np.float32)), BF16)
    upper = jnp.asarray(np.triu(np.ones((LANES, LANES), np.float32), 1), BF16)
    return pl.pallas_call(
        _route_kernel,
        grid=(n // tm,),
        in_specs=[pl.BlockSpec((tm, D_MODEL), lambda i: (i, 0)),
                  pl.BlockSpec((D_MODEL, 2 * LANES), lambda i: (0, 0)),
                  pl.BlockSpec((tm, tm), lambda i: (0, 0)),
                  pl.BlockSpec((LANES, LANES), lambda i: (0, 0))],
        out_specs=[pl.BlockSpec((tm, D_MODEL), lambda i: (i, 0)),
                   pl.BlockSpec((tm, LANES), lambda i: (i, 0)),
                   pl.BlockSpec((8, LANES), lambda i: (i, 0))],
        out_shape=[jax.ShapeDtypeStruct((n, D_MODEL), BF16),
                   jax.ShapeDtypeStruct((n, LANES), F32),
                   jax.ShapeDtypeStruct((n // tm * 8, LANES), F32)],
        compiler_params=pltpu.CompilerParams(
            dimension_semantics=("arbitrary",), vmem_limit_bytes=VMEM_LIMIT_BYTES),
        name="route",
    )(h_f32, rw, tril, upper)


def _rows(ref, row, n_rows):
    return ref.at[pl.ds(pl.multiple_of(row, SEG_ALIGN), n_rows)]


def _for_each_piece(n_units, fn):
    for bit in reversed(range(PIECE_BITS)):
        covered = (n_units >> (bit + 1)) << (bit + 1)

        @pl.when(((n_units >> bit) & 1) == 1)
        def _():
            fn(covered * SEG_ALIGN, SEG_ALIGN << bit)


def _for_each_segment_piece(seg_ref, n_seg, tile, fn):
    for e in range(N_EXPERTS):
        s = tile * N_EXPERTS + e
        sorted_row = seg_ref[s]
        local_row = seg_ref[n_seg + s]
        _for_each_piece(seg_ref[2 * n_seg + s],
                        lambda first, n_rows: fn(local_row + first, sorted_row + first, n_rows))


def _selection(info, coeff1, coeff2):
    lane = lax.broadcasted_iota(jnp.int32, (info.shape[0], LOCAL_ROWS), 1)
    pos1 = info[:, INFO_POS:INFO_POS + 1].astype(jnp.int32)
    pos2 = info[:, INFO_POS + 1:INFO_POS + 2].astype(jnp.int32)
    return jnp.where(lane == pos1, coeff1, jnp.where(lane == pos2, coeff2, 0.0)).astype(BF16)


def _dispatch_kernel(n_seg, seg_ref, pad_ref, h_ref, info_ref, xs_ref, sorted_scr, zero_scr, sems, pad_sem):
    i = pl.program_id(0)
    n_steps = pl.num_programs(0)
    slot = i % 2

    def move(tile, buf_slot, op):
        def piece(local_row, sorted_row, n_rows):
            cp = pltpu.make_async_copy(_rows(sorted_scr.at[buf_slot], local_row, n_rows),
                                       _rows(xs_ref, sorted_row, n_rows), sems.at[buf_slot])
            op(cp)
        _for_each_segment_piece(seg_ref, n_seg, tile, piece)

    @pl.when(i == 0)
    def _():
        zero_scr[...] = jnp.zeros_like(zero_scr)
        for op in (lambda cp: cp.start(), lambda cp: cp.wait()):
            for e in range(N_EXPERTS):
                lo = pad_ref[e]
                _for_each_piece(pad_ref[N_EXPERTS + e],
                                lambda first, n_rows: op(pltpu.make_async_copy(
                                    zero_scr.at[pl.ds(0, n_rows)], _rows(xs_ref, lo + first, n_rows),
                                    pad_sem)))

        def tile_copy(j):
            return pltpu.make_async_copy(zero_scr, _rows(xs_ref, j * EXPERT_ROWS, EXPERT_ROWS), pad_sem)

        def fill_tile(j, carry):
            tile_copy(j).start()
            return carry

        def drain_tile(j, carry):
            tile_copy(j).wait()
            return carry

        n_tiles = xs_ref.shape[0] // EXPERT_ROWS
        lax.fori_loop(pad_ref[2 * N_EXPERTS], n_tiles, fill_tile, 0)
        lax.fori_loop(pad_ref[2 * N_EXPERTS], n_tiles, drain_tile, 0)

    @pl.when(i >= 2)
    def _():
        move(i - 2, slot, lambda cp: cp.wait())

    sel = _selection(info_ref[...], 1.0, 1.0)
    sorted_scr[slot] = _dot_tn(sel, h_ref[...])
    move(i, slot, lambda cp: cp.start())

    @pl.when(i == n_steps - 1)
    def _():
        @pl.when(i >= 1)
        def _():
            move(i - 1, 1 - slot, lambda cp: cp.wait())
        move(i, slot, lambda cp: cp.wait())


def _dispatch(h2, info, segs, pads, n_sorted):
    n = h2.shape[0]
    tm = ROUTE_ROWS
    n_seg = n // tm * N_EXPERTS
    return pl.pallas_call(
        functools.partial(_dispatch_kernel, n_seg),
        grid_spec=pltpu.PrefetchScalarGridSpec(
            num_scalar_prefetch=2,
            grid=(n // tm,),
            in_specs=[pl.BlockSpec((tm, D_MODEL), lambda i, s, p: (i, 0)),
                      pl.BlockSpec((tm, LANES), lambda i, s, p: (i, 0))],
            out_specs=pl.BlockSpec(memory_space=pl.ANY),
            scratch_shapes=[pltpu.VMEM((2, LOCAL_ROWS, D_MODEL), F32),
                            pltpu.VMEM((EXPERT_ROWS, D_MODEL), F32),
                            pltpu.SemaphoreType.DMA((2,)), pltpu.SemaphoreType.DMA(())]),
        out_shape=jax.ShapeDtypeStruct((n_sorted, D_MODEL), F32),
        compiler_params=pltpu.CompilerParams(
            dimension_semantics=("arbitrary",), vmem_limit_bytes=VMEM_LIMIT_BYTES),
        name="dispatch",
    )(segs, pads, h2, info)


def _expert_kernel(tile_blk, tile_e, n_act, x_ref, w1_ref, w3_ref, w2_ref, y_ref):
    active = pl.program_id(0) < n_act[0]

    @pl.when(active)
    def _():
        h = x_ref[...].astype(BF16)
        a = _dot(h, w1_ref[0])
        u = (a * _sigmoid(a) * _dot(h, w3_ref[0])).astype(BF16)
        y_ref[...] = _dot(u, w2_ref[0])

    @pl.when(jnp.logical_not(active))
    def _():
        y_ref[...] = jnp.zeros_like(y_ref)


def _experts(xs, tile_blk, tile_e, n_act, w1, w3, w2):
    n_sorted = xs.shape[0]
    tm = EXPERT_ROWS
    f = w1.shape[-1]
    return pl.pallas_call(
        _expert_kernel,
        grid_spec=pltpu.PrefetchScalarGridSpec(
            num_scalar_prefetch=3,
            grid=(n_sorted // tm,),
            in_specs=[pl.BlockSpec((tm, D_MODEL), lambda i, b, e, n: (b[i], 0)),
                      pl.BlockSpec((1, D_MODEL, f), lambda i, b, e, n: (e[i], 0, 0)),
                      pl.BlockSpec((1, D_MODEL, f), lambda i, b, e, n: (e[i], 0, 0)),
                      pl.BlockSpec((1, f, D_MODEL), lambda i, b, e, n: (e[i], 0, 0))],
            out_specs=pl.BlockSpec((tm, D_MODEL), lambda i, b, e, n: (i, 0))),
        out_shape=jax.ShapeDtypeStruct((n_sorted, D_MODEL), F32),
        compiler_params=pltpu.CompilerParams(
            dimension_semantics=("arbitrary",), vmem_limit_bytes=VMEM_LIMIT_BYTES),
        name="experts",
    )(tile_blk, tile_e, n_act, xs, w1, w3, w2)


def _combine_kernel(final_norm, n_seg, seg_ref, x_ref, mod_ref, info_ref, fg_ref, ys_ref, out_ref,
                    ybuf, sems):
    i = pl.program_id(0)
    n_steps = pl.num_programs(0)
    slot = i % 2

    def move(tile, buf_slot, op):
        def piece(local_row, sorted_row, n_rows):
            cp = pltpu.make_async_copy(_rows(ys_ref, sorted_row, n_rows),
                                       _rows(ybuf.at[buf_slot], local_row, n_rows), sems.at[buf_slot])
            op(cp)
        _for_each_segment_piece(seg_ref, n_seg, tile, piece)

    @pl.when(i == 0)
    def _():
        ybuf[...] = jnp.zeros_like(ybuf)
        move(0, 0, lambda cp: cp.start())

    @pl.when(i + 1 < n_steps)
    def _():
        move(i + 1, 1 - slot, lambda cp: cp.start())

    move(i, slot, lambda cp: cp.wait())
    info = info_ref[...]
    sel = _selection(info, info[:, INFO_PROB:INFO_PROB + 1], info[:, INFO_PROB + 1:INFO_PROB + 2])
    y = _dot(sel, ybuf[slot].astype(BF16))
    out = x_ref[...] + _row(mod_ref[0, 0], MOD_G2) * y
    if final_norm:
        ms = jnp.mean(out * out, axis=-1, keepdims=True)
        out = out * lax.rsqrt(ms + EPS) * fg_ref[...]
    out_ref[...] = out


def _combine(layer, x2, mod_all, info, ys, segs, final_g, final_norm, rows_per_batch):
    n = x2.shape[0]
    tm = ROUTE_ROWS
    per_b = rows_per_batch // tm
    n_seg = n // tm * N_EXPERTS
    return pl.pallas_call(
        functools.partial(_combine_kernel, final_norm, n_seg),
        grid_spec=pltpu.PrefetchScalarGridSpec(
            num_scalar_prefetch=1,
            grid=(n // tm,),
            in_specs=[pl.BlockSpec((tm, D_MODEL), lambda i, s: (i, 0)),
                      pl.BlockSpec((1, 1, 6, D_MODEL), lambda i, s: (layer, i // per_b, 0, 0)),
                      pl.BlockSpec((tm, LANES), lambda i, s: (i, 0)),
                      pl.BlockSpec((1, D_MODEL), lambda i, s: (0, 0)),
                      pl.BlockSpec(memory_space=pl.ANY)],
            out_specs=pl.BlockSpec((tm, D_MODEL), lambda i, s: (i, 0)),
            scratch_shapes=[pltpu.VMEM((2, LOCAL_ROWS, D_MODEL), F32),
                            pltpu.SemaphoreType.DMA((2,))]),
        out_shape=jax.ShapeDtypeStruct(x2.shape, F32),
        compiler_params=pltpu.CompilerParams(
            dimension_semantics=("arbitrary",), vmem_limit_bytes=VMEM_LIMIT_BYTES),
        name="combine",
    )(segs, x2, mod_all, info, final_g, ys)


def _moe(layer, x, h_f32, mod_all, router_w, w1, w3, w2, final_g, final_norm):
    bsz, seq, _ = x.shape
    n = bsz * seq
    tm = EXPERT_ROWS
    n_tok_tiles = n // ROUTE_ROWS
    x2 = x.reshape(n, D_MODEL)
    h2, info, cnt = _route(h_f32.reshape(n, D_MODEL), router_w)
    counts = cnt.reshape(n_tok_tiles, 8, LANES)[:, 0, :N_EXPERTS].astype(jnp.int32)
    seg_rows = (counts + (SEG_ALIGN - 1)) // SEG_ALIGN * SEG_ALIGN
    group_rows = jnp.sum(seg_rows, axis=0)
    tiles = (group_rows + (tm - 1)) // tm
    tile_end = jnp.cumsum(tiles)
    group_off = (tile_end - tiles) * tm
    seg_sorted = group_off[None, :] + jnp.cumsum(seg_rows, axis=0) - seg_rows
    seg_local = jnp.cumsum(seg_rows, axis=1) - seg_rows
    segs = jnp.concatenate([seg_sorted.reshape(-1), seg_local.reshape(-1),
                            (seg_rows // SEG_ALIGN).reshape(-1)]).astype(jnp.int32)
    max_rows = 2 * n + n_tok_tiles * N_EXPERTS * (SEG_ALIGN - 1)
    n_tiles = -(-max_rows // tm) + N_EXPERTS
    n_act = tile_end[-1]
    t = jnp.minimum(jnp.arange(n_tiles, dtype=jnp.int32), n_act - 1)
    tile_e = jnp.sum(t[:, None] >= tile_end[None, :], axis=1).astype(jnp.int32)
    pad_lo = group_off + group_rows
    pads = jnp.concatenate([pad_lo, (tile_end * tm - pad_lo) // SEG_ALIGN,
                            n_act[None]]).astype(jnp.int32)
    xs = _dispatch(h2, info, segs, pads, n_tiles * tm)
    ys = _experts(xs, t, tile_e, n_act.reshape(1).astype(jnp.int32), w1, w3, w2)
    out = _combine(layer, x2, mod_all, info, ys, segs, final_g, final_norm, seq)
    return out.reshape(x.shape)


def _ffn_kernel(final_norm, n_groups, x_ref, h_ref, mod_ref, w1_ref, w3_ref, w2_ref, fg_ref, y_ref,
                acc_scr):
    e = pl.program_id(2)

    def group_product():
        h = h_ref[0]
        a = _dot(h, w1_ref[...])
        u = (a * _sigmoid(a) * _dot(h, w3_ref[...])).astype(BF16)
        return _dot(u, w2_ref[...])

    assert n_groups == 2

    @pl.when(e == 0)
    def _():
        acc_scr[...] = group_product()

    @pl.when(e == 1)
    def _():
        out = x_ref[0] + _row(mod_ref[0, 0], MOD_G2) * (acc_scr[...] + group_product())
        if final_norm:
            ms = jnp.mean(out * out, axis=-1, keepdims=True)
            out = out * lax.rsqrt(ms + EPS) * fg_ref[...]
        y_ref[0] = out


def _ffn(layer, x, h2, mod_all, w1, w3, w2, final_g, final_norm):
    bsz, seq, _ = x.shape
    n_groups = 2
    f = w1.shape[-1] // n_groups
    tm = 1024
    row = pl.BlockSpec((1, tm, D_MODEL), lambda b, i, e: (b, i, 0))
    return pl.pallas_call(
        functools.partial(_ffn_kernel, final_norm, n_groups),
        grid=(bsz, seq // tm, n_groups),
        in_specs=[row, row,
                  pl.BlockSpec((1, 1, 6, D_MODEL), lambda b, i, e: (layer, b, 0, 0)),
                  pl.BlockSpec((D_MODEL, f), lambda b, i, e: (0, e)),
                  pl.BlockSpec((D_MODEL, f), lambda b, i, e: (0, e)),
                  pl.BlockSpec((f, D_MODEL), lambda b, i, e: (e, 0)),
                  pl.BlockSpec((1, D_MODEL), lambda b, i, e: (0, 0))],
        out_specs=row,
        out_shape=jax.ShapeDtypeStruct(x.shape, F32),
        scratch_shapes=[pltpu.VMEM((tm, D_MODEL), F32)],
        compiler_params=pltpu.CompilerParams(
            dimension_semantics=("arbitrary", "arbitrary", "arbitrary"),
            vmem_limit_bytes=VMEM_LIMIT_BYTES),
        name="ffn",
    )(x, h2, mod_all, w1, w3, w2, final_g)


def kernel(x, c, positions, ada_w, ada_b, norm_mix_g, norm_ffn_g, w_in, gla_w_alpha, gla_b_alpha,
           gla_norm_g, ret_gn_g, ret_gn_b, w_out, ffn_w1, ffn_w3, ffn_w2, router_w, moe_w1,
           moe_w3, moe_w2, final_g):
    bsz = x.shape[0]
    mod_all = _ada_mod(c, ada_w, ada_b).reshape(DEPTH, bsz, 6, D_MODEL)
    cos, sin = _rope_tables(positions)
    w_a_all, w_b_all, w_ga_all = _prep_w_in(w_in)
    w_alpha_all = jnp.pad(gla_w_alpha, ((0, 0), (0, LANES - GLA_RANK), (0, 0))).astype(BF16)
    vec = lambda a: a.reshape(a.shape[0], 1, a.shape[1])
    b_alpha_all, gla_g_all, gn_g_all, gn_b_all = map(vec, (gla_b_alpha, gla_norm_g, ret_gn_g, ret_gn_b))
    g_mix_all, g_ffn_all = vec(norm_mix_g), vec(norm_ffn_g)
    w_out_all = w_out.astype(BF16)
    fg = final_g.reshape(1, D_MODEL)
    d_ff = ffn_w1.shape[-1]
    d_fe = moe_w1.shape[-1]
    n_dense, n_moe = ffn_w1.shape[0], moe_w1.shape[0]
    dense_src = (ffn_w1.reshape(n_dense * FFN_CHUNKS, D_MODEL // FFN_CHUNKS, d_ff),
                 ffn_w3.reshape(n_dense * FFN_CHUNKS, D_MODEL // FFN_CHUNKS, d_ff),
                 ffn_w2.reshape(n_dense * FFN_CHUNKS, d_ff // FFN_CHUNKS, D_MODEL))
    moe_chunks = N_EXPERTS * MOE_CHUNKS_PER_EXPERT
    moe_src = (moe_w1.reshape(n_moe * moe_chunks, D_MODEL // MOE_CHUNKS_PER_EXPERT, d_fe),
               moe_w3.reshape(n_moe * moe_chunks, D_MODEL // MOE_CHUNKS_PER_EXPERT, d_fe),
               moe_w2.reshape(n_moe * moe_chunks, d_fe // MOE_CHUNKS_PER_EXPERT, D_MODEL))
    for layer in range(DEPTH):
        li = layer // 2
        dense = layer % 2 == 0
        jobs = [(a, FFN_CHUNKS if dense else moe_chunks, li) for a in (dense_src if dense else moe_src)]
        x, h2, (w1, w3, w2) = _mixer(layer, x, mod_all, g_mix_all, w_a_all, w_b_all, w_ga_all, cos, sin,
                                     w_alpha_all, b_alpha_all, gla_g_all, gn_g_all, gn_b_all, w_out_all,
                                     g_ffn_all, BF16 if dense else F32, jobs)
        last = layer == DEPTH - 1
        if dense:
            x = _ffn(layer, x, h2, mod_all, w1.reshape(D_MODEL, d_ff), w3.reshape(D_MODEL, d_ff),
                     w2.reshape(d_ff, D_MODEL), fg, last)
        else:
            x = _moe(layer, x, h2, mod_all, router_w[li], w1.reshape(N_EXPERTS, D_MODEL, d_fe),
                     w3.reshape(N_EXPERTS, D_MODEL, d_fe), w2.reshape(N_EXPERTS, d_fe, D_MODEL), fg, last)
    return x
```

```python
import functools

import numpy as np
import jax
import jax.numpy as jnp
from jax import lax
from jax.experimental import pallas as pl
from jax.experimental.pallas import tpu as pltpu

F32 = jnp.float32
BF16 = jnp.bfloat16

D_MODEL = 1024
DEPTH = 4
CHUNK = 64
GLA_HEADS = 4
GLA_DK = 64
GLA_DV = 128
GLA_RANK = 16
GLA_TAU = 16.0
RET_HEADS = 4
RET_DK = 128
RET_DV = 128
ROPE_BASE = 10000.0
N_EXPERTS = 8
EPS = 1e-6
GLA_QK = GLA_HEADS * GLA_DK
GLA_V = GLA_HEADS * GLA_DV
RET_QK = RET_HEADS * RET_DK
RET_V = RET_HEADS * RET_DV
MIX_WIDTH = GLA_V + RET_V

LANES = 128
VMEM_LIMIT_BYTES = 56 * 1024 * 1024

PROJ_WIDTH = 2 * GLA_QK + 2 * GLA_V + 2 * RET_QK + 2 * RET_V
OFF_GQ = 0
OFF_GK = OFF_GQ + GLA_QK
OFF_GV = OFF_GK + GLA_QK
OFF_GR = OFF_GV + GLA_V
OFF_RQ = OFF_GR + GLA_V
OFF_RK = OFF_RQ + RET_QK
OFF_RV = OFF_RK + RET_QK
OFF_RG = OFF_RV + RET_V

ATTN_ROWS = 256
ATTN_CHUNKS = ATTN_ROWS // CHUNK

MOD_SH1, MOD_SC1, MOD_G1, MOD_SH2, MOD_SC2, MOD_G2 = range(6)


def _sigmoid(x):
    return 1.0 / (1.0 + jnp.exp(-x))


def _rms_mod(x, g, sc, sh):
    ms = jnp.mean(x * x, axis=-1, keepdims=True)
    return (x * lax.rsqrt(ms + EPS)) * g * (1.0 + sc) + sh


def _dot(a, b):
    return jnp.dot(a, b, preferred_element_type=F32)


def _dot_nt(a, b):
    return lax.dot_general(a, b, (((1,), (1,)), ((), ())), preferred_element_type=F32)


def _dot_tn(a, b):
    return lax.dot_general(a, b, (((0,), (0,)), ((), ())), preferred_element_type=F32)


def _row(m, r):
    return m[r:r + 1]


def _layer_spec(shape, layer):
    zeros = (0,) * (len(shape) - 1)
    return pl.BlockSpec((1,) + tuple(shape[1:]), lambda *_: (layer,) + zeros)


def _ada_kernel(c_ref, w_ref, b_ref, o_ref):
    c = c_ref[...]
    cond = c * _sigmoid(c)
    c_hi = cond.astype(BF16)
    c_lo = (cond - c_hi.astype(F32)).astype(BF16)
    w = w_ref[0]
    w_hi = w.astype(BF16)
    w_lo = (w - w_hi.astype(F32)).astype(BF16)
    o_ref[0] = _dot(c_hi, w_hi) + _dot(c_hi, w_lo) + _dot(c_lo, w_hi) + b_ref[0]


def _ada_mod(c, ada_w, ada_b):
    bsz = c.shape[0]
    tn = 1024
    n_out = ada_w.shape[-1]
    return pl.pallas_call(
        _ada_kernel,
        grid=(DEPTH, n_out // tn),
        in_specs=[
            pl.BlockSpec((bsz, D_MODEL), lambda l, j: (0, 0)),
            pl.BlockSpec((1, D_MODEL, tn), lambda l, j: (l, 0, j)),
            pl.BlockSpec((1, 1, tn), lambda l, j: (l, 0, j)),
        ],
        out_specs=pl.BlockSpec((1, bsz, tn), lambda l, j: (l, 0, j)),
        out_shape=jax.ShapeDtypeStruct((DEPTH, bsz, n_out), F32),
        name="ada_mod",
    )(c, ada_w, ada_b.reshape(DEPTH, 1, n_out))


def _rope_kernel(pos_ref, invf_ref, sign_ref, cos_ref, sin_ref):
    ang = pos_ref[...] * invf_ref[...]
    cos_ref[...] = jnp.cos(ang)
    sin_ref[...] = jnp.sin(ang) * sign_ref[...]


def _rope_tables(positions):
    n = positions.size
    half = RET_DK // 2
    inv_freq = ROPE_BASE ** (-jnp.arange(half, dtype=F32) / half)
    invf = jnp.concatenate([inv_freq, inv_freq]).reshape(1, RET_DK)
    sign = jnp.concatenate([-jnp.ones((half,), F32), jnp.ones((half,), F32)]).reshape(1, RET_DK)
    pos = positions.astype(F32).reshape(n, 1)
    tm = 2048
    row = pl.BlockSpec((1, RET_DK), lambda i: (0, 0))
    out = pl.BlockSpec((tm, RET_DK), lambda i: (i, 0))
    return pl.pallas_call(
        _rope_kernel,
        grid=(n // tm,),
        in_specs=[pl.BlockSpec((tm, 1), lambda i: (i, 0)), row, row],
        out_specs=[out, out],
        out_shape=[jax.ShapeDtypeStruct((n, RET_DK), F32)] * 2,
        name="rope_tables",
    )(pos, invf, sign)


GA_LO = 2 * GLA_QK + 2 * GLA_V
OFF_GA = PROJ_WIDTH
W_AFTER = GA_LO + LANES
PROJ_BLOCK = 256
TAIL_JOBS = 2
FFN_CHUNKS = 16
MOE_CHUNKS_PER_EXPERT = 8


def _attn_consts():
    r = ATTN_ROWS
    t = np.arange(r)
    same = (t[:, None] // CHUNK) == (t[None, :] // CHUNK)
    causal = t[:, None] >= t[None, :]
    tri = (same & causal).astype(np.float32)
    m_fwd = tri
    m_bwd = (same & ~causal).astype(np.float32)
    gam = 1.0 - 2.0 ** (-5.0 - np.arange(RET_HEADS, dtype=np.float64))
    lg = np.log(gam)
    dist = (t[:, None] - t[None, :]).astype(np.float64)
    d_ret = np.where(causal[None], np.exp(lg[:, None, None] * dist[None]),
                     np.where(same[None], np.exp(-lg[:, None, None] * dist[None]), 0.0))
    qdec = np.repeat(np.exp(lg[None, :] * (t[:, None] + 1.0)), RET_DK, axis=1)
    kdec = np.repeat(np.exp(lg[None, :] * (r - 1.0 - t[:, None])), RET_DK, axis=1)
    step_decay = [float(np.exp(lg[h] * r)) for h in range(RET_HEADS)]
    hmask = np.zeros((GLA_HEADS, 1, GLA_QK), np.float32)
    for h in range(GLA_HEADS):
        hmask[h, 0, h * GLA_DK:(h + 1) * GLA_DK] = 1.0
    return dict(tri=tri, m_fwd=m_fwd, m_bwd=m_bwd, d_ret=d_ret.astype(np.float32),
                qdec=qdec.astype(np.float32), kdec=kdec.astype(np.float32),
                step_decay=step_decay, hmask=hmask)


def _mixer_kernel(step_decay, steps_per_seq, n_cast, x_ref, mod_ref, g_ref, w_ref, *rest):
    proj_a, proj_b, o_scr, st_g, st_g_bf, st_r = rest[-6:]
    n_mix_in = len(rest) - 6 - 2 * n_cast - 2
    mix_in = rest[:n_mix_in]
    cast_src = rest[n_mix_in:n_mix_in + n_cast]
    y_ref, h2_ref = rest[n_mix_in + n_cast:n_mix_in + n_cast + 2]
    cast_dst = rest[n_mix_in + n_cast + 2:n_mix_in + 2 * n_cast + 2]
    j = pl.program_id(0)

    @pl.when(j == 0)
    def _():
        proj_b[...] = jnp.zeros_like(proj_b)

    @pl.when((j - 1) % steps_per_seq == 0)
    def _():
        st_g[...] = jnp.zeros_like(st_g)
        st_g_bf[...] = jnp.zeros_like(st_g_bf)
        st_r[...] = jnp.zeros_like(st_r)

    def step(proj_w, proj_r):
        m = mod_ref[0, 0]
        h = _rms_mod(x_ref[0], g_ref[0], _row(m, MOD_SC1), _row(m, MOD_SH1)).astype(BF16)

        def block(w0, out0, width):
            def emit():
                proj_w[:, out0:out0 + width] = _dot(h, w_ref[0, :, w0:w0 + width])
            return emit

        for src, dst in zip(cast_src, cast_dst):
            dst[...] = src[...].astype(BF16)

        pending = ([block(c0, c0, PROJ_BLOCK) for c0 in range(0, GA_LO, PROJ_BLOCK)]
                   + [block(GA_LO, OFF_GA, LANES)]
                   + [block(W_AFTER + c0, GA_LO + c0, PROJ_BLOCK)
                      for c0 in range(0, PROJ_WIDTH - GA_LO, PROJ_BLOCK)])
        _attn_body(step_decay, pending, proj_r, *mix_in, y_ref, h2_ref, o_scr, st_g, st_g_bf, st_r)

    @pl.when(j % 2 == 0)
    def _():
        step(proj_a, proj_b)

    @pl.when(j % 2 == 1)
    def _():
        step(proj_b, proj_a)


def _attn_body(step_decay, pending, proj_ref, cos_ref, sin_ref, walpha_ref, balpha_ref,
               glag_ref, gng_ref, gnb_ref, tri_ref, mfwd_ref, mbwd_ref, dret_ref,
               qdec_ref, kdec_ref, hmask_ref, xres_ref, modres_ref, wout_ref, gffn_ref,
               y_ref, h2_ref, o_scr, st_g, st_g_bf, st_r):
    r = ATTN_ROWS
    pending = list(pending)

    def interleave(n=1):
        for _ in range(min(n, len(pending) - TAIL_JOBS)):
            pending.pop(0)()

    z = _dot(proj_ref[:, OFF_GA:].astype(BF16), walpha_ref[0]) + balpha_ref[0]
    interleave(2)
    log_a = (jnp.minimum(z, 0.0) - jnp.log(1.0 + jnp.exp(-jnp.abs(z)))) * (1.0 / GLA_TAU)
    hi = log_a.astype(BF16)
    r1 = log_a - hi.astype(F32)
    mid = r1.astype(BF16)
    lo = (r1 - mid.astype(F32)).astype(BF16)
    tri = tri_ref[...]
    b = _dot(tri, hi) + _dot(tri, mid) + _dot(tri, lo)
    interleave(2)
    b3 = b.reshape(ATTN_CHUNKS, CHUNK, GLA_QK)
    b_last = b3[:, CHUNK - 1:CHUNK, :]
    k_upd_scale = jnp.exp(b_last - b3).reshape(r, GLA_QK)
    eb = jnp.exp(b)
    enb = jnp.exp(-b)
    q = proj_ref[:,OFF_GQ:OFF_GQ + GLA_QK] * (GLA_DK ** -0.5)
    k = proj_ref[:,OFF_GK:OFF_GK + GLA_QK]
    v_bf = proj_ref[:,OFF_GV:OFF_GV + GLA_V].astype(BF16)
    q_f = q * eb
    q_b = q * enb
    k_f = (k * enb).astype(BF16)
    k_b = (k * eb).astype(BF16)
    k_u = (k * k_upd_scale).astype(BF16)
    q_f_bf = q_f.astype(BF16)

    b_t = b.T
    heads_per_group = LANES // GLA_DK
    inter = []
    for g in range(ATTN_CHUNKS):
        rows = slice(g * CHUNK, (g + 1) * CHUNK)
        inter.append(_dot(q_f_bf[rows], st_g_bf[...]))
        decay = jnp.exp(b_t[:, (g + 1) * CHUNK - 1:(g + 1) * CHUNK])
        for grp in range(GLA_QK // LANES):
            v_lanes = slice(grp * heads_per_group * GLA_DV, (grp + 1) * heads_per_group * GLA_DV)
            upd = _dot_tn(k_u[rows, grp * LANES:(grp + 1) * LANES], v_bf[rows, v_lanes])
            for i in range(heads_per_group):
                h = grp * heads_per_group + i
                keys = slice(h * GLA_DK, (h + 1) * GLA_DK)
                st = (st_g[h] * decay[keys]
                      + upd[i * GLA_DK:(i + 1) * GLA_DK, i * GLA_DV:(i + 1) * GLA_DV])
                st_g[h] = st
                st_g_bf[keys, h * GLA_DV:(h + 1) * GLA_DV] = st.astype(BF16)
        interleave()
    o_inter = jnp.concatenate(inter, axis=0)

    m_fwd = mfwd_ref[...] > 0.5
    m_bwd = mbwd_ref[...] > 0.5
    for h in range(GLA_HEADS):
        hm = hmask_ref[h]
        s_f = _dot_nt((q_f * hm).astype(BF16), k_f)
        s_b = _dot_nt((q_b * hm).astype(BF16), k_b)
        sc = jnp.where(m_fwd, s_f, jnp.where(m_bwd, s_b, 0.0)).astype(BF16)
        lanes = slice(h * GLA_DV, (h + 1) * GLA_DV)
        o_h = _dot(sc, v_bf[:, lanes]) + o_inter[:, lanes]
        interleave()
        o_h = o_h * lax.rsqrt(jnp.mean(o_h * o_h, axis=-1, keepdims=True) + EPS)
        gate = proj_ref[:,OFF_GR + h * GLA_DV:OFF_GR + (h + 1) * GLA_DV]
        o_h = o_h * glag_ref[0, :, lanes] * (gate * _sigmoid(gate))
        o_scr[:, lanes] = o_h.astype(BF16)

    cos = cos_ref[0]
    sin = sin_ref[0]
    for h in range(RET_HEADS):
        lanes = slice(h * RET_DK, (h + 1) * RET_DK)
        qh = proj_ref[:,OFF_RQ + h * RET_DK:OFF_RQ + (h + 1) * RET_DK]
        kh = proj_ref[:,OFF_RK + h * RET_DK:OFF_RK + (h + 1) * RET_DK]
        vh = proj_ref[:,OFF_RV + h * RET_DV:OFF_RV + (h + 1) * RET_DV].astype(BF16)
        qh = (qh * cos + pltpu.roll(qh, RET_DK // 2, axis=1) * sin) * (RET_DK ** -0.5)
        kh = kh * cos + pltpu.roll(kh, RET_DK // 2, axis=1) * sin
        s = _dot_nt(qh.astype(BF16), kh.astype(BF16)) * dret_ref[h]
        st = st_r[h]
        o_h = _dot(s.astype(BF16), vh) + _dot((qh * qdec_ref[:, lanes]).astype(BF16), st.astype(BF16))
        st_r[h] = st * step_decay[h] + _dot_tn((kh * kdec_ref[:, lanes]).astype(BF16), vh)
        interleave()
        mu =jnp.mean(o_h, axis=-1, keepdims=True)
        d = o_h - mu
        var = jnp.mean(d * d, axis=-1, keepdims=True)
        o_h = d * lax.rsqrt(var + EPS) * gng_ref[0, :, lanes] + gnb_ref[0, :, lanes]
        gate = proj_ref[:,OFF_RG + h * RET_DV:OFF_RG + (h + 1) * RET_DV]
        o_h = o_h * (gate * _sigmoid(gate))
        o_scr[:, GLA_V + h * RET_DV:GLA_V + (h + 1) * RET_DV] = o_h.astype(BF16)
    interleave(len(pending))
    m = modres_ref[0, 0]
    y = xres_ref[0] + _row(m, MOD_G1) * _dot(o_scr[...], wout_ref[0])
    for job in pending:
        job()
    y_ref[0] = y
    h2_ref[0] = _rms_mod(y, gffn_ref[0], _row(m, MOD_SC2), _row(m, MOD_SH2)).astype(h2_ref.dtype)


def _mixer(layer, x, mod_all, g_all, w_proj_all, cos, sin, w_alpha_all, b_alpha_all,
           gla_g_all, gn_g_all, gn_b_all, w_out_all, g_ffn_all, h2_dtype, cast_jobs):
    bsz, seq, _ = x.shape
    r = ATTN_ROWS
    steps_per_seq = seq // r
    n_tiles = bsz * steps_per_seq
    c = _attn_consts()
    const2 = lambda shape: pl.BlockSpec(shape, lambda j: (0,) * len(shape))
    per_layer = lambda a: _layer_spec(a.shape, layer)
    proj_tile = lambda j: jnp.minimum(j, n_tiles - 1)
    mix_tile = lambda j: jnp.maximum(j - 1, 0)
    mix_blk = lambda w: pl.BlockSpec((1, r, w), lambda j: (mix_tile(j), 0, 0))

    def chunk_spec(a, chunks, first):
        return pl.BlockSpec((1,) + a.shape[1:], lambda j: (first + jnp.minimum(j, chunks - 1), 0, 0))

    cast_in = [chunk_spec(a, chunks, li * chunks) for a, chunks, li in cast_jobs]
    cast_out = [chunk_spec(a, chunks, 0) for a, chunks, li in cast_jobs]
    cast_shapes = [jax.ShapeDtypeStruct((chunks,) + a.shape[1:], BF16) for a, chunks, li in cast_jobs]
    out = pl.pallas_call(
        functools.partial(_mixer_kernel, c["step_decay"], steps_per_seq, len(cast_jobs)),
        grid=(n_tiles + 1,),
        in_specs=[
            pl.BlockSpec((1, r, D_MODEL), lambda j: (proj_tile(j), 0, 0)),
            pl.BlockSpec((1, 1, 6, D_MODEL), lambda j: (layer, proj_tile(j) // steps_per_seq, 0, 0)),
            per_layer(g_all), per_layer(w_proj_all),
            mix_blk(RET_DK), mix_blk(RET_DK),
            per_layer(w_alpha_all), per_layer(b_alpha_all),
            per_layer(gla_g_all), per_layer(gn_g_all), per_layer(gn_b_all),
            const2((r, r)), const2((r, r)), const2((r, r)), const2((RET_HEADS, r, r)),
            const2((r, RET_QK)), const2((r, RET_QK)),
            const2((GLA_HEADS, 1, GLA_QK)),
            mix_blk(D_MODEL),
            pl.BlockSpec((1, 1, 6, D_MODEL), lambda j: (layer, mix_tile(j) // steps_per_seq, 0, 0)),
            per_layer(w_out_all), per_layer(g_ffn_all),
        ] + cast_in,
        out_specs=[mix_blk(D_MODEL), mix_blk(D_MODEL)] + cast_out,
        out_shape=[jax.ShapeDtypeStruct((n_tiles, r, D_MODEL), F32),
                   jax.ShapeDtypeStruct((n_tiles, r, D_MODEL), h2_dtype)] + cast_shapes,
        scratch_shapes=[pltpu.VMEM((r, PROJ_WIDTH + LANES), F32),
                        pltpu.VMEM((r, PROJ_WIDTH + LANES), F32),
                        pltpu.VMEM((r, MIX_WIDTH), BF16),
                        pltpu.VMEM((GLA_HEADS, GLA_DK, GLA_DV), F32),
                        pltpu.VMEM((GLA_QK, GLA_V), BF16),
                        pltpu.VMEM((RET_HEADS, RET_DK, RET_DV), F32)],
        compiler_params=pltpu.CompilerParams(
            dimension_semantics=("arbitrary",), vmem_limit_bytes=VMEM_LIMIT_BYTES),
        name="mixer",
    )(x.reshape(n_tiles, r, D_MODEL), mod_all, g_all, w_proj_all,
      cos.reshape(n_tiles, r, RET_DK), sin.reshape(n_tiles, r, RET_DK),
      w_alpha_all, b_alpha_all, gla_g_all, gn_g_all, gn_b_all,
      jnp.asarray(c["tri"], BF16), jnp.asarray(c["m_fwd"]), jnp.asarray(c["m_bwd"]),
      jnp.asarray(c["d_ret"]), jnp.asarray(c["qdec"]), jnp.asarray(c["kdec"]),
      jnp.asarray(c["hmask"]),
      x.reshape(n_tiles, r, D_MODEL), mod_all, w_out_all, g_ffn_all, *[a for a, _, _ in cast_jobs])
    return out[0].reshape(x.shape), out[1].reshape(x.shape), out[2:]


ROUTE_ROWS = 512
EXPERT_ROWS = 512
SEG_ALIGN = 8
LOCAL_ROWS = -(-(2 * ROUTE_ROWS + N_EXPERTS * (SEG_ALIGN - 1)) // LANES) * LANES
PIECE_BITS = (max(ROUTE_ROWS, EXPERT_ROWS) // SEG_ALIGN).bit_length()
INFO_POS, INFO_PROB = 0, 2


def _route_kernel(hin_ref, rw_ref, tril_ref, upper_ref, h_ref, info_ref, cnt_ref):
    h = hin_ref[...]
    h_hi = h.astype(BF16)
    h_ref[...] = h_hi
    h_lo = (h - h_hi.astype(F32)).astype(BF16)
    hh = _dot(h_hi, rw_ref[...])
    logits = hh[:, :LANES] + hh[:, LANES:] + _dot(h_lo, rw_ref[:, :LANES])
    lane = lax.broadcasted_iota(jnp.int32, logits.shape, 1)
    neg = jnp.float32(-jnp.inf)
    lg = jnp.where(lane < N_EXPERTS, logits, neg)
    m1 = jnp.max(lg, axis=-1, keepdims=True)
    i1 = jnp.min(jnp.where(lg == m1, lane, LANES), axis=-1, keepdims=True)
    lg2 = jnp.where(lane == i1, neg, lg)
    m2 = jnp.max(lg2, axis=-1, keepdims=True)
    i2 = jnp.min(jnp.where(lg2 == m2, lane, LANES), axis=-1, keepdims=True)
    e2 = jnp.exp(m2 - m1)
    p1 = 1.0 / (1.0 + e2)
    p2 = e2 * p1
    sel1 = lane == i1
    sel2 = lane == i2
    onehot = jnp.where(sel1, 1.0, 0.0) + jnp.where(sel2, 1.0, 0.0)
    incl = _dot(tril_ref[...], onehot.astype(BF16))
    counts = incl[ROUTE_ROWS - 1:ROUTE_ROWS]
    seg_units = jnp.floor((counts + (SEG_ALIGN - 1)) * (1.0 / SEG_ALIGN))
    seg_units8 = jnp.broadcast_to(seg_units, (8, LANES)).astype(BF16)
    seg_off = _dot(seg_units8, upper_ref[...])[0:1] * SEG_ALIGN
    pos = incl - onehot + seg_off
    pos1 = jnp.sum(jnp.where(sel1, pos, 0.0), axis=-1, keepdims=True)
    pos2 = jnp.sum(jnp.where(sel2, pos, 0.0), axis=-1, keepdims=True)
    cnt_ref[...] = jnp.broadcast_to(counts, (8, LANES))
    rec = jnp.zeros(logits.shape, F32)
    for lane_id, val in ((INFO_POS, pos1), (INFO_POS + 1, pos2), (INFO_PROB, p1), (INFO_PROB + 1, p2)):
        rec = jnp.where(lane == lane_id, val, rec)
    info_ref[...] = rec


def _route(h_f32, router_w):
    n = h_f32.shape[0]
    tm = ROUTE_ROWS
    w_hi = router_w.astype(BF16)
    w_lo = (router_w - w_hi.astype(F32)).astype(BF16)
    pad = ((0, 0), (0, LANES - N_EXPERTS))
    rw = jnp.concatenate([jnp.pad(w_hi, pad), jnp.pad(w_lo, pad)], axis=1)
    tril = jnp.asarray(np.tril(np.ones((tm, tm), np.float32)), BF16)
    upper = jnp.asarray(np.triu(np.ones((LANES, LANES), np.float32), 1), BF16)
    return pl.pallas_call(
        _route_kernel,
        grid=(n // tm,),
        in_specs=[pl.BlockSpec((tm, D_MODEL), lambda i: (i, 0)),
                  pl.BlockSpec((D_MODEL, 2 * LANES), lambda i: (0, 0)),
                  pl.BlockSpec((tm, tm), lambda i: (0, 0)),
                  pl.BlockSpec((LANES, LANES), lambda i: (0, 0))],
        out_specs=[pl.BlockSpec((tm, D_MODEL), lambda i: (i, 0)),
                   pl.BlockSpec((tm, LANES), lambda i: (i, 0)),
                   pl.BlockSpec((8, LANES), lambda i: (i, 0))],
        out_shape=[jax.ShapeDtypeStruct((n, D_MODEL), BF16),
                   jax.ShapeDtypeStruct((n, LANES), F32),
                   jax.ShapeDtypeStruct((n // tm * 8, LANES), F32)],
        compiler_params=pltpu.CompilerParams(
            dimension_semantics=("arbitrary",), vmem_limit_bytes=VMEM_LIMIT_BYTES),
        name="route",
    )(h_f32, rw, tril, upper)


def _rows(ref, row, n_rows):
    return ref.at[pl.ds(pl.multiple_of(row, SEG_ALIGN), n_rows)]


def _for_each_piece(n_units, fn):
    for bit in reversed(range(PIECE_BITS)):
        covered = (n_units >> (bit + 1)) << (bit + 1)

        @pl.when(((n_units >> bit) & 1) == 1)
        def _():
            fn(covered * SEG_ALIGN, SEG_ALIGN << bit)


def _for_each_segment_piece(seg_ref, n_seg, tile, fn):
    for e in range(N_EXPERTS):
        s = tile * N_EXPERTS + e
        sorted_row = seg_ref[s]
        local_row = seg_ref[n_seg + s]
        _for_each_piece(seg_ref[2 * n_seg + s],
                        lambda first, n_rows: fn(local_row + first, sorted_row + first, n_rows))


def _selection(info, coeff1, coeff2):
    lane = lax.broadcasted_iota(jnp.int32, (info.shape[0], LOCAL_ROWS), 1)
    pos1 = info[:, INFO_POS:INFO_POS + 1].astype(jnp.int32)
    pos2 = info[:, INFO_POS + 1:INFO_POS + 2].astype(jnp.int32)
    return jnp.where(lane == pos1, coeff1, jnp.where(lane == pos2, coeff2, 0.0)).astype(BF16)


def _dispatch_kernel(n_seg, seg_ref, pad_ref, h_ref, info_ref, xs_ref, sorted_scr, zero_scr, sems, pad_sem):
    i = pl.program_id(0)
    n_steps = pl.num_programs(0)
    slot = i % 2

    def move(tile, buf_slot, op):
        def piece(local_row, sorted_row, n_rows):
            cp = pltpu.make_async_copy(_rows(sorted_scr.at[buf_slot], local_row, n_rows),
                                       _rows(xs_ref, sorted_row, n_rows), sems.at[buf_slot])
            op(cp)
        _for_each_segment_piece(seg_ref, n_seg, tile, piece)

    @pl.when(i == 0)
    def _():
        zero_scr[...] = jnp.zeros_like(zero_scr)
        for op in (lambda cp: cp.start(), lambda cp: cp.wait()):
            for e in range(N_EXPERTS):
                lo = pad_ref[e]
                _for_each_piece(pad_ref[N_EXPERTS + e],
                                lambda first, n_rows: op(pltpu.make_async_copy(
                                    zero_scr.at[pl.ds(0, n_rows)], _rows(xs_ref, lo + first, n_rows),
                                    pad_sem)))

        def tile_copy(j):
            return pltpu.make_async_copy(zero_scr, _rows(xs_ref, j * EXPERT_ROWS, EXPERT_ROWS), pad_sem)

        def fill_tile(j, carry):
            tile_copy(j).start()
            return carry

        def drain_tile(j, carry):
            tile_copy(j).wait()
            return carry

        n_tiles = xs_ref.shape[0] // EXPERT_ROWS
        lax.fori_loop(pad_ref[2 * N_EXPERTS], n_tiles, fill_tile, 0)
        lax.fori_loop(pad_ref[2 * N_EXPERTS], n_tiles, drain_tile, 0)

    @pl.when(i >= 2)
    def _():
        move(i - 2, slot, lambda cp: cp.wait())

    sel = _selection(info_ref[...], 1.0, 1.0)
    sorted_scr[slot] = _dot_tn(sel, h_ref[...])
    move(i, slot, lambda cp: cp.start())

    @pl.when(i == n_steps - 1)
    def _():
        @pl.when(i >= 1)
        def _():
            move(i - 1, 1 - slot, lambda cp: cp.wait())
        move(i, slot, lambda cp: cp.wait())


def _dispatch(h2, info, segs, pads, n_sorted):
    n = h2.shape[0]
    tm = ROUTE_ROWS
    n_seg = n // tm * N_EXPERTS
    return pl.pallas_call(
        functools.partial(_dispatch_kernel, n_seg),
        grid_spec=pltpu.PrefetchScalarGridSpec(
            num_scalar_prefetch=2,
            grid=(n // tm,),
            in_specs=[pl.BlockSpec((tm, D_MODEL), lambda i, s, p: (i, 0)),
                      pl.BlockSpec((tm, LANES), lambda i, s, p: (i, 0))],
            out_specs=pl.BlockSpec(memory_space=pl.ANY),
            scratch_shapes=[pltpu.VMEM((2, LOCAL_ROWS, D_MODEL), F32),
                            pltpu.VMEM((EXPERT_ROWS, D_MODEL), F32),
                            pltpu.SemaphoreType.DMA((2,)), pltpu.SemaphoreType.DMA(())]),
        out_shape=jax.ShapeDtypeStruct((n_sorted, D_MODEL), F32),
        compiler_params=pltpu.CompilerParams(
            dimension_semantics=("arbitrary",), vmem_limit_bytes=VMEM_LIMIT_BYTES),
        name="dispatch",
    )(segs, pads, h2, info)


def _expert_kernel(tile_blk, tile_e, n_act, x_ref, w1_ref, w3_ref, w2_ref, y_ref):
    active = pl.program_id(0) < n_act[0]

    @pl.when(active)
    def _():
        h = x_ref[...].astype(BF16)
        a = _dot(h, w1_ref[0])
        u = (a * _sigmoid(a) * _dot(h, w3_ref[0])).astype(BF16)
        y_ref[...] = _dot(u, w2_ref[0])

    @pl.when(jnp.logical_not(active))
    def _():
        y_ref[...] = jnp.zeros_like(y_ref)


def _experts(xs, tile_blk, tile_e, n_act, w1, w3, w2):
    n_sorted = xs.shape[0]
    tm = EXPERT_ROWS
    f = w1.shape[-1]
    return pl.pallas_call(
        _expert_kernel,
        grid_spec=pltpu.PrefetchScalarGridSpec(
            num_scalar_prefetch=3,
            grid=(n_sorted // tm,),
            in_specs=[pl.BlockSpec((tm, D_MODEL), lambda i, b, e, n: (b[i], 0)),
                      pl.BlockSpec((1, D_MODEL, f), lambda i, b, e, n: (e[i], 0, 0)),
                      pl.BlockSpec((1, D_MODEL, f), lambda i, b, e, n: (e[i], 0, 0)),
                      pl.BlockSpec((1, f, D_MODEL), lambda i, b, e, n: (e[i], 0, 0))],
            out_specs=pl.BlockSpec((tm, D_MODEL), lambda i, b, e, n: (i, 0))),
        out_shape=jax.ShapeDtypeStruct((n_sorted, D_MODEL), F32),
        compiler_params=pltpu.CompilerParams(
            dimension_semantics=("arbitrary",), vmem_limit_bytes=VMEM_LIMIT_BYTES),
        name="experts",
    )(tile_blk, tile_e, n_act, xs, w1, w3, w2)


def _combine_kernel(final_norm, n_seg, seg_ref, x_ref, mod_ref, info_ref, fg_ref, ys_ref, out_ref,
                    ybuf, sems):
    i = pl.program_id(0)
    n_steps = pl.num_programs(0)
    slot = i % 2

    def move(tile, buf_slot, op):
        def piece(local_row, sorted_row, n_rows):
            cp = pltpu.make_async_copy(_rows(ys_ref, sorted_row, n_rows),
                                       _rows(ybuf.at[buf_slot], local_row, n_rows), sems.at[buf_slot])
            op(cp)
        _for_each_segment_piece(seg_ref, n_seg, tile, piece)

    @pl.when(i == 0)
    def _():
        ybuf[...] = jnp.zeros_like(ybuf)
        move(0, 0, lambda cp: cp.start())

    @pl.when(i + 1 < n_steps)
    def _():
        move(i + 1, 1 - slot, lambda cp: cp.start())

    move(i, slot, lambda cp: cp.wait())
    info = info_ref[...]
    sel = _selection(info, info[:, INFO_PROB:INFO_PROB + 1], info[:, INFO_PROB + 1:INFO_PROB + 2])
    y = _dot(sel, ybuf[slot].astype(BF16))
    out = x_ref[...] + _row(mod_ref[0, 0], MOD_G2) * y
    if final_norm:
        ms = jnp.mean(out * out, axis=-1, keepdims=True)
        out = out * lax.rsqrt(ms + EPS) * fg_ref[...]
    out_ref[...] = out


def _combine(layer, x2, mod_all, info, ys, segs, final_g, final_norm, rows_per_batch):
    n = x2.shape[0]
    tm = ROUTE_ROWS
    per_b = rows_per_batch // tm
    n_seg = n // tm * N_EXPERTS
    return pl.pallas_call(
        functools.partial(_combine_kernel, final_norm, n_seg),
        grid_spec=pltpu.PrefetchScalarGridSpec(
            num_scalar_prefetch=1,
            grid=(n // tm,),
            in_specs=[pl.BlockSpec((tm, D_MODEL), lambda i, s: (i, 0)),
                      pl.BlockSpec((1, 1, 6, D_MODEL), lambda i, s: (layer, i // per_b, 0, 0)),
                      pl.BlockSpec((tm, LANES), lambda i, s: (i, 0)),
                      pl.BlockSpec((1, D_MODEL), lambda i, s: (0, 0)),
                      pl.BlockSpec(memory_space=pl.ANY)],
            out_specs=pl.BlockSpec((tm, D_MODEL), lambda i, s: (i, 0)),
            scratch_shapes=[pltpu.VMEM((2, LOCAL_ROWS, D_MODEL), F32),
                            pltpu.SemaphoreType.DMA((2,))]),
        out_shape=jax.ShapeDtypeStruct(x2.shape, F32),
        compiler_params=pltpu.CompilerParams(
            dimension_semantics=("arbitrary",), vmem_limit_bytes=VMEM_LIMIT_BYTES),
        name="combine",
    )(segs, x2, mod_all, info, final_g, ys)


def _moe(layer, x, h_f32, mod_all, router_w, w1, w3, w2, final_g, final_norm):
    bsz, seq, _ = x.shape
    n = bsz * seq
    tm = EXPERT_ROWS
    n_tok_tiles = n // ROUTE_ROWS
    x2 = x.reshape(n, D_MODEL)
    h2, info, cnt = _route(h_f32.reshape(n, D_MODEL), router_w)
    counts = cnt.reshape(n_tok_tiles, 8, LANES)[:, 0, :N_EXPERTS].astype(jnp.int32)
    seg_rows = (counts + (SEG_ALIGN - 1)) // SEG_ALIGN * SEG_ALIGN
    group_rows = jnp.sum(seg_rows, axis=0)
    tiles = (group_rows + (tm - 1)) // tm
    tile_end = jnp.cumsum(tiles)
    group_off = (tile_end - tiles) * tm
    seg_sorted = group_off[None, :] + jnp.cumsum(seg_rows, axis=0) - seg_rows
    seg_local = jnp.cumsum(seg_rows, axis=1) - seg_rows
    segs = jnp.concatenate([seg_sorted.reshape(-1), seg_local.reshape(-1),
                            (seg_rows // SEG_ALIGN).reshape(-1)]).astype(jnp.int32)
    max_rows = 2 * n + n_tok_tiles * N_EXPERTS * (SEG_ALIGN - 1)
    n_tiles = -(-max_rows // tm) + N_EXPERTS
    n_act = tile_end[-1]
    t = jnp.minimum(jnp.arange(n_tiles, dtype=jnp.int32), n_act - 1)
    tile_e = jnp.sum(t[:, None] >= tile_end[None, :], axis=1).astype(jnp.int32)
    pad_lo = group_off + group_rows
    pads = jnp.concatenate([pad_lo, (tile_end * tm - pad_lo) // SEG_ALIGN,
                            n_act[None]]).astype(jnp.int32)
    xs = _dispatch(h2, info, segs, pads, n_tiles * tm)
    ys = _experts(xs, t, tile_e, n_act.reshape(1).astype(jnp.int32), w1, w3, w2)
    out = _combine(layer, x2, mod_all, info, ys, segs, final_g, final_norm, seq)
    return out.reshape(x.shape)


def _ffn_kernel(final_norm, n_groups, x_ref, h_ref, mod_ref, w1_ref, w3_ref, w2_ref, fg_ref, y_ref,
                acc_scr):
    e = pl.program_id(2)

    def group_product():
        h = h_ref[0]
        a = _dot(h, w1_ref[...])
        u = (a * _sigmoid(a) * _dot(h, w3_ref[...])).astype(BF16)
        return _dot(u, w2_ref[...])

    assert n_groups == 2

    @pl.when(e == 0)
    def _():
        acc_scr[...] = group_product()

    @pl.when(e == 1)
    def _():
        out = x_ref[0] + _row(mod_ref[0, 0], MOD_G2) * (acc_scr[...] + group_product())
        if final_norm:
            ms = jnp.mean(out * out, axis=-1, keepdims=True)
            out = out * lax.rsqrt(ms + EPS) * fg_ref[...]
        y_ref[0] = out


def _ffn(layer, x, h2, mod_all, w1, w3, w2, final_g, final_norm):
    bsz, seq, _ = x.shape
    n_groups = 2
    f = w1.shape[-1] // n_groups
    tm = 1024
    row = pl.BlockSpec((1, tm, D_MODEL), lambda b, i, e: (b, i, 0))
    return pl.pallas_call(
        functools.partial(_ffn_kernel, final_norm, n_groups),
        grid=(bsz, seq // tm, n_groups),
        in_specs=[row, row,
                  pl.BlockSpec((1, 1, 6, D_MODEL), lambda b, i, e: (layer, b, 0, 0)),
                  pl.BlockSpec((D_MODEL, f), lambda b, i, e: (0, e)),
                  pl.BlockSpec((D_MODEL, f), lambda b, i, e: (0, e)),
                  pl.BlockSpec((f, D_MODEL), lambda b, i, e: (e, 0)),
                  pl.BlockSpec((1, D_MODEL), lambda b, i, e: (0, 0))],
        out_specs=row,
        out_shape=jax.ShapeDtypeStruct(x.shape, F32),
        scratch_shapes=[pltpu.VMEM((tm, D_MODEL), F32)],
        compiler_params=pltpu.CompilerParams(
            dimension_semantics=("arbitrary", "arbitrary", "arbitrary"),
            vmem_limit_bytes=VMEM_LIMIT_BYTES),
        name="ffn",
    )(x, h2, mod_all, w1, w3, w2, final_g)


def kernel(x, c, positions, ada_w, ada_b, norm_mix_g, norm_ffn_g, w_in, gla_w_alpha, gla_b_alpha,
           gla_norm_g, ret_gn_g, ret_gn_b, w_out, ffn_w1, ffn_w3, ffn_w2, router_w, moe_w1,
           moe_w3, moe_w2, final_g):
    bsz = x.shape[0]
    mod_all = _ada_mod(c, ada_w, ada_b).reshape(DEPTH, bsz, 6, D_MODEL)
    cos, sin = _rope_tables(positions)
    w_bf = w_in.astype(BF16)
    gate_pad = jnp.zeros(w_bf.shape[:2] + (LANES - GLA_RANK,), BF16)
    w_proj_all = jnp.concatenate([w_bf[:, :, :GA_LO + GLA_RANK], gate_pad, w_bf[:, :, GA_LO + GLA_RANK:]],
                                 axis=2)
    w_alpha_all = jnp.pad(gla_w_alpha, ((0, 0), (0, LANES - GLA_RANK), (0, 0))).astype(BF16)
    vec = lambda a: a.reshape(a.shape[0], 1, a.shape[1])
    b_alpha_all, gla_g_all, gn_g_all, gn_b_all = map(vec, (gla_b_alpha, gla_norm_g, ret_gn_g, ret_gn_b))
    g_mix_all, g_ffn_all = vec(norm_mix_g), vec(norm_ffn_g)
    w_out_all = w_out.astype(BF16)
    fg = final_g.reshape(1, D_MODEL)
    d_ff = ffn_w1.shape[-1]
    d_fe = moe_w1.shape[-1]
    n_dense, n_moe = ffn_w1.shape[0], moe_w1.shape[0]
    dense_src = (ffn_w1.reshape(n_dense * FFN_CHUNKS, D_MODEL // FFN_CHUNKS, d_ff),
                 ffn_w3.reshape(n_dense * FFN_CHUNKS, D_MODEL // FFN_CHUNKS, d_ff),
                 ffn_w2.reshape(n_dense * FFN_CHUNKS, d_ff // FFN_CHUNKS, D_MODEL))
    moe_chunks = N_EXPERTS * MOE_CHUNKS_PER_EXPERT
    moe_src = (moe_w1.reshape(n_moe * moe_chunks, D_MODEL // MOE_CHUNKS_PER_EXPERT, d_fe),
               moe_w3.reshape(n_moe * moe_chunks, D_MODEL // MOE_CHUNKS_PER_EXPERT, d_fe),
               moe_w2.reshape(n_moe * moe_chunks, d_fe // MOE_CHUNKS_PER_EXPERT, D_MODEL))
    for layer in range(DEPTH):
        li = layer // 2
        dense = layer % 2 == 0
        jobs = [(a, FFN_CHUNKS if dense else moe_chunks, li) for a in (dense_src if dense else moe_src)]
        x, h2, (w1, w3, w2) = _mixer(layer, x, mod_all, g_mix_all, w_proj_all, cos, sin,
                                     w_alpha_all, b_alpha_all, gla_g_all, gn_g_all, gn_b_all, w_out_all,
                                     g_ffn_all, BF16 if dense else F32, jobs)
        last = layer == DEPTH - 1
        if dense:
            x = _ffn(layer, x, h2, mod_all, w1.reshape(D_MODEL, d_ff), w3.reshape(D_MODEL, d_ff),
                     w2.reshape(d_ff, D_MODEL), fg, last)
        else:
            x = _moe(layer, x, h2, mod_all, router_w[li], w1.reshape(N_EXPERTS, D_MODEL, d_fe),
                     w3.reshape(N_EXPERTS, D_MODEL, d_fe), w2.reshape(N_EXPERTS, d_fe, D_MODEL), fg, last)
    return x
```

```python
import functools

import numpy as np
import jax
import jax.numpy as jnp
from jax import lax
from jax.experimental import pallas as pl
from jax.experimental.pallas import tpu as pltpu

F32 = jnp.float32
BF16 = jnp.bfloat16

D_MODEL = 1024
DEPTH = 4
CHUNK = 64
GLA_HEADS = 4
GLA_DK = 64
GLA_DV = 128
GLA_RANK = 16
GLA_TAU = 16.0
RET_HEADS = 4
RET_DK = 128
RET_DV = 128
ROPE_BASE = 10000.0
N_EXPERTS = 8
EPS = 1e-6
GLA_QK = GLA_HEADS * GLA_DK
GLA_V = GLA_HEADS * GLA_DV
RET_QK = RET_HEADS * RET_DK
RET_V = RET_HEADS * RET_DV
MIX_WIDTH = GLA_V + RET_V

LANES = 128
VMEM_LIMIT_BYTES = 56 * 1024 * 1024

PROJ_WIDTH = 2 * GLA_QK + 2 * GLA_V + 2 * RET_QK + 2 * RET_V
OFF_GQ = 0
OFF_GK = OFF_GQ + GLA_QK
OFF_GV = OFF_GK + GLA_QK
OFF_GR = OFF_GV + GLA_V
OFF_RQ = OFF_GR + GLA_V
OFF_RK = OFF_RQ + RET_QK
OFF_RV = OFF_RK + RET_QK
OFF_RG = OFF_RV + RET_V

ATTN_ROWS = 256
ATTN_CHUNKS = ATTN_ROWS // CHUNK

MOD_SH1, MOD_SC1, MOD_G1, MOD_SH2, MOD_SC2, MOD_G2 = range(6)


def _sigmoid(x):
    return 1.0 / (1.0 + jnp.exp(-x))


def _rms_mod(x, g, sc, sh):
    ms = jnp.mean(x * x, axis=-1, keepdims=True)
    return (x * lax.rsqrt(ms + EPS)) * g * (1.0 + sc) + sh


def _dot(a, b):
    return jnp.dot(a, b, preferred_element_type=F32)


def _dot_nt(a, b):
    return lax.dot_general(a, b, (((1,), (1,)), ((), ())), preferred_element_type=F32)


def _dot_tn(a, b):
    return lax.dot_general(a, b, (((0,), (0,)), ((), ())), preferred_element_type=F32)


def _row(m, r):
    return m[r:r + 1]


def _layer_spec(shape, layer):
    zeros = (0,) * (len(shape) - 1)
    return pl.BlockSpec((1,) + tuple(shape[1:]), lambda *_: (layer,) + zeros)


def _ada_kernel(c_ref, w_ref, b_ref, o_ref):
    c = c_ref[...]
    cond = c * _sigmoid(c)
    c_hi = cond.astype(BF16)
    c_lo = (cond - c_hi.astype(F32)).astype(BF16)
    w = w_ref[0]
    w_hi = w.astype(BF16)
    w_lo = (w - w_hi.astype(F32)).astype(BF16)
    o_ref[0] = _dot(c_hi, w_hi) + _dot(c_hi, w_lo) + _dot(c_lo, w_hi) + b_ref[0]


def _ada_mod(c, ada_w, ada_b):
    bsz = c.shape[0]
    tn = 1024
    n_out = ada_w.shape[-1]
    return pl.pallas_call(
        _ada_kernel,
        grid=(DEPTH, n_out // tn),
        in_specs=[
            pl.BlockSpec((bsz, D_MODEL), lambda l, j: (0, 0)),
            pl.BlockSpec((1, D_MODEL, tn), lambda l, j: (l, 0, j)),
            pl.BlockSpec((1, 1, tn), lambda l, j: (l, 0, j)),
        ],
        out_specs=pl.BlockSpec((1, bsz, tn), lambda l, j: (l, 0, j)),
        out_shape=jax.ShapeDtypeStruct((DEPTH, bsz, n_out), F32),
        name="ada_mod",
    )(c, ada_w, ada_b.reshape(DEPTH, 1, n_out))


def _rope_kernel(pos_ref, invf_ref, sign_ref, cos_ref, sin_ref):
    ang = pos_ref[...] * invf_ref[...]
    cos_ref[...] = jnp.cos(ang)
    sin_ref[...] = jnp.sin(ang) * sign_ref[...]


def _rope_tables(positions):
    n = positions.size
    half = RET_DK // 2
    inv_freq = ROPE_BASE ** (-jnp.arange(half, dtype=F32) / half)
    invf = jnp.concatenate([inv_freq, inv_freq]).reshape(1, RET_DK)
    sign = jnp.concatenate([-jnp.ones((half,), F32), jnp.ones((half,), F32)]).reshape(1, RET_DK)
    pos = positions.astype(F32).reshape(n, 1)
    tm = 2048
    row = pl.BlockSpec((1, RET_DK), lambda i: (0, 0))
    out = pl.BlockSpec((tm, RET_DK), lambda i: (i, 0))
    return pl.pallas_call(
        _rope_kernel,
        grid=(n // tm,),
        in_specs=[pl.BlockSpec((tm, 1), lambda i: (i, 0)), row, row],
        out_specs=[out, out],
        out_shape=[jax.ShapeDtypeStruct((n, RET_DK), F32)] * 2,
        name="rope_tables",
    )(pos, invf, sign)


GA_LO = 2 * GLA_QK + 2 * GLA_V
OFF_GA = PROJ_WIDTH
W_AFTER = GA_LO + LANES


def _prep_w_proj_kernel(w_ref, o_ref):
    o_ref[0, :, :GA_LO] = w_ref[0, :, :GA_LO]
    gate = w_ref[0, :, GA_LO:W_AFTER]
    lane = lax.broadcasted_iota(jnp.int32, gate.shape, 1)
    o_ref[0, :, GA_LO:W_AFTER] = jnp.where(lane < GLA_RANK, gate, jnp.zeros_like(gate))
    o_ref[0, :, W_AFTER:] = w_ref[0, :, GA_LO + GLA_RANK:]


def _prep_w_proj(w_bf):
    depth, d, width = w_bf.shape
    rows = 256
    return pl.pallas_call(
        _prep_w_proj_kernel,
        grid=(depth, d // rows),
        in_specs=[pl.BlockSpec((1, rows, width), lambda l, i: (l, i, 0))],
        out_specs=pl.BlockSpec((1, rows, PROJ_WIDTH + LANES), lambda l, i: (l, i, 0)),
        out_shape=jax.ShapeDtypeStruct((depth, d, PROJ_WIDTH + LANES), BF16),
        name="prep_w_proj",
    )(w_bf)
PROJ_BLOCK = 256
TAIL_JOBS = 2
FFN_CHUNKS = 16
MOE_CHUNKS_PER_EXPERT = 8


def _attn_consts():
    r = ATTN_ROWS
    t = np.arange(r)
    same = (t[:, None] // CHUNK) == (t[None, :] // CHUNK)
    causal = t[:, None] >= t[None, :]
    tri = (same & causal).astype(np.float32)
    m_fwd = tri
    m_bwd = (same & ~causal).astype(np.float32)
    gam = 1.0 - 2.0 ** (-5.0 - np.arange(RET_HEADS, dtype=np.float64))
    lg = np.log(gam)
    dist = (t[:, None] - t[None, :]).astype(np.float64)
    d_ret = np.where(causal[None], np.exp(lg[:, None, None] * dist[None]),
                     np.where(same[None], np.exp(-lg[:, None, None] * dist[None]), 0.0))
    qdec = np.repeat(np.exp(lg[None, :] * (t[:, None] + 1.0)), RET_DK, axis=1)
    kdec = np.repeat(np.exp(lg[None, :] * (r - 1.0 - t[:, None])), RET_DK, axis=1)
    step_decay = [float(np.exp(lg[h] * r)) for h in range(RET_HEADS)]
    hmask = np.zeros((GLA_HEADS, 1, GLA_QK), np.float32)
    for h in range(GLA_HEADS):
        hmask[h, 0, h * GLA_DK:(h + 1) * GLA_DK] = 1.0
    return dict(tri=tri, m_fwd=m_fwd, m_bwd=m_bwd, d_ret=d_ret.astype(np.float32),
                qdec=qdec.astype(np.float32), kdec=kdec.astype(np.float32),
                step_decay=step_decay, hmask=hmask)


def _mixer_kernel(step_decay, steps_per_seq, n_tiles, n_cast, x_ref, mod_ref, g_ref, w_ref, *rest):
    proj_a, proj_b, o_scr, st_g, st_g_bf, st_r = rest[-6:]
    n_mix_in = len(rest) - 6 - 2 * n_cast - 2
    mix_in = rest[:n_mix_in]
    cast_src = rest[n_mix_in:n_mix_in + n_cast]
    y_ref, h2_ref = rest[n_mix_in + n_cast:n_mix_in + n_cast + 2]
    cast_dst = rest[n_mix_in + n_cast + 2:n_mix_in + 2 * n_cast + 2]
    j = pl.program_id(0)
    last = n_tiles
    bufs = (proj_a, proj_b)

    @pl.when((j - 1) % steps_per_seq == 0)
    def _():
        st_g[...] = jnp.zeros_like(st_g)
        st_g_bf[...] = jnp.zeros_like(st_g_bf)
        st_r[...] = jnp.zeros_like(st_r)

    def step(proj_w, proj_r):
        pending = []
        if proj_w is not None:
            m = mod_ref[0, 0]
            h = _rms_mod(x_ref[0], g_ref[0], _row(m, MOD_SC1), _row(m, MOD_SH1)).astype(BF16)

            def block(w0, out0, width):
                def emit():
                    proj_w[:, out0:out0 + width] = _dot(h, w_ref[0, :, w0:w0 + width])
                return emit

            for src, dst in zip(cast_src, cast_dst):
                dst[...] = src[...].astype(BF16)

            pending = ([block(c0, c0, PROJ_BLOCK) for c0 in range(0, GA_LO, PROJ_BLOCK)]
                       + [block(GA_LO, OFF_GA, LANES)]
                       + [block(W_AFTER + c0, GA_LO + c0, PROJ_BLOCK)
                          for c0 in range(0, PROJ_WIDTH - GA_LO, PROJ_BLOCK)])
        if proj_r is None:
            for job in pending:
                job()
        else:
            _attn_body(step_decay, pending, proj_r, *mix_in, y_ref, h2_ref, o_scr, st_g, st_g_bf, st_r)

    @pl.when(j == 0)
    def _():
        step(bufs[0], None)

    for parity in range(2):
        @pl.when(jnp.logical_and(jnp.logical_and(j > 0, j < last), j % 2 == parity))
        def _():
            step(bufs[parity], bufs[1 - parity])

    @pl.when(j == last)
    def _():
        step(None, bufs[(n_tiles - 1) % 2])


def _attn_body(step_decay, pending, proj_ref, cos_ref, sin_ref, walpha_ref, balpha_ref,
               glag_ref, gng_ref, gnb_ref, tri_ref, mfwd_ref, mbwd_ref, dret_ref,
               qdec_ref, kdec_ref, hmask_ref, xres_ref, modres_ref, wout_ref, gffn_ref,
               y_ref, h2_ref, o_scr, st_g, st_g_bf, st_r):
    r = ATTN_ROWS
    pending = list(pending)

    def interleave(n=1):
        for _ in range(min(n, len(pending) - TAIL_JOBS)):
            pending.pop(0)()

    z = _dot(proj_ref[:, OFF_GA:].astype(BF16), walpha_ref[0]) + balpha_ref[0]
    interleave(2)
    log_a = (jnp.minimum(z, 0.0) - jnp.log(1.0 + jnp.exp(-jnp.abs(z)))) * (1.0 / GLA_TAU)
    hi = log_a.astype(BF16)
    r1 = log_a - hi.astype(F32)
    mid = r1.astype(BF16)
    lo = (r1 - mid.astype(F32)).astype(BF16)
    tri = tri_ref[...]
    b = _dot(tri, hi) + _dot(tri, mid) + _dot(tri, lo)
    interleave(2)
    b3 = b.reshape(ATTN_CHUNKS, CHUNK, GLA_QK)
    b_last = b3[:, CHUNK - 1:CHUNK, :]
    k_upd_scale = jnp.exp(b_last - b3).reshape(r, GLA_QK)
    eb = jnp.exp(b)
    enb = jnp.exp(-b)
    q = proj_ref[:,OFF_GQ:OFF_GQ + GLA_QK] * (GLA_DK ** -0.5)
    k = proj_ref[:,OFF_GK:OFF_GK + GLA_QK]
    v_bf = proj_ref[:,OFF_GV:OFF_GV + GLA_V].astype(BF16)
    q_f = q * eb
    q_b = q * enb
    k_f = (k * enb).astype(BF16)
    k_b = (k * eb).astype(BF16)
    k_u = (k * k_upd_scale).astype(BF16)
    q_f_bf = q_f.astype(BF16)

    b_t = b.T
    heads_per_group = LANES // GLA_DK
    inter = []
    for g in range(ATTN_CHUNKS):
        rows = slice(g * CHUNK, (g + 1) * CHUNK)
        inter.append(_dot(q_f_bf[rows], st_g_bf[...]))
        decay = jnp.exp(b_t[:, (g + 1) * CHUNK - 1:(g + 1) * CHUNK])
        for grp in range(GLA_QK // LANES):
            v_lanes = slice(grp * heads_per_group * GLA_DV, (grp + 1) * heads_per_group * GLA_DV)
            upd = _dot_tn(k_u[rows, grp * LANES:(grp + 1) * LANES], v_bf[rows, v_lanes])
            for i in range(heads_per_group):
                h = grp * heads_per_group + i
                keys = slice(h * GLA_DK, (h + 1) * GLA_DK)
                st = (st_g[h] * decay[keys]
                      + upd[i * GLA_DK:(i + 1) * GLA_DK, i * GLA_DV:(i + 1) * GLA_DV])
                st_g[h] = st
                st_g_bf[keys, h * GLA_DV:(h + 1) * GLA_DV] = st.astype(BF16)
        interleave()
    o_inter = jnp.concatenate(inter, axis=0)

    m_fwd = mfwd_ref[...] > 0.5
    m_bwd = mbwd_ref[...] > 0.5
    for h in range(GLA_HEADS):
        hm = hmask_ref[h]
        s_f = _dot_nt((q_f * hm).astype(BF16), k_f)
        s_b = _dot_nt((q_b * hm).astype(BF16), k_b)
        sc = jnp.where(m_fwd, s_f, jnp.where(m_bwd, s_b, 0.0)).astype(BF16)
        lanes = slice(h * GLA_DV, (h + 1) * GLA_DV)
        o_h = _dot(sc, v_bf[:, lanes]) + o_inter[:, lanes]
        interleave()
        o_h = o_h * lax.rsqrt(jnp.mean(o_h * o_h, axis=-1, keepdims=True) + EPS)
        gate = proj_ref[:,OFF_GR + h * GLA_DV:OFF_GR + (h + 1) * GLA_DV]
        o_h = o_h * glag_ref[0, :, lanes] * (gate * _sigmoid(gate))
        o_scr[:, lanes] = o_h.astype(BF16)

    cos = cos_ref[0]
    sin = sin_ref[0]
    for h in range(RET_HEADS):
        lanes = slice(h * RET_DK, (h + 1) * RET_DK)
        qh = proj_ref[:,OFF_RQ + h * RET_DK:OFF_RQ + (h + 1) * RET_DK]
        kh = proj_ref[:,OFF_RK + h * RET_DK:OFF_RK + (h + 1) * RET_DK]
        vh = proj_ref[:,OFF_RV + h * RET_DV:OFF_RV + (h + 1) * RET_DV].astype(BF16)
        qh = (qh * cos + pltpu.roll(qh, RET_DK // 2, axis=1) * sin) * (RET_DK ** -0.5)
        kh = kh * cos + pltpu.roll(kh, RET_DK // 2, axis=1) * sin
        s = _dot_nt(qh.astype(BF16), kh.astype(BF16)) * dret_ref[h]
        st = st_r[h]
        o_h = _dot(s.astype(BF16), vh) + _dot((qh * qdec_ref[:, lanes]).astype(BF16), st.astype(BF16))
        st_r[h] = st * step_decay[h] + _dot_tn((kh * kdec_ref[:, lanes]).astype(BF16), vh)
        interleave()
        mu =jnp.mean(o_h, axis=-1, keepdims=True)
        d = o_h - mu
        var = jnp.mean(d * d, axis=-1, keepdims=True)
        o_h = d * lax.rsqrt(var + EPS) * gng_ref[0, :, lanes] + gnb_ref[0, :, lanes]
        gate = proj_ref[:,OFF_RG + h * RET_DV:OFF_RG + (h + 1) * RET_DV]
        o_h = o_h * (gate * _sigmoid(gate))
        o_scr[:, GLA_V + h * RET_DV:GLA_V + (h + 1) * RET_DV] = o_h.astype(BF16)
    interleave(len(pending))
    m = modres_ref[0, 0]
    y = xres_ref[0] + _row(m, MOD_G1) * _dot(o_scr[...], wout_ref[0])
    for job in pending:
        job()
    y_ref[0] = y
    h2_ref[0] = _rms_mod(y, gffn_ref[0], _row(m, MOD_SC2), _row(m, MOD_SH2)).astype(h2_ref.dtype)


def _mixer(layer, x, mod_all, g_all, w_proj_all, cos, sin, w_alpha_all, b_alpha_all,
           gla_g_all, gn_g_all, gn_b_all, w_out_all, g_ffn_all, h2_dtype, cast_jobs):
    bsz, seq, _ = x.shape
    r = ATTN_ROWS
    steps_per_seq = seq // r
    n_tiles = bsz * steps_per_seq
    c = _attn_consts()
    const2 = lambda shape: pl.BlockSpec(shape, lambda j: (0,) * len(shape))
    per_layer = lambda a: _layer_spec(a.shape, layer)
    proj_tile = lambda j: jnp.minimum(j, n_tiles - 1)
    mix_tile = lambda j: jnp.maximum(j - 1, 0)
    mix_blk = lambda w: pl.BlockSpec((1, r, w), lambda j: (mix_tile(j), 0, 0))

    def chunk_spec(a, chunks, first):
        return pl.BlockSpec((1,) + a.shape[1:], lambda j: (first + jnp.minimum(j, chunks - 1), 0, 0))

    cast_in = [chunk_spec(a, chunks, li * chunks) for a, chunks, li in cast_jobs]
    cast_out = [chunk_spec(a, chunks, 0) for a, chunks, li in cast_jobs]
    cast_shapes = [jax.ShapeDtypeStruct((chunks,) + a.shape[1:], BF16) for a, chunks, li in cast_jobs]
    out = pl.pallas_call(
        functools.partial(_mixer_kernel, c["step_decay"], steps_per_seq, n_tiles, len(cast_jobs)),
        grid=(n_tiles + 1,),
        in_specs=[
            pl.BlockSpec((1, r, D_MODEL), lambda j: (proj_tile(j), 0, 0)),
            pl.BlockSpec((1, 1, 6, D_MODEL), lambda j: (layer, proj_tile(j) // steps_per_seq, 0, 0)),
            per_layer(g_all), per_layer(w_proj_all),
            mix_blk(RET_DK), mix_blk(RET_DK),
            per_layer(w_alpha_all), per_layer(b_alpha_all),
            per_layer(gla_g_all), per_layer(gn_g_all), per_layer(gn_b_all),
            const2((r, r)), const2((r, r)), const2((r, r)), const2((RET_HEADS, r, r)),
            const2((r, RET_QK)), const2((r, RET_QK)),
            const2((GLA_HEADS, 1, GLA_QK)),
            mix_blk(D_MODEL),
            pl.BlockSpec((1, 1, 6, D_MODEL), lambda j: (layer, mix_tile(j) // steps_per_seq, 0, 0)),
            per_layer(w_out_all), per_layer(g_ffn_all),
        ] + cast_in,
        out_specs=[mix_blk(D_MODEL), mix_blk(D_MODEL)] + cast_out,
        out_shape=[jax.ShapeDtypeStruct((n_tiles, r, D_MODEL), F32),
                   jax.ShapeDtypeStruct((n_tiles, r, D_MODEL), h2_dtype)] + cast_shapes,
        scratch_shapes=[pltpu.VMEM((r, PROJ_WIDTH + LANES), F32),
                        pltpu.VMEM((r, PROJ_WIDTH + LANES), F32),
                        pltpu.VMEM((r, MIX_WIDTH), BF16),
                        pltpu.VMEM((GLA_HEADS, GLA_DK, GLA_DV), F32),
                        pltpu.VMEM((GLA_QK, GLA_V), BF16),
                        pltpu.VMEM((RET_HEADS, RET_DK, RET_DV), F32)],
        compiler_params=pltpu.CompilerParams(
            dimension_semantics=("arbitrary",), vmem_limit_bytes=VMEM_LIMIT_BYTES),
        name="mixer",
    )(x.reshape(n_tiles, r, D_MODEL), mod_all, g_all, w_proj_all,
      cos.reshape(n_tiles, r, RET_DK), sin.reshape(n_tiles, r, RET_DK),
      w_alpha_all, b_alpha_all, gla_g_all, gn_g_all, gn_b_all,
      jnp.asarray(c["tri"], BF16), jnp.asarray(c["m_fwd"]), jnp.asarray(c["m_bwd"]),
      jnp.asarray(c["d_ret"]), jnp.asarray(c["qdec"]), jnp.asarray(c["kdec"]),
      jnp.asarray(c["hmask"]),
      x.reshape(n_tiles, r, D_MODEL), mod_all, w_out_all, g_ffn_all, *[a for a, _, _ in cast_jobs])
    return out[0].reshape(x.shape), out[1].reshape(x.shape), out[2:]


ROUTE_ROWS = 512
EXPERT_ROWS = 512
SEG_ALIGN = 8
LOCAL_ROWS = -(-(2 * ROUTE_ROWS + N_EXPERTS * (SEG_ALIGN - 1)) // LANES) * LANES
PIECE_BITS = (max(ROUTE_ROWS, EXPERT_ROWS) // SEG_ALIGN).bit_length()
INFO_POS, INFO_PROB = 0, 2


def _route_kernel(hin_ref, rw_ref, tril_ref, upper_ref, h_ref, info_ref, cnt_ref):
    h = hin_ref[...]
    h_hi = h.astype(BF16)
    h_ref[...] = h_hi
    h_lo = (h - h_hi.astype(F32)).astype(BF16)
    hh = _dot(h_hi, rw_ref[...])
    logits = hh[:, :LANES] + hh[:, LANES:] + _dot(h_lo, rw_ref[:, :LANES])
    lane = lax.broadcasted_iota(jnp.int32, logits.shape, 1)
    neg = jnp.float32(-jnp.inf)
    lg = jnp.where(lane < N_EXPERTS, logits, neg)
    m1 = jnp.max(lg, axis=-1, keepdims=True)
    i1 = jnp.min(jnp.where(lg == m1, lane, LANES), axis=-1, keepdims=True)
    lg2 = jnp.where(lane == i1, neg, lg)
    m2 = jnp.max(lg2, axis=-1, keepdims=True)
    i2 = jnp.min(jnp.where(lg2 == m2, lane, LANES), axis=-1, keepdims=True)
    e2 = jnp.exp(m2 - m1)
    p1 = 1.0 / (1.0 + e2)
    p2 = e2 * p1
    sel1 = lane == i1
    sel2 = lane == i2
    onehot = jnp.where(sel1, 1.0, 0.0) + jnp.where(sel2, 1.0, 0.0)
    incl = _dot(tril_ref[...], onehot.astype(BF16))
    counts = incl[ROUTE_ROWS - 1:ROUTE_ROWS]
    seg_units = jnp.floor((counts + (SEG_ALIGN - 1)) * (1.0 / SEG_ALIGN))
    seg_units8 = jnp.broadcast_to(seg_units, (8, LANES)).astype(BF16)
    seg_off = _dot(seg_units8, upper_ref[...])[0:1] * SEG_ALIGN
    pos = incl - onehot + seg_off
    pos1 = jnp.sum(jnp.where(sel1, pos, 0.0), axis=-1, keepdims=True)
    pos2 = jnp.sum(jnp.where(sel2, pos, 0.0), axis=-1, keepdims=True)
    cnt_ref[...] = jnp.broadcast_to(counts, (8, LANES))
    rec = jnp.zeros(logits.shape, F32)
    for lane_id, val in ((INFO_POS, pos1), (INFO_POS + 1, pos2), (INFO_PROB, p1), (INFO_PROB + 1, p2)):
        rec = jnp.where(lane == lane_id, val, rec)
    info_ref[...] = rec


def _route(h_f32, router_w):
    n = h_f32.shape[0]
    tm = ROUTE_ROWS
    w_hi = router_w.astype(BF16)
    w_lo = (router_w - w_hi.astype(F32)).astype(BF16)
    pad = ((0, 0), (0, LANES - N_EXPERTS))
    rw = jnp.concatenate([jnp.pad(w_hi, pad), jnp.pad(w_lo, pad)], axis=1)
    tril = jnp.asarray(np.tril(np.ones((tm, tm), np.float32)), BF16)
    upper = jnp.asarray(np.triu(np.ones((LANES, LANES), np.float32), 1), BF16)
    return pl.pallas_call(
        _route_kernel,
        grid=(n // tm,),
        in_specs=[pl.BlockSpec((tm, D_MODEL), lambda i: (i, 0)),
                  pl.BlockSpec((D_MODEL, 2 * LANES), lambda i: (0, 0)),
                  pl.BlockSpec((tm, tm), lambda i: (0, 0)),
                  pl.BlockSpec((LANES, LANES), lambda i: (0, 0))],
        out_specs=[pl.BlockSpec((tm, D_MODEL), lambda i: (i, 0)),
                   pl.BlockSpec((tm, LANES), lambda i: (i, 0)),
                   pl.BlockSpec((8, LANES), lambda i: (i, 0))],
        out_shape=[jax.ShapeDtypeStruct((n, D_MODEL), BF16),
                   jax.ShapeDtypeStruct((n, LANES), F32),
                   jax.ShapeDtypeStruct((n // tm * 8, LANES), F32)],
        compiler_params=pltpu.CompilerParams(
            dimension_semantics=("arbitrary",), vmem_limit_bytes=VMEM_LIMIT_BYTES),
        name="route",
    )(h_f32, rw, tril, upper)


def _rows(ref, row, n_rows):
    return ref.at[pl.ds(pl.multiple_of(row, SEG_ALIGN), n_rows)]


def _for_each_piece(n_units, fn):
    for bit in reversed(range(PIECE_BITS)):
        covered = (n_units >> (bit + 1)) << (bit + 1)

        @pl.when(((n_units >> bit) & 1) == 1)
        def _():
            fn(covered * SEG_ALIGN, SEG_ALIGN << bit)


def _for_each_segment_piece(seg_ref, n_seg, tile, fn):
    for e in range(N_EXPERTS):
        s = tile * N_EXPERTS + e
        sorted_row = seg_ref[s]
        local_row = seg_ref[n_seg + s]
        _for_each_piece(seg_ref[2 * n_seg + s],
                        lambda first, n_rows: fn(local_row + first, sorted_row + first, n_rows))


def _selection(info, coeff1, coeff2):
    lane = lax.broadcasted_iota(jnp.int32, (info.shape[0], LOCAL_ROWS), 1)
    pos1 = info[:, INFO_POS:INFO_POS + 1].astype(jnp.int32)
    pos2 = info[:, INFO_POS + 1:INFO_POS + 2].astype(jnp.int32)
    return jnp.where(lane == pos1, coeff1, jnp.where(lane == pos2, coeff2, 0.0)).astype(BF16)


def _dispatch_kernel(n_seg, seg_ref, pad_ref, h_ref, info_ref, xs_ref, sorted_scr, zero_scr, sems, pad_sem):
    i = pl.program_id(0)
    n_steps = pl.num_programs(0)
    slot = i % 2

    def move(tile, buf_slot, op):
        def piece(local_row, sorted_row, n_rows):
            cp = pltpu.make_async_copy(_rows(sorted_scr.at[buf_slot], local_row, n_rows),
                                       _rows(xs_ref, sorted_row, n_rows), sems.at[buf_slot])
            op(cp)
        _for_each_segment_piece(seg_ref, n_seg, tile, piece)

    @pl.when(i == 0)
    def _():
        zero_scr[...] = jnp.zeros_like(zero_scr)
        for op in (lambda cp: cp.start(), lambda cp: cp.wait()):
            for e in range(N_EXPERTS):
                lo = pad_ref[e]
                _for_each_piece(pad_ref[N_EXPERTS + e],
                                lambda first, n_rows: op(pltpu.make_async_copy(
                                    zero_scr.at[pl.ds(0, n_rows)], _rows(xs_ref, lo + first, n_rows),
                                    pad_sem)))

        def tile_copy(j):
            return pltpu.make_async_copy(zero_scr, _rows(xs_ref, j * EXPERT_ROWS, EXPERT_ROWS), pad_sem)

        def fill_tile(j, carry):
            tile_copy(j).start()
            return carry

        def drain_tile(j, carry):
            tile_copy(j).wait()
            return carry

        n_tiles = xs_ref.shape[0] // EXPERT_ROWS
        lax.fori_loop(pad_ref[2 * N_EXPERTS], n_tiles, fill_tile, 0)
        lax.fori_loop(pad_ref[2 * N_EXPERTS], n_tiles, drain_tile, 0)

    @pl.when(i >= 2)
    def _():
        move(i - 2, slot, lambda cp: cp.wait())

    sel = _selection(info_ref[...], 1.0, 1.0)
    sorted_scr[slot] = _dot_tn(sel, h_ref[...])
    move(i, slot, lambda cp: cp.start())

    @pl.when(i == n_steps - 1)
    def _():
        @pl.when(i >= 1)
        def _():
            move(i - 1, 1 - slot, lambda cp: cp.wait())
        move(i, slot, lambda cp: cp.wait())


def _dispatch(h2, info, segs, pads, n_sorted):
    n = h2.shape[0]
    tm = ROUTE_ROWS
    n_seg = n // tm * N_EXPERTS
    return pl.pallas_call(
        functools.partial(_dispatch_kernel, n_seg),
        grid_spec=pltpu.PrefetchScalarGridSpec(
            num_scalar_prefetch=2,
            grid=(n // tm,),
            in_specs=[pl.BlockSpec((tm, D_MODEL), lambda i, s, p: (i, 0)),
                      pl.BlockSpec((tm, LANES), lambda i, s, p: (i, 0))],
            out_specs=pl.BlockSpec(memory_space=pl.ANY),
            scratch_shapes=[pltpu.VMEM((2, LOCAL_ROWS, D_MODEL), F32),
                            pltpu.VMEM((EXPERT_ROWS, D_MODEL), F32),
                            pltpu.SemaphoreType.DMA((2,)), pltpu.SemaphoreType.DMA(())]),
        out_shape=jax.ShapeDtypeStruct((n_sorted, D_MODEL), F32),
        compiler_params=pltpu.CompilerParams(
            dimension_semantics=("arbitrary",), vmem_limit_bytes=VMEM_LIMIT_BYTES),
        name="dispatch",
    )(segs, pads, h2, info)


def _expert_kernel(tile_blk, tile_e, n_act, x_ref, w1_ref, w3_ref, w2_ref, y_ref):
    active = pl.program_id(0) < n_act[0]

    @pl.when(active)
    def _():
        h = x_ref[...].astype(BF16)
        a = _dot(h, w1_ref[0])
        u = (a * _sigmoid(a) * _dot(h, w3_ref[0])).astype(BF16)
        y_ref[...] = _dot(u, w2_ref[0])

    @pl.when(jnp.logical_not(active))
    def _():
        y_ref[...] = jnp.zeros_like(y_ref)


def _experts(xs, tile_blk, tile_e, n_act, w1, w3, w2):
    n_sorted = xs.shape[0]
    tm = EXPERT_ROWS
    f = w1.shape[-1]
    return pl.pallas_call(
        _expert_kernel,
        grid_spec=pltpu.PrefetchScalarGridSpec(
            num_scalar_prefetch=3,
            grid=(n_sorted // tm,),
            in_specs=[pl.BlockSpec((tm, D_MODEL), lambda i, b, e, n: (b[i], 0)),
                      pl.BlockSpec((1, D_MODEL, f), lambda i, b, e, n: (e[i], 0, 0)),
                      pl.BlockSpec((1, D_MODEL, f), lambda i, b, e, n: (e[i], 0, 0)),
                      pl.BlockSpec((1, f, D_MODEL), lambda i, b, e, n: (e[i], 0, 0))],
            out_specs=pl.BlockSpec((tm, D_MODEL), lambda i, b, e, n: (i, 0))),
        out_shape=jax.ShapeDtypeStruct((n_sorted, D_MODEL), F32),
        compiler_params=pltpu.CompilerParams(
            dimension_semantics=("arbitrary",), vmem_limit_bytes=VMEM_LIMIT_BYTES),
        name="experts",
    )(tile_blk, tile_e, n_act, xs, w1, w3, w2)


def _combine_kernel(final_norm, n_seg, seg_ref, x_ref, mod_ref, info_ref, fg_ref, ys_ref, out_ref,
                    ybuf, sems):
    i = pl.program_id(0)
    n_steps = pl.num_programs(0)
    slot = i % 2

    def move(tile, buf_slot, op):
        def piece(local_row, sorted_row, n_rows):
            cp = pltpu.make_async_copy(_rows(ys_ref, sorted_row, n_rows),
                                       _rows(ybuf.at[buf_slot], local_row, n_rows), sems.at[buf_slot])
            op(cp)
        _for_each_segment_piece(seg_ref, n_seg, tile, piece)

    @pl.when(i == 0)
    def _():
        ybuf[...] = jnp.zeros_like(ybuf)
        move(0, 0, lambda cp: cp.start())

    @pl.when(i + 1 < n_steps)
    def _():
        move(i + 1, 1 - slot, lambda cp: cp.start())

    move(i, slot, lambda cp: cp.wait())
    info = info_ref[...]
    sel = _selection(info, info[:, INFO_PROB:INFO_PROB + 1], info[:, INFO_PROB + 1:INFO_PROB + 2])
    y = _dot(sel, ybuf[slot].astype(BF16))
    out = x_ref[...] + _row(mod_ref[0, 0], MOD_G2) * y
    if final_norm:
        ms = jnp.mean(out * out, axis=-1, keepdims=True)
        out = out * lax.rsqrt(ms + EPS) * fg_ref[...]
    out_ref[...] = out


def _combine(layer, x2, mod_all, info, ys, segs, final_g, final_norm, rows_per_batch):
    n = x2.shape[0]
    tm = ROUTE_ROWS
    per_b = rows_per_batch // tm
    n_seg = n // tm * N_EXPERTS
    return pl.pallas_call(
        functools.partial(_combine_kernel, final_norm, n_seg),
        grid_spec=pltpu.PrefetchScalarGridSpec(
            num_scalar_prefetch=1,
            grid=(n // tm,),
            in_specs=[pl.BlockSpec((tm, D_MODEL), lambda i, s: (i, 0)),
                      pl.BlockSpec((1, 1, 6, D_MODEL), lambda i, s: (layer, i // per_b, 0, 0)),
                      pl.BlockSpec((tm, LANES), lambda i, s: (i, 0)),
                      pl.BlockSpec((1, D_MODEL), lambda i, s: (0, 0)),
                      pl.BlockSpec(memory_space=pl.ANY)],
            out_specs=pl.BlockSpec((tm, D_MODEL), lambda i, s: (i, 0)),
            scratch_shapes=[pltpu.VMEM((2, LOCAL_ROWS, D_MODEL), F32),
                            pltpu.SemaphoreType.DMA((2,))]),
        out_shape=jax.ShapeDtypeStruct(x2.shape, F32),
        compiler_params=pltpu.CompilerParams(
            dimension_semantics=("arbitrary",), vmem_limit_bytes=VMEM_LIMIT_BYTES),
        name="combine",
    )(segs, x2, mod_all, info, final_g, ys)


def _moe(layer, x, h_f32, mod_all, router_w, w1, w3, w2, final_g, final_norm):
    bsz, seq, _ = x.shape
    n = bsz * seq
    tm = EXPERT_ROWS
    n_tok_tiles = n // ROUTE_ROWS
    x2 = x.reshape(n, D_MODEL)
    h2, info, cnt = _route(h_f32.reshape(n, D_MODEL), router_w)
    counts = cnt.reshape(n_tok_tiles, 8, LANES)[:, 0, :N_EXPERTS].astype(jnp.int32)
    seg_rows = (counts + (SEG_ALIGN - 1)) // SEG_ALIGN * SEG_ALIGN
    group_rows = jnp.sum(seg_rows, axis=0)
    tiles = (group_rows + (tm - 1)) // tm
    tile_end = jnp.cumsum(tiles)
    group_off = (tile_end - tiles) * tm
    seg_sorted = group_off[None, :] + jnp.cumsum(seg_rows, axis=0) - seg_rows
    seg_local = jnp.cumsum(seg_rows, axis=1) - seg_rows
    segs = jnp.concatenate([seg_sorted.reshape(-1), seg_local.reshape(-1),
                            (seg_rows // SEG_ALIGN).reshape(-1)]).astype(jnp.int32)
    max_rows = 2 * n + n_tok_tiles * N_EXPERTS * (SEG_ALIGN - 1)
    n_tiles = -(-max_rows // tm) + N_EXPERTS
    n_act = tile_end[-1]
    t = jnp.minimum(jnp.arange(n_tiles, dtype=jnp.int32), n_act - 1)
    tile_e = jnp.sum(t[:, None] >= tile_end[None, :], axis=1).astype(jnp.int32)
    pad_lo = group_off + group_rows
    pads = jnp.concatenate([pad_lo, (tile_end * tm - pad_lo) // SEG_ALIGN,
                            n_act[None]]).astype(jnp.int32)
    xs = _dispatch(h2, info, segs, pads, n_tiles * tm)
    ys = _experts(xs, t, tile_e, n_act.reshape(1).astype(jnp.int32), w1, w3, w2)
    out = _combine(layer, x2, mod_all, info, ys, segs, final_g, final_norm, seq)
    return out.reshape(x.shape)


def _ffn_kernel(final_norm, n_groups, x_ref, h_ref, mod_ref, w1_ref, w3_ref, w2_ref, fg_ref, y_ref,
                acc_scr):
    e = pl.program_id(2)

    def group_product():
        h = h_ref[0]
        a = _dot(h, w1_ref[...])
        u = (a * _sigmoid(a) * _dot(h, w3_ref[...])).astype(BF16)
        return _dot(u, w2_ref[...])

    assert n_groups == 2

    @pl.when(e == 0)
    def _():
        acc_scr[...] = group_product()

    @pl.when(e == 1)
    def _():
        out = x_ref[0] + _row(mod_ref[0, 0], MOD_G2) * (acc_scr[...] + group_product())
        if final_norm:
            ms = jnp.mean(out * out, axis=-1, keepdims=True)
            out = out * lax.rsqrt(ms + EPS) * fg_ref[...]
        y_ref[0] = out


def _ffn(layer, x, h2, mod_all, w1, w3, w2, final_g, final_norm):
    bsz, seq, _ = x.shape
    n_groups = 2
    f = w1.shape[-1] // n_groups
    tm = 1024
    row = pl.BlockSpec((1, tm, D_MODEL), lambda b, i, e: (b, i, 0))
    return pl.pallas_call(
        functools.partial(_ffn_kernel, final_norm, n_groups),
        grid=(bsz, seq // tm, n_groups),
        in_specs=[row, row,
                  pl.BlockSpec((1, 1, 6, D_MODEL), lambda b, i, e: (layer, b, 0, 0)),
                  pl.BlockSpec((D_MODEL, f), lambda b, i, e: (0, e)),
                  pl.BlockSpec((D_MODEL, f), lambda b, i, e: (0, e)),
                  pl.BlockSpec((f, D_MODEL), lambda b, i, e: (e, 0)),
                  pl.BlockSpec((1, D_MODEL), lambda b, i, e: (0, 0))],
        out_specs=row,
        out_shape=jax.ShapeDtypeStruct(x.shape, F32),
        scratch_shapes=[pltpu.VMEM((tm, D_MODEL), F32)],
        compiler_params=pltpu.CompilerParams(
            dimension_semantics=("arbitrary", "arbitrary", "arbitrary"),
            vmem_limit_bytes=VMEM_LIMIT_BYTES),
        name="ffn",
    )(x, h2, mod_all, w1, w3, w2, final_g)


def kernel(x, c, positions, ada_w, ada_b, norm_mix_g, norm_ffn_g, w_in, gla_w_alpha, gla_b_alpha,
           gla_norm_g, ret_gn_g, ret_gn_b, w_out, ffn_w1, ffn_w3, ffn_w2, router_w, moe_w1,
           moe_w3, moe_w2, final_g):
    bsz = x.shape[0]
    mod_all = _ada_mod(c, ada_w, ada_b).reshape(DEPTH, bsz, 6, D_MODEL)
    cos, sin = _rope_tables(positions)
    w_proj_all = _prep_w_proj(w_in.astype(BF16))
    w_alpha_all = jnp.pad(gla_w_alpha, ((0, 0), (0, LANES - GLA_RANK), (0, 0))).astype(BF16)
    vec = lambda a: a.reshape(a.shape[0], 1, a.shape[1])
    b_alpha_all, gla_g_all, gn_g_all, gn_b_all = map(vec, (gla_b_alpha, gla_norm_g, ret_gn_g, ret_gn_b))
    g_mix_all, g_ffn_all = vec(norm_mix_g), vec(norm_ffn_g)
    w_out_all = w_out.astype(BF16)
    fg = final_g.reshape(1, D_MODEL)
    d_ff = ffn_w1.shape[-1]
    d_fe = moe_w1.shape[-1]
    n_dense, n_moe = ffn_w1.shape[0], moe_w1.shape[0]
    dense_src = (ffn_w1.reshape(n_dense * FFN_CHUNKS, D_MODEL // FFN_CHUNKS, d_ff),
                 ffn_w3.reshape(n_dense * FFN_CHUNKS, D_MODEL // FFN_CHUNKS, d_ff),
                 ffn_w2.reshape(n_dense * FFN_CHUNKS, d_ff // FFN_CHUNKS, D_MODEL))
    moe_chunks = N_EXPERTS * MOE_CHUNKS_PER_EXPERT
    moe_src = (moe_w1.reshape(n_moe * moe_chunks, D_MODEL // MOE_CHUNKS_PER_EXPERT, d_fe),
               moe_w3.reshape(n_moe * moe_chunks, D_MODEL // MOE_CHUNKS_PER_EXPERT, d_fe),
               moe_w2.reshape(n_moe * moe_chunks, d_fe // MOE_CHUNKS_PER_EXPERT, D_MODEL))
    for layer in range(DEPTH):
        li = layer // 2
        dense = layer % 2 == 0
        jobs = [(a, FFN_CHUNKS if dense else moe_chunks, li) for a in (dense_src if dense else moe_src)]
        x, h2, (w1, w3, w2) = _mixer(layer, x, mod_all, g_mix_all, w_proj_all, cos, sin,
                                     w_alpha_all, b_alpha_all, gla_g_all, gn_g_all, gn_b_all, w_out_all,
                                     g_ffn_all, BF16 if dense else F32, jobs)
        last = layer == DEPTH - 1
        if dense:
            x = _ffn(layer, x, h2, mod_all, w1.reshape(D_MODEL, d_ff), w3.reshape(D_MODEL, d_ff),
                     w2.reshape(d_ff, D_MODEL), fg, last)
        else:
            x = _moe(layer, x, h2, mod_all, router_w[li], w1.reshape(N_EXPERTS, D_MODEL, d_fe),
                     w3.reshape(N_EXPERTS, D_MODEL, d_fe), w2.reshape(N_EXPERTS, d_fe, D_MODEL), fg, last)
    return x
```

```python
import functools

import numpy as np
import jax
import jax.numpy as jnp
from jax import lax
from jax.experimental import pallas as pl
from jax.experimental.pallas import tpu as pltpu

F32 = jnp.float32
BF16 = jnp.bfloat16

D_MODEL = 1024
DEPTH = 4
CHUNK = 64
GLA_HEADS = 4
GLA_DK = 64
GLA_DV = 128
GLA_RANK = 16
GLA_TAU = 16.0
RET_HEADS = 4
RET_DK = 128
RET_DV = 128
ROPE_BASE = 10000.0
N_EXPERTS = 8
EPS = 1e-6
GLA_QK = GLA_HEADS * GLA_DK
GLA_V = GLA_HEADS * GLA_DV
RET_QK = RET_HEADS * RET_DK
RET_V = RET_HEADS * RET_DV
MIX_WIDTH = GLA_V + RET_V

LANES = 128
VMEM_LIMIT_BYTES = 56 * 1024 * 1024

PROJ_WIDTH = 2 * GLA_QK + 2 * GLA_V + 2 * RET_QK + 2 * RET_V
OFF_GQ = 0
OFF_GK = OFF_GQ + GLA_QK
OFF_GV = OFF_GK + GLA_QK
OFF_GR = OFF_GV + GLA_V
OFF_RQ = OFF_GR + GLA_V
OFF_RK = OFF_RQ + RET_QK
OFF_RV = OFF_RK + RET_QK
OFF_RG = OFF_RV + RET_V

ATTN_ROWS = 256
ATTN_CHUNKS = ATTN_ROWS // CHUNK

MOD_SH1, MOD_SC1, MOD_G1, MOD_SH2, MOD_SC2, MOD_G2 = range(6)


def _sigmoid(x):
    return 1.0 / (1.0 + jnp.exp(-x))


def _rms_mod(x, g, sc, sh):
    ms = jnp.mean(x * x, axis=-1, keepdims=True)
    return (x * lax.rsqrt(ms + EPS)) * g * (1.0 + sc) + sh


def _dot(a, b):
    return jnp.dot(a, b, preferred_element_type=F32)


def _dot_nt(a, b):
    return lax.dot_general(a, b, (((1,), (1,)), ((), ())), preferred_element_type=F32)


def _dot_tn(a, b):
    return lax.dot_general(a, b, (((0,), (0,)), ((), ())), preferred_element_type=F32)


def _row(m, r):
    return m[r:r + 1]


def _layer_spec(shape, layer):
    zeros = (0,) * (len(shape) - 1)
    return pl.BlockSpec((1,) + tuple(shape[1:]), lambda *_: (layer,) + zeros)


def _ada_kernel(c_ref, w_ref, b_ref, o_ref):
    c = c_ref[...]
    cond = c * _sigmoid(c)
    c_hi = cond.astype(BF16)
    c_lo = (cond - c_hi.astype(F32)).astype(BF16)
    w = w_ref[0]
    w_hi = w.astype(BF16)
    w_lo = (w - w_hi.astype(F32)).astype(BF16)
    o_ref[0] = _dot(c_hi, w_hi) + _dot(c_hi, w_lo) + _dot(c_lo, w_hi) + b_ref[0]


def _ada_mod(c, ada_w, ada_b):
    bsz = c.shape[0]
    tn = 2048
    n_out = ada_w.shape[-1]
    return pl.pallas_call(
        _ada_kernel,
        grid=(DEPTH, n_out // tn),
        in_specs=[
            pl.BlockSpec((bsz, D_MODEL), lambda l, j: (0, 0)),
            pl.BlockSpec((1, D_MODEL, tn), lambda l, j: (l, 0, j)),
            pl.BlockSpec((1, 1, tn), lambda l, j: (l, 0, j)),
        ],
        out_specs=pl.BlockSpec((1, bsz, tn), lambda l, j: (l, 0, j)),
        out_shape=jax.ShapeDtypeStruct((DEPTH, bsz, n_out), F32),
        compiler_params=pltpu.CompilerParams(vmem_limit_bytes=VMEM_LIMIT_BYTES),
        name="ada_mod",
    )(c, ada_w, ada_b.reshape(DEPTH, 1, n_out))


def _rope_kernel(pos_ref, invf_ref, sign_ref, cos_ref, sin_ref):
    ang = pos_ref[...] * invf_ref[...]
    cos_ref[...] = jnp.cos(ang)
    sin_ref[...] = jnp.sin(ang) * sign_ref[...]


def _rope_tables(positions):
    n = positions.size
    half = RET_DK // 2
    inv_freq = ROPE_BASE ** (-jnp.arange(half, dtype=F32) / half)
    invf = jnp.concatenate([inv_freq, inv_freq]).reshape(1, RET_DK)
    sign = jnp.concatenate([-jnp.ones((half,), F32), jnp.ones((half,), F32)]).reshape(1, RET_DK)
    pos = positions.astype(F32).reshape(n, 1)
    tm = 2048
    row = pl.BlockSpec((1, RET_DK), lambda i: (0, 0))
    out = pl.BlockSpec((tm, RET_DK), lambda i: (i, 0))
    return pl.pallas_call(
        _rope_kernel,
        grid=(n // tm,),
        in_specs=[pl.BlockSpec((tm, 1), lambda i: (i, 0)), row, row],
        out_specs=[out, out],
        out_shape=[jax.ShapeDtypeStruct((n, RET_DK), F32)] * 2,
        name="rope_tables",
    )(pos, invf, sign)


GA_LO = 2 * GLA_QK + 2 * GLA_V
OFF_GA = PROJ_WIDTH
W_AFTER = GA_LO + LANES


def _prep_w_proj_kernel(w_ref, o_ref):
    o_ref[0, :, :GA_LO] = w_ref[0, :, :GA_LO]
    gate = w_ref[0, :, GA_LO:W_AFTER]
    lane = lax.broadcasted_iota(jnp.int32, gate.shape, 1)
    o_ref[0, :, GA_LO:W_AFTER] = jnp.where(lane < GLA_RANK, gate, jnp.zeros_like(gate))
    o_ref[0, :, W_AFTER:] = w_ref[0, :, GA_LO + GLA_RANK:GA_LO + GLA_RANK + PROJ_WIDTH - GA_LO]


def _prep_w_proj(w_bf):
    depth, d, width = w_bf.shape
    rows = 256
    return pl.pallas_call(
        _prep_w_proj_kernel,
        grid=(depth, d // rows),
        in_specs=[pl.BlockSpec((1, rows, width), lambda l, i: (l, i, 0))],
        out_specs=pl.BlockSpec((1, rows, PROJ_WIDTH + LANES), lambda l, i: (l, i, 0)),
        out_shape=jax.ShapeDtypeStruct((depth, d, PROJ_WIDTH + LANES), BF16),
        name="prep_w_proj",
    )(w_bf)
PROJ_BLOCK = 256
TAIL_JOBS = 2
FFN_CHUNKS = 16
MOE_CHUNKS_PER_EXPERT = 8


def _attn_consts():
    r = ATTN_ROWS
    t = np.arange(r)
    same = (t[:, None] // CHUNK) == (t[None, :] // CHUNK)
    causal = t[:, None] >= t[None, :]
    tri = (same & causal).astype(np.float32)
    m_fwd = tri
    m_bwd = (same & ~causal).astype(np.float32)
    gam = 1.0 - 2.0 ** (-5.0 - np.arange(RET_HEADS, dtype=np.float64))
    lg = np.log(gam)
    dist = (t[:, None] - t[None, :]).astype(np.float64)
    d_ret = np.where(causal[None], np.exp(lg[:, None, None] * dist[None]),
                     np.where(same[None], np.exp(-lg[:, None, None] * dist[None]), 0.0))
    qdec = np.repeat(np.exp(lg[None, :] * (t[:, None] + 1.0)), RET_DK, axis=1)
    kdec = np.repeat(np.exp(lg[None, :] * (r - 1.0 - t[:, None])), RET_DK, axis=1)
    step_decay = [float(np.exp(lg[h] * r)) for h in range(RET_HEADS)]
    hmask = np.zeros((GLA_HEADS, 1, GLA_QK), np.float32)
    for h in range(GLA_HEADS):
        hmask[h, 0, h * GLA_DK:(h + 1) * GLA_DK] = 1.0
    return dict(tri=tri, m_fwd=m_fwd, m_bwd=m_bwd, d_ret=d_ret.astype(np.float32),
                qdec=qdec.astype(np.float32), kdec=kdec.astype(np.float32),
                step_decay=step_decay, hmask=hmask)


def _mixer_kernel(step_decay, steps_per_seq, n_tiles, n_cast, x_ref, mod_ref, g_ref, w_ref, *rest):
    proj_a, proj_b, o_scr, st_g, st_g_bf, st_r = rest[-6:]
    n_mix_in = len(rest) - 6 - 2 * n_cast - 2
    mix_in = rest[:n_mix_in]
    cast_src = rest[n_mix_in:n_mix_in + n_cast]
    y_ref, h2_ref = rest[n_mix_in + n_cast:n_mix_in + n_cast + 2]
    cast_dst = rest[n_mix_in + n_cast + 2:n_mix_in + 2 * n_cast + 2]
    j = pl.program_id(0)
    last = n_tiles
    bufs = (proj_a, proj_b)

    @pl.when((j - 1) % steps_per_seq == 0)
    def _():
        st_g[...] = jnp.zeros_like(st_g)
        st_g_bf[...] = jnp.zeros_like(st_g_bf)
        st_r[...] = jnp.zeros_like(st_r)

    def step(proj_w, proj_r):
        pending = []
        if proj_w is not None:
            m = mod_ref[0, 0]
            h = _rms_mod(x_ref[0], g_ref[0], _row(m, MOD_SC1), _row(m, MOD_SH1)).astype(BF16)

            def block(w0, out0, width):
                def emit():
                    proj_w[:, out0:out0 + width] = _dot(h, w_ref[0, :, w0:w0 + width])
                return emit

            for src, dst in zip(cast_src, cast_dst):
                dst[...] = src[...].astype(BF16)

            pending = ([block(c0, c0, PROJ_BLOCK) for c0 in range(0, GA_LO, PROJ_BLOCK)]
                       + [block(GA_LO, OFF_GA, LANES)]
                       + [block(W_AFTER + c0, GA_LO + c0, PROJ_BLOCK)
                          for c0 in range(0, PROJ_WIDTH - GA_LO, PROJ_BLOCK)])
        if proj_r is None:
            for job in pending:
                job()
        else:
            _attn_body(step_decay, pending, proj_r, *mix_in, y_ref, h2_ref, o_scr, st_g, st_g_bf, st_r)

    @pl.when(j == 0)
    def _():
        step(bufs[0], None)

    for parity in range(2):
        @pl.when(jnp.logical_and(jnp.logical_and(j > 0, j < last), j % 2 == parity))
        def _():
            step(bufs[parity], bufs[1 - parity])

    @pl.when(j == last)
    def _():
        step(None, bufs[(n_tiles - 1) % 2])


def _attn_body(step_decay, pending, proj_ref, cos_ref, sin_ref, walpha_ref, balpha_ref,
               glag_ref, gng_ref, gnb_ref, tri_ref, mfwd_ref, mbwd_ref, dret_ref,
               qdec_ref, kdec_ref, hmask_ref, xres_ref, modres_ref, wout_ref, gffn_ref,
               y_ref, h2_ref, o_scr, st_g, st_g_bf, st_r):
    r = ATTN_ROWS
    pending = list(pending)

    def interleave(n=1):
        for _ in range(min(n, len(pending) - TAIL_JOBS)):
            pending.pop(0)()

    z = _dot(proj_ref[:, OFF_GA:].astype(BF16), walpha_ref[0]) + balpha_ref[0]
    interleave(2)
    log_a = (jnp.minimum(z, 0.0) - jnp.log(1.0 + jnp.exp(-jnp.abs(z)))) * (1.0 / GLA_TAU)
    hi = log_a.astype(BF16)
    r1 = log_a - hi.astype(F32)
    mid = r1.astype(BF16)
    lo = (r1 - mid.astype(F32)).astype(BF16)
    tri = tri_ref[...]
    b = _dot(tri, hi) + _dot(tri, mid) + _dot(tri, lo)
    interleave(2)
    b3 = b.reshape(ATTN_CHUNKS, CHUNK, GLA_QK)
    b_last = b3[:, CHUNK - 1:CHUNK, :]
    k_upd_scale = jnp.exp(b_last - b3).reshape(r, GLA_QK)
    eb = jnp.exp(b)
    enb = jnp.exp(-b)
    q = proj_ref[:,OFF_GQ:OFF_GQ + GLA_QK] * (GLA_DK ** -0.5)
    k = proj_ref[:,OFF_GK:OFF_GK + GLA_QK]
    v_bf = proj_ref[:,OFF_GV:OFF_GV + GLA_V].astype(BF16)
    q_f = q * eb
    q_b = q * enb
    k_f = (k * enb).astype(BF16)
    k_b = (k * eb).astype(BF16)
    k_u = (k * k_upd_scale).astype(BF16)
    q_f_bf = q_f.astype(BF16)

    b_t = b.T
    heads_per_group = LANES // GLA_DK
    inter = []
    for g in range(ATTN_CHUNKS):
        rows = slice(g * CHUNK, (g + 1) * CHUNK)
        inter.append(_dot(q_f_bf[rows], st_g_bf[...]))
        decay = jnp.exp(b_t[:, (g + 1) * CHUNK - 1:(g + 1) * CHUNK])
        for grp in range(GLA_QK // LANES):
            v_lanes = slice(grp * heads_per_group * GLA_DV, (grp + 1) * heads_per_group * GLA_DV)
            upd = _dot_tn(k_u[rows, grp * LANES:(grp + 1) * LANES], v_bf[rows, v_lanes])
            for i in range(heads_per_group):
                h = grp * heads_per_group + i
                keys = slice(h * GLA_DK, (h + 1) * GLA_DK)
                st = (st_g[h] * decay[keys]
                      + upd[i * GLA_DK:(i + 1) * GLA_DK, i * GLA_DV:(i + 1) * GLA_DV])
                st_g[h] = st
                st_g_bf[keys, h * GLA_DV:(h + 1) * GLA_DV] = st.astype(BF16)
        interleave()
    o_inter = jnp.concatenate(inter, axis=0)

    m_fwd = mfwd_ref[...] > 0.5
    m_bwd = mbwd_ref[...] > 0.5
    for h in range(GLA_HEADS):
        hm = hmask_ref[h]
        s_f = _dot_nt((q_f * hm).astype(BF16), k_f)
        s_b = _dot_nt((q_b * hm).astype(BF16), k_b)
        sc = jnp.where(m_fwd, s_f, jnp.where(m_bwd, s_b, 0.0)).astype(BF16)
        lanes = slice(h * GLA_DV, (h + 1) * GLA_DV)
        o_h = _dot(sc, v_bf[:, lanes]) + o_inter[:, lanes]
        interleave()
        o_h = o_h * lax.rsqrt(jnp.mean(o_h * o_h, axis=-1, keepdims=True) + EPS)
        gate = proj_ref[:,OFF_GR + h * GLA_DV:OFF_GR + (h + 1) * GLA_DV]
        o_h = o_h * glag_ref[0, :, lanes] * (gate * _sigmoid(gate))
        o_scr[:, lanes] = o_h.astype(BF16)

    cos = cos_ref[0]
    sin = sin_ref[0]
    for h in range(RET_HEADS):
        lanes = slice(h * RET_DK, (h + 1) * RET_DK)
        qh = proj_ref[:,OFF_RQ + h * RET_DK:OFF_RQ + (h + 1) * RET_DK]
        kh = proj_ref[:,OFF_RK + h * RET_DK:OFF_RK + (h + 1) * RET_DK]
        vh = proj_ref[:,OFF_RV + h * RET_DV:OFF_RV + (h + 1) * RET_DV].astype(BF16)
        qh = (qh * cos + pltpu.roll(qh, RET_DK // 2, axis=1) * sin) * (RET_DK ** -0.5)
        kh = kh * cos + pltpu.roll(kh, RET_DK // 2, axis=1) * sin
        s = _dot_nt(qh.astype(BF16), kh.astype(BF16)) * dret_ref[h]
        st = st_r[h]
        o_h = _dot(s.astype(BF16), vh) + _dot((qh * qdec_ref[:, lanes]).astype(BF16), st.astype(BF16))
        st_r[h] = st * step_decay[h] + _dot_tn((kh * kdec_ref[:, lanes]).astype(BF16), vh)
        interleave()
        mu =jnp.mean(o_h, axis=-1, keepdims=True)
        d = o_h - mu
        var = jnp.mean(d * d, axis=-1, keepdims=True)
        o_h = d * lax.rsqrt(var + EPS) * gng_ref[0, :, lanes] + gnb_ref[0, :, lanes]
        gate = proj_ref[:,OFF_RG + h * RET_DV:OFF_RG + (h + 1) * RET_DV]
        o_h = o_h * (gate * _sigmoid(gate))
        o_scr[:, GLA_V + h * RET_DV:GLA_V + (h + 1) * RET_DV] = o_h.astype(BF16)
    interleave(len(pending))
    m = modres_ref[0, 0]
    y = xres_ref[0] + _row(m, MOD_G1) * _dot(o_scr[...], wout_ref[0])
    for job in pending:
        job()
    y_ref[0] = y
    h2_ref[0] = _rms_mod(y, gffn_ref[0], _row(m, MOD_SC2), _row(m, MOD_SH2)).astype(h2_ref.dtype)


def _mixer(layer, x, mod_all, g_all, w_proj_all, cos, sin, w_alpha_all, b_alpha_all,
           gla_g_all, gn_g_all, gn_b_all, w_out_all, g_ffn_all, h2_dtype, cast_jobs):
    bsz, seq, _ = x.shape
    r = ATTN_ROWS
    steps_per_seq = seq // r
    n_tiles = bsz * steps_per_seq
    c = _attn_consts()
    const2 = lambda shape: pl.BlockSpec(shape, lambda j: (0,) * len(shape))
    per_layer = lambda a: _layer_spec(a.shape, layer)
    proj_tile = lambda j: jnp.minimum(j, n_tiles - 1)
    mix_tile = lambda j: jnp.maximum(j - 1, 0)
    mix_blk = lambda w: pl.BlockSpec((1, r, w), lambda j: (mix_tile(j), 0, 0))

    def chunk_spec(a, chunks, first):
        return pl.BlockSpec((1,) + a.shape[1:], lambda j: (first + jnp.minimum(j, chunks - 1), 0, 0))

    cast_in = [chunk_spec(a, chunks, li * chunks) for a, chunks, li in cast_jobs]
    cast_out = [chunk_spec(a, chunks, 0) for a, chunks, li in cast_jobs]
    cast_shapes = [jax.ShapeDtypeStruct((chunks,) + a.shape[1:], BF16) for a, chunks, li in cast_jobs]
    out = pl.pallas_call(
        functools.partial(_mixer_kernel, c["step_decay"], steps_per_seq, n_tiles, len(cast_jobs)),
        grid=(n_tiles + 1,),
        in_specs=[
            pl.BlockSpec((1, r, D_MODEL), lambda j: (proj_tile(j), 0, 0)),
            pl.BlockSpec((1, 1, 6, D_MODEL), lambda j: (layer, proj_tile(j) // steps_per_seq, 0, 0)),
            per_layer(g_all), per_layer(w_proj_all),
            mix_blk(RET_DK), mix_blk(RET_DK),
            per_layer(w_alpha_all), per_layer(b_alpha_all),
            per_layer(gla_g_all), per_layer(gn_g_all), per_layer(gn_b_all),
            const2((r, r)), const2((r, r)), const2((r, r)), const2((RET_HEADS, r, r)),
            const2((r, RET_QK)), const2((r, RET_QK)),
            const2((GLA_HEADS, 1, GLA_QK)),
            mix_blk(D_MODEL),
            pl.BlockSpec((1, 1, 6, D_MODEL), lambda j: (layer, mix_tile(j) // steps_per_seq, 0, 0)),
            per_layer(w_out_all), per_layer(g_ffn_all),
        ] + cast_in,
        out_specs=[mix_blk(D_MODEL), mix_blk(D_MODEL)] + cast_out,
        out_shape=[jax.ShapeDtypeStruct((n_tiles, r, D_MODEL), F32),
                   jax.ShapeDtypeStruct((n_tiles, r, D_MODEL), h2_dtype)] + cast_shapes,
        scratch_shapes=[pltpu.VMEM((r, PROJ_WIDTH + LANES), F32),
                        pltpu.VMEM((r, PROJ_WIDTH + LANES), F32),
                        pltpu.VMEM((r, MIX_WIDTH), BF16),
                        pltpu.VMEM((GLA_HEADS, GLA_DK, GLA_DV), F32),
                        pltpu.VMEM((GLA_QK, GLA_V), BF16),
                        pltpu.VMEM((RET_HEADS, RET_DK, RET_DV), F32)],
        compiler_params=pltpu.CompilerParams(
            dimension_semantics=("arbitrary",), vmem_limit_bytes=VMEM_LIMIT_BYTES),
        name="mixer",
    )(x.reshape(n_tiles, r, D_MODEL), mod_all, g_all, w_proj_all,
      cos.reshape(n_tiles, r, RET_DK), sin.reshape(n_tiles, r, RET_DK),
      w_alpha_all, b_alpha_all, gla_g_all, gn_g_all, gn_b_all,
      jnp.asarray(c["tri"], BF16), jnp.asarray(c["m_fwd"]), jnp.asarray(c["m_bwd"]),
      jnp.asarray(c["d_ret"]), jnp.asarray(c["qdec"]), jnp.asarray(c["kdec"]),
      jnp.asarray(c["hmask"]),
      x.reshape(n_tiles, r, D_MODEL), mod_all, w_out_all, g_ffn_all, *[a for a, _, _ in cast_jobs])
    return out[0].reshape(x.shape), out[1].reshape(x.shape), out[2:]


ROUTE_ROWS = 512
EXPERT_ROWS = 512
SEG_ALIGN = 8
LOCAL_ROWS = -(-(2 * ROUTE_ROWS + N_EXPERTS * (SEG_ALIGN - 1)) // LANES) * LANES
PIECE_BITS = (max(ROUTE_ROWS, EXPERT_ROWS) // SEG_ALIGN).bit_length()
INFO_POS, INFO_PROB = 0, 2


def _route_kernel(hin_ref, rw_ref, tril_ref, upper_ref, h_ref, info_ref, cnt_ref):
    h = hin_ref[...]
    h_hi = h.astype(BF16)
    h_ref[...] = h_hi
    h_lo = (h - h_hi.astype(F32)).astype(BF16)
    hh = _dot(h_hi, rw_ref[...])
    logits = hh[:, :LANES] + hh[:, LANES:] + _dot(h_lo, rw_ref[:, :LANES])
    lane = lax.broadcasted_iota(jnp.int32, logits.shape, 1)
    neg = jnp.float32(-jnp.inf)
    lg = jnp.where(lane < N_EXPERTS, logits, neg)
    m1 = jnp.max(lg, axis=-1, keepdims=True)
    i1 = jnp.min(jnp.where(lg == m1, lane, LANES), axis=-1, keepdims=True)
    lg2 = jnp.where(lane == i1, neg, lg)
    m2 = jnp.max(lg2, axis=-1, keepdims=True)
    i2 = jnp.min(jnp.where(lg2 == m2, lane, LANES), axis=-1, keepdims=True)
    e2 = jnp.exp(m2 - m1)
    p1 = 1.0 / (1.0 + e2)
    p2 = e2 * p1
    sel1 = lane == i1
    sel2 = lane == i2
    onehot = jnp.where(sel1, 1.0, 0.0) + jnp.where(sel2, 1.0, 0.0)
    incl = _dot(tril_ref[...], onehot.astype(BF16))
    counts = incl[ROUTE_ROWS - 1:ROUTE_ROWS]
    seg_units = jnp.floor((counts + (SEG_ALIGN - 1)) * (1.0 / SEG_ALIGN))
    seg_units8 = jnp.broadcast_to(seg_units, (8, LANES)).astype(BF16)
    seg_off = _dot(seg_units8, upper_ref[...])[0:1] * SEG_ALIGN
    pos = incl - onehot + seg_off
    pos1 = jnp.sum(jnp.where(sel1, pos, 0.0), axis=-1, keepdims=True)
    pos2 = jnp.sum(jnp.where(sel2, pos, 0.0), axis=-1, keepdims=True)
    cnt_ref[...] = jnp.broadcast_to(counts, (8, LANES))
    rec = jnp.zeros(logits.shape, F32)
    for lane_id, val in ((INFO_POS, pos1), (INFO_POS + 1, pos2), (INFO_PROB, p1), (INFO_PROB + 1, p2)):
        rec = jnp.where(lane == lane_id, val, rec)
    info_ref[...] = rec


def _route(h_f32, router_w):
    n = h_f32.shape[0]
    tm = ROUTE_ROWS
    w_hi = router_w.astype(BF16)
    w_lo = (router_w - w_hi.astype(F32)).astype(BF16)
    pad = ((0, 0), (0, LANES - N_EXPERTS))
    rw = jnp.concatenate([jnp.pad(w_hi, pad), jnp.pad(w_lo, pad)], axis=1)
    tril = jnp.asarray(np.tril(np.ones((tm, tm), np.float32)), BF16)
    upper = jnp.asarray(np.triu(np.ones((LANES, LANES), np.float32), 1), BF16)
    return pl.pallas_call(
        _route_kernel,
        grid=(n // tm,),
        in_specs=[pl.BlockSpec((tm, D_MODEL), lambda i: (i, 0)),
                  pl.BlockSpec((D_MODEL, 2 * LANES), lambda i: (0, 0)),
                  pl.BlockSpec((tm, tm), lambda i: (0, 0)),
                  pl.BlockSpec((LANES, LANES), lambda i: (0, 0))],
        out_specs=[pl.BlockSpec((tm, D_MODEL), lambda i: (i, 0)),
                   pl.BlockSpec((tm, LANES), lambda i: (i, 0)),
                   pl.BlockSpec((8, LANES), lambda i: (i, 0))],
        out_shape=[jax.ShapeDtypeStruct((n, D_MODEL), BF16),
                   jax.ShapeDtypeStruct((n, LANES), F32),
                   jax.ShapeDtypeStruct((n // tm * 8, LANES), F32)],
        compiler_params=pltpu.CompilerParams(
            dimension_semantics=("arbitrary",), vmem_limit_bytes=VMEM_LIMIT_BYTES),
        name="route",
    )(h_f32, rw, tril, upper)


def _rows(ref, row, n_rows):
    return ref.at[pl.ds(pl.multiple_of(row, SEG_ALIGN), n_rows)]


def _for_each_piece(n_units, fn):
    for bit in reversed(range(PIECE_BITS)):
        covered = (n_units >> (bit + 1)) << (bit + 1)

        @pl.when(((n_units >> bit) & 1) == 1)
        def _():
            fn(covered * SEG_ALIGN, SEG_ALIGN << bit)


def _for_each_segment_piece(seg_ref, n_seg, tile, fn):
    for e in range(N_EXPERTS):
        s = tile * N_EXPERTS + e
        sorted_row = seg_ref[s]
        local_row = seg_ref[n_seg + s]
        _for_each_piece(seg_ref[2 * n_seg + s],
                        lambda first, n_rows: fn(local_row + first, sorted_row + first, n_rows))


def _selection(info, coeff1, coeff2):
    lane = lax.broadcasted_iota(jnp.int32, (info.shape[0], LOCAL_ROWS), 1)
    pos1 = info[:, INFO_POS:INFO_POS + 1].astype(jnp.int32)
    pos2 = info[:, INFO_POS + 1:INFO_POS + 2].astype(jnp.int32)
    return jnp.where(lane == pos1, coeff1, jnp.where(lane == pos2, coeff2, 0.0)).astype(BF16)


def _dispatch_kernel(n_seg, seg_ref, pad_ref, h_ref, info_ref, xs_ref, sorted_scr, zero_scr, sems, pad_sem):
    i = pl.program_id(0)
    n_steps = pl.num_programs(0)
    slot = i % 2

    def move(tile, buf_slot, op):
        def piece(local_row, sorted_row, n_rows):
            cp = pltpu.make_async_copy(_rows(sorted_scr.at[buf_slot], local_row, n_rows),
                                       _rows(xs_ref, sorted_row, n_rows), sems.at[buf_slot])
            op(cp)
        _for_each_segment_piece(seg_ref, n_seg, tile, piece)

    @pl.when(i == 0)
    def _():
        zero_scr[...] = jnp.zeros_like(zero_scr)
        for op in (lambda cp: cp.start(), lambda cp: cp.wait()):
            for e in range(N_EXPERTS):
                lo = pad_ref[e]
                _for_each_piece(pad_ref[N_EXPERTS + e],
                                lambda first, n_rows: op(pltpu.make_async_copy(
                                    zero_scr.at[pl.ds(0, n_rows)], _rows(xs_ref, lo + first, n_rows),
                                    pad_sem)))

        def tile_copy(j):
            return pltpu.make_async_copy(zero_scr, _rows(xs_ref, j * EXPERT_ROWS, EXPERT_ROWS), pad_sem)

        def fill_tile(j, carry):
            tile_copy(j).start()
            return carry

        def drain_tile(j, carry):
            tile_copy(j).wait()
            return carry

        n_tiles = xs_ref.shape[0] // EXPERT_ROWS
        lax.fori_loop(pad_ref[2 * N_EXPERTS], n_tiles, fill_tile, 0)
        lax.fori_loop(pad_ref[2 * N_EXPERTS], n_tiles, drain_tile, 0)

    @pl.when(i >= 2)
    def _():
        move(i - 2, slot, lambda cp: cp.wait())

    sel = _selection(info_ref[...], 1.0, 1.0)
    sorted_scr[slot] = _dot_tn(sel, h_ref[...])
    move(i, slot, lambda cp: cp.start())

    @pl.when(i == n_steps - 1)
    def _():
        @pl.when(i >= 1)
        def _():
            move(i - 1, 1 - slot, lambda cp: cp.wait())
        move(i, slot, lambda cp: cp.wait())


def _dispatch(h2, info, segs, pads, n_sorted):
    n = h2.shape[0]
    tm = ROUTE_ROWS
    n_seg = n // tm * N_EXPERTS
    return pl.pallas_call(
        functools.partial(_dispatch_kernel, n_seg),
        grid_spec=pltpu.PrefetchScalarGridSpec(
            num_scalar_prefetch=2,
            grid=(n // tm,),
            in_specs=[pl.BlockSpec((tm, D_MODEL), lambda i, s, p: (i, 0)),
                      pl.BlockSpec((tm, LANES), lambda i, s, p: (i, 0))],
            out_specs=pl.BlockSpec(memory_space=pl.ANY),
            scratch_shapes=[pltpu.VMEM((2, LOCAL_ROWS, D_MODEL), F32),
                            pltpu.VMEM((EXPERT_ROWS, D_MODEL), F32),
                            pltpu.SemaphoreType.DMA((2,)), pltpu.SemaphoreType.DMA(())]),
        out_shape=jax.ShapeDtypeStruct((n_sorted, D_MODEL), F32),
        compiler_params=pltpu.CompilerParams(
            dimension_semantics=("arbitrary",), vmem_limit_bytes=VMEM_LIMIT_BYTES),
        name="dispatch",
    )(segs, pads, h2, info)


def _expert_kernel(tile_blk, tile_e, n_act, x_ref, w1_ref, w3_ref, w2_ref, y_ref):
    active = pl.program_id(0) < n_act[0]

    @pl.when(active)
    def _():
        h = x_ref[...].astype(BF16)
        a = _dot(h, w1_ref[0])
        u = (a * _sigmoid(a) * _dot(h, w3_ref[0])).astype(BF16)
        y_ref[...] = _dot(u, w2_ref[0])

    @pl.when(jnp.logical_not(active))
    def _():
        y_ref[...] = jnp.zeros_like(y_ref)


def _experts(xs, tile_blk, tile_e, n_act, w1, w3, w2):
    n_sorted = xs.shape[0]
    tm = EXPERT_ROWS
    f = w1.shape[-1]
    return pl.pallas_call(
        _expert_kernel,
        grid_spec=pltpu.PrefetchScalarGridSpec(
            num_scalar_prefetch=3,
            grid=(n_sorted // tm,),
            in_specs=[pl.BlockSpec((tm, D_MODEL), lambda i, b, e, n: (b[i], 0)),
                      pl.BlockSpec((1, D_MODEL, f), lambda i, b, e, n: (e[i], 0, 0)),
                      pl.BlockSpec((1, D_MODEL, f), lambda i, b, e, n: (e[i], 0, 0)),
                      pl.BlockSpec((1, f, D_MODEL), lambda i, b, e, n: (e[i], 0, 0))],
            out_specs=pl.BlockSpec((tm, D_MODEL), lambda i, b, e, n: (i, 0))),
        out_shape=jax.ShapeDtypeStruct((n_sorted, D_MODEL), F32),
        compiler_params=pltpu.CompilerParams(
            dimension_semantics=("arbitrary",), vmem_limit_bytes=VMEM_LIMIT_BYTES),
        name="experts",
    )(tile_blk, tile_e, n_act, xs, w1, w3, w2)


def _combine_kernel(final_norm, n_seg, seg_ref, x_ref, mod_ref, info_ref, fg_ref, ys_ref, out_ref,
                    ybuf, sems):
    i = pl.program_id(0)
    n_steps = pl.num_programs(0)
    slot = i % 2

    def move(tile, buf_slot, op):
        def piece(local_row, sorted_row, n_rows):
            cp = pltpu.make_async_copy(_rows(ys_ref, sorted_row, n_rows),
                                       _rows(ybuf.at[buf_slot], local_row, n_rows), sems.at[buf_slot])
            op(cp)
        _for_each_segment_piece(seg_ref, n_seg, tile, piece)

    @pl.when(i == 0)
    def _():
        ybuf[...] = jnp.zeros_like(ybuf)
        move(0, 0, lambda cp: cp.start())

    @pl.when(i + 1 < n_steps)
    def _():
        move(i + 1, 1 - slot, lambda cp: cp.start())

    move(i, slot, lambda cp: cp.wait())
    info = info_ref[...]
    sel = _selection(info, info[:, INFO_PROB:INFO_PROB + 1], info[:, INFO_PROB + 1:INFO_PROB + 2])
    y = _dot(sel, ybuf[slot].astype(BF16))
    out = x_ref[...] + _row(mod_ref[0, 0], MOD_G2) * y
    if final_norm:
        ms = jnp.mean(out * out, axis=-1, keepdims=True)
        out = out * lax.rsqrt(ms + EPS) * fg_ref[...]
    out_ref[...] = out


def _combine(layer, x2, mod_all, info, ys, segs, final_g, final_norm, rows_per_batch):
    n = x2.shape[0]
    tm = ROUTE_ROWS
    per_b = rows_per_batch // tm
    n_seg = n // tm * N_EXPERTS
    return pl.pallas_call(
        functools.partial(_combine_kernel, final_norm, n_seg),
        grid_spec=pltpu.PrefetchScalarGridSpec(
            num_scalar_prefetch=1,
            grid=(n // tm,),
            in_specs=[pl.BlockSpec((tm, D_MODEL), lambda i, s: (i, 0)),
                      pl.BlockSpec((1, 1, 6, D_MODEL), lambda i, s: (layer, i // per_b, 0, 0)),
                      pl.BlockSpec((tm, LANES), lambda i, s: (i, 0)),
                      pl.BlockSpec((1, D_MODEL), lambda i, s: (0, 0)),
                      pl.BlockSpec(memory_space=pl.ANY)],
            out_specs=pl.BlockSpec((tm, D_MODEL), lambda i, s: (i, 0)),
            scratch_shapes=[pltpu.VMEM((2, LOCAL_ROWS, D_MODEL), F32),
                            pltpu.SemaphoreType.DMA((2,))]),
        out_shape=jax.ShapeDtypeStruct(x2.shape, F32),
        compiler_params=pltpu.CompilerParams(
            dimension_semantics=("arbitrary",), vmem_limit_bytes=VMEM_LIMIT_BYTES),
        name="combine",
    )(segs, x2, mod_all, info, final_g, ys)


def _moe(layer, x, h_f32, mod_all, router_w, w1, w3, w2, final_g, final_norm):
    bsz, seq, _ = x.shape
    n = bsz * seq
    tm = EXPERT_ROWS
    n_tok_tiles = n // ROUTE_ROWS
    x2 = x.reshape(n, D_MODEL)
    h2, info, cnt = _route(h_f32.reshape(n, D_MODEL), router_w)
    counts = cnt.reshape(n_tok_tiles, 8, LANES)[:, 0, :N_EXPERTS].astype(jnp.int32)
    seg_rows = (counts + (SEG_ALIGN - 1)) // SEG_ALIGN * SEG_ALIGN
    group_rows = jnp.sum(seg_rows, axis=0)
    tiles = (group_rows + (tm - 1)) // tm
    tile_end = jnp.cumsum(tiles)
    group_off = (tile_end - tiles) * tm
    seg_sorted = group_off[None, :] + jnp.cumsum(seg_rows, axis=0) - seg_rows
    seg_local = jnp.cumsum(seg_rows, axis=1) - seg_rows
    segs = jnp.concatenate([seg_sorted.reshape(-1), seg_local.reshape(-1),
                            (seg_rows // SEG_ALIGN).reshape(-1)]).astype(jnp.int32)
    max_rows = 2 * n + n_tok_tiles * N_EXPERTS * (SEG_ALIGN - 1)
    n_tiles = -(-max_rows // tm) + N_EXPERTS
    n_act = tile_end[-1]
    t = jnp.minimum(jnp.arange(n_tiles, dtype=jnp.int32), n_act - 1)
    tile_e = jnp.sum(t[:, None] >= tile_end[None, :], axis=1).astype(jnp.int32)
    pad_lo = group_off + group_rows
    pads = jnp.concatenate([pad_lo, (tile_end * tm - pad_lo) // SEG_ALIGN,
                            n_act[None]]).astype(jnp.int32)
    xs = _dispatch(h2, info, segs, pads, n_tiles * tm)
    ys = _experts(xs, t, tile_e, n_act.reshape(1).astype(jnp.int32), w1, w3, w2)
    out = _combine(layer, x2, mod_all, info, ys, segs, final_g, final_norm, seq)
    return out.reshape(x.shape)


def _ffn_kernel(final_norm, n_groups, x_ref, h_ref, mod_ref, w1_ref, w3_ref, w2_ref, fg_ref, y_ref,
                acc_scr):
    e = pl.program_id(2)

    def group_product():
        h = h_ref[0]
        a = _dot(h, w1_ref[...])
        u = (a * _sigmoid(a) * _dot(h, w3_ref[...])).astype(BF16)
        return _dot(u, w2_ref[...])

    assert n_groups == 2

    @pl.when(e == 0)
    def _():
        acc_scr[...] = group_product()

    @pl.when(e == 1)
    def _():
        out = x_ref[0] + _row(mod_ref[0, 0], MOD_G2) * (acc_scr[...] + group_product())
        if final_norm:
            ms = jnp.mean(out * out, axis=-1, keepdims=True)
            out = out * lax.rsqrt(ms + EPS) * fg_ref[...]
        y_ref[0] = out


def _ffn(layer, x, h2, mod_all, w1, w3, w2, final_g, final_norm):
    bsz, seq, _ = x.shape
    n_groups = 2
    f = w1.shape[-1] // n_groups
    tm = 1024
    row = pl.BlockSpec((1, tm, D_MODEL), lambda b, i, e: (b, i, 0))
    return pl.pallas_call(
        functools.partial(_ffn_kernel, final_norm, n_groups),
        grid=(bsz, seq // tm, n_groups),
        in_specs=[row, row,
                  pl.BlockSpec((1, 1, 6, D_MODEL), lambda b, i, e: (layer, b, 0, 0)),
                  pl.BlockSpec((D_MODEL, f), lambda b, i, e: (0, e)),
                  pl.BlockSpec((D_MODEL, f), lambda b, i, e: (0, e)),
                  pl.BlockSpec((f, D_MODEL), lambda b, i, e: (e, 0)),
                  pl.BlockSpec((1, D_MODEL), lambda b, i, e: (0, 0))],
        out_specs=row,
        out_shape=jax.ShapeDtypeStruct(x.shape, F32),
        scratch_shapes=[pltpu.VMEM((tm, D_MODEL), F32)],
        compiler_params=pltpu.CompilerParams(
            dimension_semantics=("arbitrary", "arbitrary", "arbitrary"),
            vmem_limit_bytes=VMEM_LIMIT_BYTES),
        name="ffn",
    )(x, h2, mod_all, w1, w3, w2, final_g)


def kernel(x, c, positions, ada_w, ada_b, norm_mix_g, norm_ffn_g, w_in, gla_w_alpha, gla_b_alpha,
           gla_norm_g, ret_gn_g, ret_gn_b, w_out, ffn_w1, ffn_w3, ffn_w2, router_w, moe_w1,
           moe_w3, moe_w2, final_g):
    bsz = x.shape[0]
    mod_all = _ada_mod(c, ada_w, ada_b).reshape(DEPTH, bsz, 6, D_MODEL)
    cos, sin = _rope_tables(positions)
    lane_pad = PROJ_WIDTH + LANES - w_in.shape[-1]
    w_proj_all = _prep_w_proj(jnp.pad(w_in, ((0, 0), (0, 0), (0, lane_pad))).astype(BF16))
    w_alpha_all = jnp.pad(gla_w_alpha, ((0, 0), (0, LANES - GLA_RANK), (0, 0))).astype(BF16)
    vec = lambda a: a.reshape(a.shape[0], 1, a.shape[1])
    b_alpha_all, gla_g_all, gn_g_all, gn_b_all = map(vec, (gla_b_alpha, gla_norm_g, ret_gn_g, ret_gn_b))
    g_mix_all, g_ffn_all = vec(norm_mix_g), vec(norm_ffn_g)
    w_out_all = w_out.astype(BF16)
    fg = final_g.reshape(1, D_MODEL)
    d_ff = ffn_w1.shape[-1]
    d_fe = moe_w1.shape[-1]
    n_dense, n_moe = ffn_w1.shape[0], moe_w1.shape[0]
    dense_src = (ffn_w1.reshape(n_dense * FFN_CHUNKS, D_MODEL // FFN_CHUNKS, d_ff),
                 ffn_w3.reshape(n_dense * FFN_CHUNKS, D_MODEL // FFN_CHUNKS, d_ff),
                 ffn_w2.reshape(n_dense * FFN_CHUNKS, d_ff // FFN_CHUNKS, D_MODEL))
    moe_chunks = N_EXPERTS * MOE_CHUNKS_PER_EXPERT
    moe_src = (moe_w1.reshape(n_moe * moe_chunks, D_MODEL // MOE_CHUNKS_PER_EXPERT, d_fe),
               moe_w3.reshape(n_moe * moe_chunks, D_MODEL // MOE_CHUNKS_PER_EXPERT, d_fe),
               moe_w2.reshape(n_moe * moe_chunks, d_fe // MOE_CHUNKS_PER_EXPERT, D_MODEL))
    for layer in range(DEPTH):
        li = layer // 2
        dense = layer % 2 == 0
        jobs = [(a, FFN_CHUNKS if dense else moe_chunks, li) for a in (dense_src if dense else moe_src)]
        x, h2, (w1, w3, w2) = _mixer(layer, x, mod_all, g_mix_all, w_proj_all, cos, sin,
                                     w_alpha_all, b_alpha_all, gla_g_all, gn_g_all, gn_b_all, w_out_all,
                                     g_ffn_all, BF16 if dense else F32, jobs)
        last = layer == DEPTH - 1
        if dense:
            x = _ffn(layer, x, h2, mod_all, w1.reshape(D_MODEL, d_ff), w3.reshape(D_MODEL, d_ff),
                     w2.reshape(d_ff, D_MODEL), fg, last)
        else:
            x = _moe(layer, x, h2, mod_all, router_w[li], w1.reshape(N_EXPERTS, D_MODEL, d_fe),
                     w3.reshape(N_EXPERTS, D_MODEL, d_fe), w2.reshape(N_EXPERTS, d_fe, D_MODEL), fg, last)
    return x
```

```python
import functools

import numpy as np
import jax
import jax.numpy as jnp
from jax import lax
from jax.experimental import pallas as pl
from jax.experimental.pallas import tpu as pltpu

F32 = jnp.float32
BF16 = jnp.bfloat16

D_MODEL = 1024
DEPTH = 4
CHUNK = 64
GLA_HEADS = 4
GLA_DK = 64
GLA_DV = 128
GLA_RANK = 16
GLA_TAU = 16.0
RET_HEADS = 4
RET_DK = 128
RET_DV = 128
ROPE_BASE = 10000.0
N_EXPERTS = 8
EPS = 1e-6
GLA_QK = GLA_HEADS * GLA_DK
GLA_V = GLA_HEADS * GLA_DV
RET_QK = RET_HEADS * RET_DK
RET_V = RET_HEADS * RET_DV
MIX_WIDTH = GLA_V + RET_V

LANES = 128
VMEM_LIMIT_BYTES = 56 * 1024 * 1024

PROJ_WIDTH = 2 * GLA_QK + 2 * GLA_V + 2 * RET_QK + 2 * RET_V
OFF_GQ = 0
OFF_GK = OFF_GQ + GLA_QK
OFF_GV = OFF_GK + GLA_QK
OFF_GR = OFF_GV + GLA_V
OFF_RQ = OFF_GR + GLA_V
OFF_RK = OFF_RQ + RET_QK
OFF_RV = OFF_RK + RET_QK
OFF_RG = OFF_RV + RET_V

ATTN_ROWS = 256
ATTN_CHUNKS = ATTN_ROWS // CHUNK

MOD_SH1, MOD_SC1, MOD_G1, MOD_SH2, MOD_SC2, MOD_G2 = range(6)


def _sigmoid(x):
    return 1.0 / (1.0 + jnp.exp(-x))


def _rms_mod(x, g, sc, sh):
    ms = jnp.mean(x * x, axis=-1, keepdims=True)
    return (x * lax.rsqrt(ms + EPS)) * g * (1.0 + sc) + sh


def _dot(a, b):
    return jnp.dot(a, b, preferred_element_type=F32)


def _dot_nt(a, b):
    return lax.dot_general(a, b, (((1,), (1,)), ((), ())), preferred_element_type=F32)


def _dot_tn(a, b):
    return lax.dot_general(a, b, (((0,), (0,)), ((), ())), preferred_element_type=F32)


def _row(m, r):
    return m[r:r + 1]


def _layer_spec(shape, layer):
    zeros = (0,) * (len(shape) - 1)
    return pl.BlockSpec((1,) + tuple(shape[1:]), lambda *_: (layer,) + zeros)


def _ada_kernel(c_ref, w_ref, b_ref, o_ref):
    c = c_ref[...]
    cond = c * _sigmoid(c)
    c_hi = cond.astype(BF16)
    c_lo = (cond - c_hi.astype(F32)).astype(BF16)
    w = w_ref[0]
    w_hi = w.astype(BF16)
    w_lo = (w - w_hi.astype(F32)).astype(BF16)
    o_ref[0] = _dot(c_hi, w_hi) + _dot(c_hi, w_lo) + _dot(c_lo, w_hi) + b_ref[0]


def _ada_mod(c, ada_w, ada_b):
    bsz = c.shape[0]
    tn = 1024
    n_out = ada_w.shape[-1]
    return pl.pallas_call(
        _ada_kernel,
        grid=(DEPTH, n_out // tn),
        in_specs=[
            pl.BlockSpec((bsz, D_MODEL), lambda l, j: (0, 0)),
            pl.BlockSpec((1, D_MODEL, tn), lambda l, j: (l, 0, j)),
            pl.BlockSpec((1, 1, tn), lambda l, j: (l, 0, j)),
        ],
        out_specs=pl.BlockSpec((1, bsz, tn), lambda l, j: (l, 0, j)),
        out_shape=jax.ShapeDtypeStruct((DEPTH, bsz, n_out), F32),
        compiler_params=pltpu.CompilerParams(vmem_limit_bytes=VMEM_LIMIT_BYTES),
        name="ada_mod",
    )(c, ada_w, ada_b.reshape(DEPTH, 1, n_out))


def _rope_kernel(pos_ref, invf_ref, sign_ref, cos_ref, sin_ref):
    ang = pos_ref[...] * invf_ref[...]
    cos_ref[...] = jnp.cos(ang)
    sin_ref[...] = jnp.sin(ang) * sign_ref[...]


def _rope_tables(positions):
    n = positions.size
    half = RET_DK // 2
    inv_freq = ROPE_BASE ** (-jnp.arange(half, dtype=F32) / half)
    invf = jnp.concatenate([inv_freq, inv_freq]).reshape(1, RET_DK)
    sign = jnp.concatenate([-jnp.ones((half,), F32), jnp.ones((half,), F32)]).reshape(1, RET_DK)
    pos = positions.astype(F32).reshape(n, 1)
    tm = 2048
    row = pl.BlockSpec((1, RET_DK), lambda i: (0, 0))
    out = pl.BlockSpec((tm, RET_DK), lambda i: (i, 0))
    return pl.pallas_call(
        _rope_kernel,
        grid=(n // tm,),
        in_specs=[pl.BlockSpec((tm, 1), lambda i: (i, 0)), row, row],
        out_specs=[out, out],
        out_shape=[jax.ShapeDtypeStruct((n, RET_DK), F32)] * 2,
        name="rope_tables",
    )(pos, invf, sign)


GA_LO = 2 * GLA_QK + 2 * GLA_V
OFF_GA = PROJ_WIDTH
W_AFTER = GA_LO + LANES


def _prep_w_proj_kernel(w_ref, o_ref):
    o_ref[0, :, :GA_LO] = w_ref[0, :, :GA_LO]
    gate = w_ref[0, :, GA_LO:W_AFTER]
    lane = lax.broadcasted_iota(jnp.int32, gate.shape, 1)
    o_ref[0, :, GA_LO:W_AFTER] = jnp.where(lane < GLA_RANK, gate, jnp.zeros_like(gate))
    o_ref[0, :, W_AFTER:] = w_ref[0, :, GA_LO + GLA_RANK:GA_LO + GLA_RANK + PROJ_WIDTH - GA_LO]


def _prep_w_proj(w_bf):
    depth, d, width = w_bf.shape
    rows = 256
    return pl.pallas_call(
        _prep_w_proj_kernel,
        grid=(depth, d // rows),
        in_specs=[pl.BlockSpec((1, rows, width), lambda l, i: (l, i, 0))],
        out_specs=pl.BlockSpec((1, rows, PROJ_WIDTH + LANES), lambda l, i: (l, i, 0)),
        out_shape=jax.ShapeDtypeStruct((depth, d, PROJ_WIDTH + LANES), BF16),
        name="prep_w_proj",
    )(w_bf)
PROJ_BLOCK = 256
TAIL_JOBS = 2
FFN_CHUNKS = 16
MOE_CHUNKS_PER_EXPERT = 8


def _attn_consts():
    r = ATTN_ROWS
    t = np.arange(r)
    same = (t[:, None] // CHUNK) == (t[None, :] // CHUNK)
    causal = t[:, None] >= t[None, :]
    tri = (same & causal).astype(np.float32)
    m_fwd = tri
    m_bwd = (same & ~causal).astype(np.float32)
    gam = 1.0 - 2.0 ** (-5.0 - np.arange(RET_HEADS, dtype=np.float64))
    lg = np.log(gam)
    dist = (t[:, None] - t[None, :]).astype(np.float64)
    d_ret = np.where(causal[None], np.exp(lg[:, None, None] * dist[None]),
                     np.where(same[None], np.exp(-lg[:, None, None] * dist[None]), 0.0))
    qdec = np.repeat(np.exp(lg[None, :] * (t[:, None] + 1.0)), RET_DK, axis=1)
    kdec = np.repeat(np.exp(lg[None, :] * (r - 1.0 - t[:, None])), RET_DK, axis=1)
    step_decay = [float(np.exp(lg[h] * r)) for h in range(RET_HEADS)]
    hmask = np.zeros((GLA_HEADS, 1, GLA_QK), np.float32)
    for h in range(GLA_HEADS):
        hmask[h, 0, h * GLA_DK:(h + 1) * GLA_DK] = 1.0
    return dict(tri=tri, m_fwd=m_fwd, m_bwd=m_bwd, d_ret=d_ret.astype(np.float32),
                qdec=qdec.astype(np.float32), kdec=kdec.astype(np.float32),
                step_decay=step_decay, hmask=hmask)


def _mixer_kernel(step_decay, steps_per_seq, n_tiles, n_cast, x_ref, mod_ref, g_ref, w_ref, *rest):
    proj_a, proj_b, o_scr, st_g, st_g_bf, st_r = rest[-6:]
    n_mix_in = len(rest) - 6 - 2 * n_cast - 2
    mix_in = rest[:n_mix_in]
    cast_src = rest[n_mix_in:n_mix_in + n_cast]
    y_ref, h2_ref = rest[n_mix_in + n_cast:n_mix_in + n_cast + 2]
    cast_dst = rest[n_mix_in + n_cast + 2:n_mix_in + 2 * n_cast + 2]
    j = pl.program_id(0)
    last = n_tiles
    bufs = (proj_a, proj_b)

    @pl.when((j - 1) % steps_per_seq == 0)
    def _():
        st_g[...] = jnp.zeros_like(st_g)
        st_g_bf[...] = jnp.zeros_like(st_g_bf)
        st_r[...] = jnp.zeros_like(st_r)

    def step(proj_w, proj_r):
        pending = []
        if proj_w is not None:
            m = mod_ref[0, 0]
            h = _rms_mod(x_ref[0], g_ref[0], _row(m, MOD_SC1), _row(m, MOD_SH1)).astype(BF16)

            def block(w0, out0, width):
                def emit():
                    proj_w[:, out0:out0 + width] = _dot(h, w_ref[0, :, w0:w0 + width])
                return emit

            for src, dst in zip(cast_src, cast_dst):
                dst[...] = src[...].astype(BF16)

            pending = ([block(c0, c0, PROJ_BLOCK) for c0 in range(0, GA_LO, PROJ_BLOCK)]
                       + [block(GA_LO, OFF_GA, LANES)]
                       + [block(W_AFTER + c0, GA_LO + c0, PROJ_BLOCK)
                          for c0 in range(0, PROJ_WIDTH - GA_LO, PROJ_BLOCK)])
        if proj_r is None:
            for job in pending:
                job()
        else:
            _attn_body(step_decay, pending, proj_r, *mix_in, y_ref, h2_ref, o_scr, st_g, st_g_bf, st_r)

    @pl.when(j == 0)
    def _():
        step(bufs[0], None)

    for parity in range(2):
        @pl.when(jnp.logical_and(jnp.logical_and(j > 0, j < last), j % 2 == parity))
        def _():
            step(bufs[parity], bufs[1 - parity])

    @pl.when(j == last)
    def _():
        step(None, bufs[(n_tiles - 1) % 2])


def _attn_body(step_decay, pending, proj_ref, cos_ref, sin_ref, walpha_ref, balpha_ref,
               glag_ref, gng_ref, gnb_ref, tri_ref, mfwd_ref, mbwd_ref, dret_ref,
               qdec_ref, kdec_ref, hmask_ref, xres_ref, modres_ref, wout_ref, gffn_ref,
               y_ref, h2_ref, o_scr, st_g, st_g_bf, st_r):
    r = ATTN_ROWS
    pending = list(pending)

    def interleave(n=1):
        for _ in range(min(n, len(pending) - TAIL_JOBS)):
            pending.pop(0)()

    z = _dot(proj_ref[:, OFF_GA:].astype(BF16), walpha_ref[0]) + balpha_ref[0]
    interleave(2)
    log_a = (jnp.minimum(z, 0.0) - jnp.log(1.0 + jnp.exp(-jnp.abs(z)))) * (1.0 / GLA_TAU)
    hi = log_a.astype(BF16)
    r1 = log_a - hi.astype(F32)
    mid = r1.astype(BF16)
    lo = (r1 - mid.astype(F32)).astype(BF16)
    tri = tri_ref[...]
    b = _dot(tri, hi) + _dot(tri, mid) + _dot(tri, lo)
    interleave(2)
    b3 = b.reshape(ATTN_CHUNKS, CHUNK, GLA_QK)
    b_last = b3[:, CHUNK - 1:CHUNK, :]
    k_upd_scale = jnp.exp(b_last - b3).reshape(r, GLA_QK)
    eb = jnp.exp(b)
    enb = jnp.exp(-b)
    q = proj_ref[:,OFF_GQ:OFF_GQ + GLA_QK] * (GLA_DK ** -0.5)
    k = proj_ref[:,OFF_GK:OFF_GK + GLA_QK]
    v_bf = proj_ref[:,OFF_GV:OFF_GV + GLA_V].astype(BF16)
    q_f = q * eb
    q_b = q * enb
    k_f = (k * enb).astype(BF16)
    k_b = (k * eb).astype(BF16)
    k_u = (k * k_upd_scale).astype(BF16)
    q_f_bf = q_f.astype(BF16)

    b_t = b.T
    heads_per_group = LANES // GLA_DK
    inter = []
    for g in range(ATTN_CHUNKS):
        rows = slice(g * CHUNK, (g + 1) * CHUNK)
        inter.append(_dot(q_f_bf[rows], st_g_bf[...]))
        decay = jnp.exp(b_t[:, (g + 1) * CHUNK - 1:(g + 1) * CHUNK])
        for grp in range(GLA_QK // LANES):
            v_lanes = slice(grp * heads_per_group * GLA_DV, (grp + 1) * heads_per_group * GLA_DV)
            upd = _dot_tn(k_u[rows, grp * LANES:(grp + 1) * LANES], v_bf[rows, v_lanes])
            for i in range(heads_per_group):
                h = grp * heads_per_group + i
                keys = slice(h * GLA_DK, (h + 1) * GLA_DK)
                st = (st_g[h] * decay[keys]
                      + upd[i * GLA_DK:(i + 1) * GLA_DK, i * GLA_DV:(i + 1) * GLA_DV])
                st_g[h] = st
                st_g_bf[keys, h * GLA_DV:(h + 1) * GLA_DV] = st.astype(BF16)
        interleave()
    o_inter = jnp.concatenate(inter, axis=0)

    m_fwd = mfwd_ref[...] > 0.5
    m_bwd = mbwd_ref[...] > 0.5
    for h in range(GLA_HEADS):
        hm = hmask_ref[h]
        s_f = _dot_nt((q_f * hm).astype(BF16), k_f)
        s_b = _dot_nt((q_b * hm).astype(BF16), k_b)
        sc = jnp.where(m_fwd, s_f, jnp.where(m_bwd, s_b, 0.0)).astype(BF16)
        lanes = slice(h * GLA_DV, (h + 1) * GLA_DV)
        o_h = _dot(sc, v_bf[:, lanes]) + o_inter[:, lanes]
        interleave()
        o_h = o_h * lax.rsqrt(jnp.mean(o_h * o_h, axis=-1, keepdims=True) + EPS)
        gate = proj_ref[:,OFF_GR + h * GLA_DV:OFF_GR + (h + 1) * GLA_DV]
        o_h = o_h * glag_ref[0, :, lanes] * (gate * _sigmoid(gate))
        o_scr[:, lanes] = o_h.astype(BF16)

    cos = cos_ref[0]
    sin = sin_ref[0]
    for h in range(RET_HEADS):
        lanes = slice(h * RET_DK, (h + 1) * RET_DK)
        qh = proj_ref[:,OFF_RQ + h * RET_DK:OFF_RQ + (h + 1) * RET_DK]
        kh = proj_ref[:,OFF_RK + h * RET_DK:OFF_RK + (h + 1) * RET_DK]
        vh = proj_ref[:,OFF_RV + h * RET_DV:OFF_RV + (h + 1) * RET_DV].astype(BF16)
        qh = (qh * cos + pltpu.roll(qh, RET_DK // 2, axis=1) * sin) * (RET_DK ** -0.5)
        kh = kh * cos + pltpu.roll(kh, RET_DK // 2, axis=1) * sin
        s = _dot_nt(qh.astype(BF16), kh.astype(BF16)) * dret_ref[h]
        st = st_r[h]
        o_h = _dot(s.astype(BF16), vh) + _dot((qh * qdec_ref[:, lanes]).astype(BF16), st.astype(BF16))
        st_r[h] = st * step_decay[h] + _dot_tn((kh * kdec_ref[:, lanes]).astype(BF16), vh)
        interleave()
        mu =jnp.mean(o_h, axis=-1, keepdims=True)
        d = o_h - mu
        var = jnp.mean(d * d, axis=-1, keepdims=True)
        o_h = d * lax.rsqrt(var + EPS) * gng_ref[0, :, lanes] + gnb_ref[0, :, lanes]
        gate = proj_ref[:,OFF_RG + h * RET_DV:OFF_RG + (h + 1) * RET_DV]
        o_h = o_h * (gate * _sigmoid(gate))
        o_scr[:, GLA_V + h * RET_DV:GLA_V + (h + 1) * RET_DV] = o_h.astype(BF16)
    interleave(len(pending))
    m = modres_ref[0, 0]
    y = xres_ref[0] + _row(m, MOD_G1) * _dot(o_scr[...], wout_ref[0])
    for job in pending:
        job()
    y_ref[0] = y
    h2_ref[0] = _rms_mod(y, gffn_ref[0], _row(m, MOD_SC2), _row(m, MOD_SH2)).astype(h2_ref.dtype)


def _mixer(layer, x, mod_all, g_all, w_proj_all, cos, sin, w_alpha_all, b_alpha_all,
           gla_g_all, gn_g_all, gn_b_all, w_out_all, g_ffn_all, h2_dtype, cast_jobs):
    bsz, seq, _ = x.shape
    r = ATTN_ROWS
    steps_per_seq = seq // r
    n_tiles = bsz * steps_per_seq
    c = _attn_consts()
    const2 = lambda shape: pl.BlockSpec(shape, lambda j: (0,) * len(shape))
    per_layer = lambda a: _layer_spec(a.shape, layer)
    proj_tile = lambda j: jnp.minimum(j, n_tiles - 1)
    mix_tile = lambda j: jnp.maximum(j - 1, 0)
    mix_blk = lambda w: pl.BlockSpec((1, r, w), lambda j: (mix_tile(j), 0, 0))

    def chunk_spec(a, chunks, first):
        return pl.BlockSpec((1,) + a.shape[1:], lambda j: (first + jnp.minimum(j, chunks - 1), 0, 0))

    cast_in = [chunk_spec(a, chunks, li * chunks) for a, chunks, li in cast_jobs]
    cast_out = [chunk_spec(a, chunks, 0) for a, chunks, li in cast_jobs]
    cast_shapes = [jax.ShapeDtypeStruct((chunks,) + a.shape[1:], BF16) for a, chunks, li in cast_jobs]
    out = pl.pallas_call(
        functools.partial(_mixer_kernel, c["step_decay"], steps_per_seq, n_tiles, len(cast_jobs)),
        grid=(n_tiles + 1,),
        in_specs=[
            pl.BlockSpec((1, r, D_MODEL), lambda j: (proj_tile(j), 0, 0)),
            pl.BlockSpec((1, 1, 6, D_MODEL), lambda j: (layer, proj_tile(j) // steps_per_seq, 0, 0)),
            per_layer(g_all), per_layer(w_proj_all),
            mix_blk(RET_DK), mix_blk(RET_DK),
            per_layer(w_alpha_all), per_layer(b_alpha_all),
            per_layer(gla_g_all), per_layer(gn_g_all), per_layer(gn_b_all),
            const2((r, r)), const2((r, r)), const2((r, r)), const2((RET_HEADS, r, r)),
            const2((r, RET_QK)), const2((r, RET_QK)),
            const2((GLA_HEADS, 1, GLA_QK)),
            mix_blk(D_MODEL),
            pl.BlockSpec((1, 1, 6, D_MODEL), lambda j: (layer, mix_tile(j) // steps_per_seq, 0, 0)),
            per_layer(w_out_all), per_layer(g_ffn_all),
        ] + cast_in,
        out_specs=[mix_blk(D_MODEL), mix_blk(D_MODEL)] + cast_out,
        out_shape=[jax.ShapeDtypeStruct((n_tiles, r, D_MODEL), F32),
                   jax.ShapeDtypeStruct((n_tiles, r, D_MODEL), h2_dtype)] + cast_shapes,
        scratch_shapes=[pltpu.VMEM((r, PROJ_WIDTH + LANES), F32),
                        pltpu.VMEM((r, PROJ_WIDTH + LANES), F32),
                        pltpu.VMEM((r, MIX_WIDTH), BF16),
                        pltpu.VMEM((GLA_HEADS, GLA_DK, GLA_DV), F32),
                        pltpu.VMEM((GLA_QK, GLA_V), BF16),
                        pltpu.VMEM((RET_HEADS, RET_DK, RET_DV), F32)],
        compiler_params=pltpu.CompilerParams(
            dimension_semantics=("arbitrary",), vmem_limit_bytes=VMEM_LIMIT_BYTES),
        name="mixer",
    )(x.reshape(n_tiles, r, D_MODEL), mod_all, g_all, w_proj_all,
      cos.reshape(n_tiles, r, RET_DK), sin.reshape(n_tiles, r, RET_DK),
      w_alpha_all, b_alpha_all, gla_g_all, gn_g_all, gn_b_all,
      jnp.asarray(c["tri"], BF16), jnp.asarray(c["m_fwd"]), jnp.asarray(c["m_bwd"]),
      jnp.asarray(c["d_ret"]), jnp.asarray(c["qdec"]), jnp.asarray(c["kdec"]),
      jnp.asarray(c["hmask"]),
      x.reshape(n_tiles, r, D_MODEL), mod_all, w_out_all, g_ffn_all, *[a for a, _, _ in cast_jobs])
    return out[0].reshape(x.shape), out[1].reshape(x.shape), out[2:]


ROUTE_ROWS = 512
EXPERT_ROWS = 512
SEG_ALIGN = 8
LOCAL_ROWS = -(-(2 * ROUTE_ROWS + N_EXPERTS * (SEG_ALIGN - 1)) // LANES) * LANES
PIECE_BITS = (max(ROUTE_ROWS, EXPERT_ROWS) // SEG_ALIGN).bit_length()
INFO_POS, INFO_PROB = 0, 2


def _route_kernel(hin_ref, rw_ref, tril_ref, upper_ref, h_ref, info_ref, cnt_ref):
    h = hin_ref[...]
    h_hi = h.astype(BF16)
    h_ref[...] = h_hi
    h_lo = (h - h_hi.astype(F32)).astype(BF16)
    hh = _dot(h_hi, rw_ref[...])
    logits = hh[:, :LANES] + hh[:, LANES:] + _dot(h_lo, rw_ref[:, :LANES])
    lane = lax.broadcasted_iota(jnp.int32, logits.shape, 1)
    neg = jnp.float32(-jnp.inf)
    lg = jnp.where(lane < N_EXPERTS, logits, neg)
    m1 = jnp.max(lg, axis=-1, keepdims=True)
    i1 = jnp.min(jnp.where(lg == m1, lane, LANES), axis=-1, keepdims=True)
    lg2 = jnp.where(lane == i1, neg, lg)
    m2 = jnp.max(lg2, axis=-1, keepdims=True)
    i2 = jnp.min(jnp.where(lg2 == m2, lane, LANES), axis=-1, keepdims=True)
    e2 = jnp.exp(m2 - m1)
    p1 = 1.0 / (1.0 + e2)
    p2 = e2 * p1
    sel1 = lane == i1
    sel2 = lane == i2
    onehot = jnp.where(sel1, 1.0, 0.0) + jnp.where(sel2, 1.0, 0.0)
    incl = _dot(tril_ref[...], onehot.astype(BF16))
    counts = incl[ROUTE_ROWS - 1:ROUTE_ROWS]
    seg_units = jnp.floor((counts + (SEG_ALIGN - 1)) * (1.0 / SEG_ALIGN))
    seg_units8 = jnp.broadcast_to(seg_units, (8, LANES)).astype(BF16)
    seg_off = _dot(seg_units8, upper_ref[...])[0:1] * SEG_ALIGN
    pos = incl - onehot + seg_off
    pos1 = jnp.sum(jnp.where(sel1, pos, 0.0), axis=-1, keepdims=True)
    pos2 = jnp.sum(jnp.where(sel2, pos, 0.0), axis=-1, keepdims=True)
    cnt_ref[...] = jnp.broadcast_to(counts, (8, LANES))
    rec = jnp.zeros(logits.shape, F32)
    for lane_id, val in ((INFO_POS, pos1), (INFO_POS + 1, pos2), (INFO_PROB, p1), (INFO_PROB + 1, p2)):
        rec = jnp.where(lane == lane_id, val, rec)
    info_ref[...] = rec


def _route(h_f32, router_w):
    n = h_f32.shape[0]
    tm = ROUTE_ROWS
    w_hi = router_w.astype(BF16)
    w_lo = (router_w - w_hi.astype(F32)).astype(BF16)
    pad = ((0, 0), (0, LANES - N_EXPERTS))
    rw = jnp.concatenate([jnp.pad(w_hi, pad), jnp.pad(w_lo, pad)], axis=1)
    tril = jnp.asarray(np.tril(np.ones((tm, tm), np.float32)), BF16)
    upper = jnp.asarray(np.triu(np.ones((LANES, LANES), np.float32), 1), BF16)
    return pl.pallas_call(
        _route_kernel,
        grid=(n // tm,),
        in_specs=[pl.BlockSpec((tm, D_MODEL), lambda i: (i, 0)),
                  pl.BlockSpec((D_MODEL, 2 * LANES), lambda i: (0, 0)),
                  pl.BlockSpec((tm, tm), lambda i: (0, 0)),
                  pl.BlockSpec((LANES, LANES), lambda i: (0, 0))],
        out_specs=[pl.BlockSpec((tm, D_MODEL), lambda i: (i, 0)),
                   pl.BlockSpec((tm, LANES), lambda i: (i, 0)),
                   pl.BlockSpec((8, LANES), lambda i: (i, 0))],
        out_shape=[jax.ShapeDtypeStruct((n, D_MODEL), BF16),
                   jax.ShapeDtypeStruct((n, LANES), F32),
                   jax.ShapeDtypeStruct((n // tm * 8, LANES), F32)],
        compiler_params=pltpu.CompilerParams(
            dimension_semantics=("arbitrary",), vmem_limit_bytes=VMEM_LIMIT_BYTES),
        name="route",
    )(h_f32, rw, tril, upper)


def _rows(ref, row, n_rows):
    return ref.at[pl.ds(pl.multiple_of(row, SEG_ALIGN), n_rows)]


def _for_each_piece(n_units, fn):
    for bit in reversed(range(PIECE_BITS)):
        covered = (n_units >> (bit + 1)) << (bit + 1)

        @pl.when(((n_units >> bit) & 1) == 1)
        def _():
            fn(covered * SEG_ALIGN, SEG_ALIGN << bit)


def _for_each_segment_piece(seg_ref, n_seg, tile, fn):
    for e in range(N_EXPERTS):
        s = tile * N_EXPERTS + e
        sorted_row = seg_ref[s]
        local_row = seg_ref[n_seg + s]
        _for_each_piece(seg_ref[2 * n_seg + s],
                        lambda first, n_rows: fn(local_row + first, sorted_row + first, n_rows))


def _selection(info, coeff1, coeff2):
    lane = lax.broadcasted_iota(jnp.int32, (info.shape[0], LOCAL_ROWS), 1)
    pos1 = info[:, INFO_POS:INFO_POS + 1].astype(jnp.int32)
    pos2 = info[:, INFO_POS + 1:INFO_POS + 2].astype(jnp.int32)
    return jnp.where(lane == pos1, coeff1, jnp.where(lane == pos2, coeff2, 0.0)).astype(BF16)


def _dispatch_kernel(n_seg, seg_ref, pad_ref, h_ref, info_ref, xs_ref, sorted_scr, zero_scr, sems, pad_sem):
    i = pl.program_id(0)
    n_steps = pl.num_programs(0)
    slot = i % 2

    def move(tile, buf_slot, op):
        def piece(local_row, sorted_row, n_rows):
            cp = pltpu.make_async_copy(_rows(sorted_scr.at[buf_slot], local_row, n_rows),
                                       _rows(xs_ref, sorted_row, n_rows), sems.at[buf_slot])
            op(cp)
        _for_each_segment_piece(seg_ref, n_seg, tile, piece)

    @pl.when(i == 0)
    def _():
        zero_scr[...] = jnp.zeros_like(zero_scr)
        for op in (lambda cp: cp.start(), lambda cp: cp.wait()):
            for e in range(N_EXPERTS):
                lo = pad_ref[e]
                _for_each_piece(pad_ref[N_EXPERTS + e],
                                lambda first, n_rows: op(pltpu.make_async_copy(
                                    zero_scr.at[pl.ds(0, n_rows)], _rows(xs_ref, lo + first, n_rows),
                                    pad_sem)))

        def tile_copy(j):
            return pltpu.make_async_copy(zero_scr, _rows(xs_ref, j * EXPERT_ROWS, EXPERT_ROWS), pad_sem)

        def fill_tile(j, carry):
            tile_copy(j).start()
            return carry

        def drain_tile(j, carry):
            tile_copy(j).wait()
            return carry

        n_tiles = xs_ref.shape[0] // EXPERT_ROWS
        lax.fori_loop(pad_ref[2 * N_EXPERTS], n_tiles, fill_tile, 0)
        lax.fori_loop(pad_ref[2 * N_EXPERTS], n_tiles, drain_tile, 0)

    @pl.when(i >= 2)
    def _():
        move(i - 2, slot, lambda cp: cp.wait())

    sel = _selection(info_ref[...], 1.0, 1.0)
    sorted_scr[slot] = _dot_tn(sel, h_ref[...])
    move(i, slot, lambda cp: cp.start())

    @pl.when(i == n_steps - 1)
    def _():
        @pl.when(i >= 1)
        def _():
            move(i - 1, 1 - slot, lambda cp: cp.wait())
        move(i, slot, lambda cp: cp.wait())


def _dispatch(h2, info, segs, pads, n_sorted):
    n = h2.shape[0]
    tm = ROUTE_ROWS
    n_seg = n // tm * N_EXPERTS
    return pl.pallas_call(
        functools.partial(_dispatch_kernel, n_seg),
        grid_spec=pltpu.PrefetchScalarGridSpec(
            num_scalar_prefetch=2,
            grid=(n // tm,),
            in_specs=[pl.BlockSpec((tm, D_MODEL), lambda i, s, p: (i, 0)),
                      pl.BlockSpec((tm, LANES), lambda i, s, p: (i, 0))],
            out_specs=pl.BlockSpec(memory_space=pl.ANY),
            scratch_shapes=[pltpu.VMEM((2, LOCAL_ROWS, D_MODEL), F32),
                            pltpu.VMEM((EXPERT_ROWS, D_MODEL), F32),
                            pltpu.SemaphoreType.DMA((2,)), pltpu.SemaphoreType.DMA(())]),
        out_shape=jax.ShapeDtypeStruct((n_sorted, D_MODEL), F32),
        compiler_params=pltpu.CompilerParams(
            dimension_semantics=("arbitrary",), vmem_limit_bytes=VMEM_LIMIT_BYTES),
        name="dispatch",
    )(segs, pads, h2, info)


def _expert_kernel(tile_blk, tile_e, n_act, x_ref, w1_ref, w3_ref, w2_ref, y_ref):
    active = pl.program_id(0) < n_act[0]

    @pl.when(active)
    def _():
        h = x_ref[...].astype(BF16)
        a = _dot(h, w1_ref[0])
        u = (a * _sigmoid(a) * _dot(h, w3_ref[0])).astype(BF16)
        y_ref[...] = _dot(u, w2_ref[0])

    @pl.when(jnp.logical_not(active))
    def _():
        y_ref[...] = jnp.zeros_like(y_ref)


def _experts(xs, tile_blk, tile_e, n_act, w1, w3, w2):
    n_sorted = xs.shape[0]
    tm = EXPERT_ROWS
    f = w1.shape[-1]
    return pl.pallas_call(
        _expert_kernel,
        grid_spec=pltpu.PrefetchScalarGridSpec(
            num_scalar_prefetch=3,
            grid=(n_sorted // tm,),
            in_specs=[pl.BlockSpec((tm, D_MODEL), lambda i, b, e, n: (b[i], 0)),
                      pl.BlockSpec((1, D_MODEL, f), lambda i, b, e, n: (e[i], 0, 0)),
                      pl.BlockSpec((1, D_MODEL, f), lambda i, b, e, n: (e[i], 0, 0)),
                      pl.BlockSpec((1, f, D_MODEL), lambda i, b, e, n: (e[i], 0, 0))],
            out_specs=pl.BlockSpec((tm, D_MODEL), lambda i, b, e, n: (i, 0))),
        out_shape=jax.ShapeDtypeStruct((n_sorted, D_MODEL), F32),
        compiler_params=pltpu.CompilerParams(
            dimension_semantics=("arbitrary",), vmem_limit_bytes=VMEM_LIMIT_BYTES),
        name="experts",
    )(tile_blk, tile_e, n_act, xs, w1, w3, w2)


def _combine_kernel(final_norm, n_seg, seg_ref, x_ref, mod_ref, info_ref, fg_ref, ys_ref, out_ref,
                    ybuf, sems):
    i = pl.program_id(0)
    n_steps = pl.num_programs(0)
    slot = i % 2

    def move(tile, buf_slot, op):
        def piece(local_row, sorted_row, n_rows):
            cp = pltpu.make_async_copy(_rows(ys_ref, sorted_row, n_rows),
                                       _rows(ybuf.at[buf_slot], local_row, n_rows), sems.at[buf_slot])
            op(cp)
        _for_each_segment_piece(seg_ref, n_seg, tile, piece)

    @pl.when(i == 0)
    def _():
        ybuf[...] = jnp.zeros_like(ybuf)
        move(0, 0, lambda cp: cp.start())

    @pl.when(i + 1 < n_steps)
    def _():
        move(i + 1, 1 - slot, lambda cp: cp.start())

    move(i, slot, lambda cp: cp.wait())
    info = info_ref[...]
    sel = _selection(info, info[:, INFO_PROB:INFO_PROB + 1], info[:, INFO_PROB + 1:INFO_PROB + 2])
    y = _dot(sel, ybuf[slot].astype(BF16))
    out = x_ref[...] + _row(mod_ref[0, 0], MOD_G2) * y
    if final_norm:
        ms = jnp.mean(out * out, axis=-1, keepdims=True)
        out = out * lax.rsqrt(ms + EPS) * fg_ref[...]
    out_ref[...] = out


def _combine(layer, x2, mod_all, info, ys, segs, final_g, final_norm, rows_per_batch):
    n = x2.shape[0]
    tm = ROUTE_ROWS
    per_b = rows_per_batch // tm
    n_seg = n // tm * N_EXPERTS
    return pl.pallas_call(
        functools.partial(_combine_kernel, final_norm, n_seg),
        grid_spec=pltpu.PrefetchScalarGridSpec(
            num_scalar_prefetch=1,
            grid=(n // tm,),
            in_specs=[pl.BlockSpec((tm, D_MODEL), lambda i, s: (i, 0)),
                      pl.BlockSpec((1, 1, 6, D_MODEL), lambda i, s: (layer, i // per_b, 0, 0)),
                      pl.BlockSpec((tm, LANES), lambda i, s: (i, 0)),
                      pl.BlockSpec((1, D_MODEL), lambda i, s: (0, 0)),
                      pl.BlockSpec(memory_space=pl.ANY)],
            out_specs=pl.BlockSpec((tm, D_MODEL), lambda i, s: (i, 0)),
            scratch_shapes=[pltpu.VMEM((2, LOCAL_ROWS, D_MODEL), F32),
                            pltpu.SemaphoreType.DMA((2,))]),
        out_shape=jax.ShapeDtypeStruct(x2.shape, F32),
        compiler_params=pltpu.CompilerParams(
            dimension_semantics=("arbitrary",), vmem_limit_bytes=VMEM_LIMIT_BYTES),
        name="combine",
    )(segs, x2, mod_all, info, final_g, ys)


def _moe(layer, x, h_f32, mod_all, router_w, w1, w3, w2, final_g, final_norm):
    bsz, seq, _ = x.shape
    n = bsz * seq
    tm = EXPERT_ROWS
    n_tok_tiles = n // ROUTE_ROWS
    x2 = x.reshape(n, D_MODEL)
    h2, info, cnt = _route(h_f32.reshape(n, D_MODEL), router_w)
    counts = cnt.reshape(n_tok_tiles, 8, LANES)[:, 0, :N_EXPERTS].astype(jnp.int32)
    seg_rows = (counts + (SEG_ALIGN - 1)) // SEG_ALIGN * SEG_ALIGN
    group_rows = jnp.sum(seg_rows, axis=0)
    tiles = (group_rows + (tm - 1)) // tm
    tile_end = jnp.cumsum(tiles)
    group_off = (tile_end - tiles) * tm
    seg_sorted = group_off[None, :] + jnp.cumsum(seg_rows, axis=0) - seg_rows
    seg_local = jnp.cumsum(seg_rows, axis=1) - seg_rows
    segs = jnp.concatenate([seg_sorted.reshape(-1), seg_local.reshape(-1),
                            (seg_rows // SEG_ALIGN).reshape(-1)]).astype(jnp.int32)
    max_rows = 2 * n + n_tok_tiles * N_EXPERTS * (SEG_ALIGN - 1)
    n_tiles = -(-max_rows // tm) + N_EXPERTS
    n_act = tile_end[-1]
    t = jnp.minimum(jnp.arange(n_tiles, dtype=jnp.int32), n_act - 1)
    tile_e = jnp.sum(t[:, None] >= tile_end[None, :], axis=1).astype(jnp.int32)
    pad_lo = group_off + group_rows
    pads = jnp.concatenate([pad_lo, (tile_end * tm - pad_lo) // SEG_ALIGN,
                            n_act[None]]).astype(jnp.int32)
    xs = _dispatch(h2, info, segs, pads, n_tiles * tm)
    ys = _experts(xs, t, tile_e, n_act.reshape(1).astype(jnp.int32), w1, w3, w2)
    out = _combine(layer, x2, mod_all, info, ys, segs, final_g, final_norm, seq)
    return out.reshape(x.shape)


def _ffn_kernel(final_norm, n_groups, x_ref, h_ref, mod_ref, w1_ref, w3_ref, w2_ref, fg_ref, y_ref,
                acc_scr):
    e = pl.program_id(2)

    def group_product():
        h = h_ref[0]
        a = _dot(h, w1_ref[...])
        u = (a * _sigmoid(a) * _dot(h, w3_ref[...])).astype(BF16)
        return _dot(u, w2_ref[...])

    assert n_groups == 2

    @pl.when(e == 0)
    def _():
        acc_scr[...] = group_product()

    @pl.when(e == 1)
    def _():
        out = x_ref[0] + _row(mod_ref[0, 0], MOD_G2) * (acc_scr[...] + group_product())
        if final_norm:
            ms = jnp.mean(out * out, axis=-1, keepdims=True)
            out = out * lax.rsqrt(ms + EPS) * fg_ref[...]
        y_ref[0] = out


def _ffn(layer, x, h2, mod_all, w1, w3, w2, final_g, final_norm):
    bsz, seq, _ = x.shape
    n_groups = 2
    f = w1.shape[-1] // n_groups
    tm = 1024
    row = pl.BlockSpec((1, tm, D_MODEL), lambda b, i, e: (b, i, 0))
    return pl.pallas_call(
        functools.partial(_ffn_kernel, final_norm, n_groups),
        grid=(bsz, seq // tm, n_groups),
        in_specs=[row, row,
                  pl.BlockSpec((1, 1, 6, D_MODEL), lambda b, i, e: (layer, b, 0, 0)),
                  pl.BlockSpec((D_MODEL, f), lambda b, i, e: (0, e)),
                  pl.BlockSpec((D_MODEL, f), lambda b, i, e: (0, e)),
                  pl.BlockSpec((f, D_MODEL), lambda b, i, e: (e, 0)),
                  pl.BlockSpec((1, D_MODEL), lambda b, i, e: (0, 0))],
        out_specs=row,
        out_shape=jax.ShapeDtypeStruct(x.shape, F32),
        scratch_shapes=[pltpu.VMEM((tm, D_MODEL), F32)],
        compiler_params=pltpu.CompilerParams(
            dimension_semantics=("arbitrary", "arbitrary", "arbitrary"),
            vmem_limit_bytes=VMEM_LIMIT_BYTES),
        name="ffn",
    )(x, h2, mod_all, w1, w3, w2, final_g)


def kernel(x, c, positions, ada_w, ada_b, norm_mix_g, norm_ffn_g, w_in, gla_w_alpha, gla_b_alpha,
           gla_norm_g, ret_gn_g, ret_gn_b, w_out, ffn_w1, ffn_w3, ffn_w2, router_w, moe_w1,
           moe_w3, moe_w2, final_g):
    bsz = x.shape[0]
    mod_all = _ada_mod(c, ada_w, ada_b).reshape(DEPTH, bsz, 6, D_MODEL)
    cos, sin = _rope_tables(positions)
    lane_pad = jnp.zeros(w_in.shape[:2] + (PROJ_WIDTH + LANES - w_in.shape[-1],), BF16)
    w_proj_all = _prep_w_proj(jnp.concatenate([w_in.astype(BF16), lane_pad], axis=2))
    w_alpha_all = jnp.pad(gla_w_alpha, ((0, 0), (0, LANES - GLA_RANK), (0, 0))).astype(BF16)
    vec = lambda a: a.reshape(a.shape[0], 1, a.shape[1])
    b_alpha_all, gla_g_all, gn_g_all, gn_b_all = map(vec, (gla_b_alpha, gla_norm_g, ret_gn_g, ret_gn_b))
    g_mix_all, g_ffn_all = vec(norm_mix_g), vec(norm_ffn_g)
    w_out_all = w_out.astype(BF16)
    fg = final_g.reshape(1, D_MODEL)
    d_ff = ffn_w1.shape[-1]
    d_fe = moe_w1.shape[-1]
    n_dense, n_moe = ffn_w1.shape[0], moe_w1.shape[0]
    dense_src = (ffn_w1.reshape(n_dense * FFN_CHUNKS, D_MODEL // FFN_CHUNKS, d_ff),
                 ffn_w3.reshape(n_dense * FFN_CHUNKS, D_MODEL // FFN_CHUNKS, d_ff),
                 ffn_w2.reshape(n_dense * FFN_CHUNKS, d_ff // FFN_CHUNKS, D_MODEL))
    moe_chunks = N_EXPERTS * MOE_CHUNKS_PER_EXPERT
    moe_src = (moe_w1.reshape(n_moe * moe_chunks, D_MODEL // MOE_CHUNKS_PER_EXPERT, d_fe),
               moe_w3.reshape(n_moe * moe_chunks, D_MODEL // MOE_CHUNKS_PER_EXPERT, d_fe),
               moe_w2.reshape(n_moe * moe_chunks, d_fe // MOE_CHUNKS_PER_EXPERT, D_MODEL))
    for layer in range(DEPTH):
        li = layer // 2
        dense = layer % 2 == 0
        jobs = [(a, FFN_CHUNKS if dense else moe_chunks, li) for a in (dense_src if dense else moe_src)]
        x, h2, (w1, w3, w2) = _mixer(layer, x, mod_all, g_mix_all, w_proj_all, cos, sin,
                                     w_alpha_all, b_alpha_all, gla_g_all, gn_g_all, gn_b_all, w_out_all,
                                     g_ffn_all, BF16 if dense else F32, jobs)
        last = layer == DEPTH - 1
        if dense:
            x = _ffn(layer, x, h2, mod_all, w1.reshape(D_MODEL, d_ff), w3.reshape(D_MODEL, d_ff),
                     w2.reshape(d_ff, D_MODEL), fg, last)
        else:
            x = _moe(layer, x, h2, mod_all, router_w[li], w1.reshape(N_EXPERTS, D_MODEL, d_fe),
                     w3.reshape(N_EXPERTS, D_MODEL, d_fe), w2.reshape(N_EXPERTS, d_fe, D_MODEL), fg, last)
    return x
```

```python
import functools

import numpy as np
import jax
import jax.numpy as jnp
from jax import lax
from jax.experimental import pallas as pl
from jax.experimental.pallas import tpu as pltpu

F32 = jnp.float32
BF16 = jnp.bfloat16

D_MODEL = 1024
DEPTH = 4
CHUNK = 64
GLA_HEADS = 4
GLA_DK = 64
GLA_DV = 128
GLA_RANK = 16
GLA_TAU = 16.0
RET_HEADS = 4
RET_DK = 128
RET_DV = 128
ROPE_BASE = 10000.0
N_EXPERTS = 8
EPS = 1e-6
GLA_QK = GLA_HEADS * GLA_DK
GLA_V = GLA_HEADS * GLA_DV
RET_QK = RET_HEADS * RET_DK
RET_V = RET_HEADS * RET_DV
MIX_WIDTH = GLA_V + RET_V

LANES = 128
VMEM_LIMIT_BYTES = 56 * 1024 * 1024

PROJ_WIDTH = 2 * GLA_QK + 2 * GLA_V + 2 * RET_QK + 2 * RET_V
OFF_GQ = 0
OFF_GK = OFF_GQ + GLA_QK
OFF_GV = OFF_GK + GLA_QK
OFF_GR = OFF_GV + GLA_V
OFF_RQ = OFF_GR + GLA_V
OFF_RK = OFF_RQ + RET_QK
OFF_RV = OFF_RK + RET_QK
OFF_RG = OFF_RV + RET_V

ATTN_ROWS = 256
ATTN_CHUNKS = ATTN_ROWS // CHUNK

MOD_SH1, MOD_SC1, MOD_G1, MOD_SH2, MOD_SC2, MOD_G2 = range(6)


def _sigmoid(x):
    return 1.0 / (1.0 + jnp.exp(-x))


def _rms_mod(x, g, sc, sh):
    ms = jnp.mean(x * x, axis=-1, keepdims=True)
    return (x * lax.rsqrt(ms + EPS)) * g * (1.0 + sc) + sh


def _dot(a, b):
    return jnp.dot(a, b, preferred_element_type=F32)


def _dot_nt(a, b):
    return lax.dot_general(a, b, (((1,), (1,)), ((), ())), preferred_element_type=F32)


def _dot_tn(a, b):
    return lax.dot_general(a, b, (((0,), (0,)), ((), ())), preferred_element_type=F32)


def _row(m, r):
    return m[r:r + 1]


def _layer_spec(shape, layer):
    zeros = (0,) * (len(shape) - 1)
    return pl.BlockSpec((1,) + tuple(shape[1:]), lambda *_: (layer,) + zeros)


def _ada_kernel(c_ref, w_ref, b_ref, o_ref):
    c = c_ref[...]
    cond = c * _sigmoid(c)
    c_hi = cond.astype(BF16)
    c_lo = (cond - c_hi.astype(F32)).astype(BF16)
    w = w_ref[0]
    w_hi = w.astype(BF16)
    w_lo = (w - w_hi.astype(F32)).astype(BF16)
    o_ref[0] = _dot(c_hi, w_hi) + _dot(c_hi, w_lo) + _dot(c_lo, w_hi) + b_ref[0]


def _ada_mod(c, ada_w, ada_b):
    bsz = c.shape[0]
    tn = 2048
    n_out = ada_w.shape[-1]
    return pl.pallas_call(
        _ada_kernel,
        grid=(DEPTH, n_out // tn),
        in_specs=[
            pl.BlockSpec((bsz, D_MODEL), lambda l, j: (0, 0)),
            pl.BlockSpec((1, D_MODEL, tn), lambda l, j: (l, 0, j)),
            pl.BlockSpec((1, 1, tn), lambda l, j: (l, 0, j)),
        ],
        out_specs=pl.BlockSpec((1, bsz, tn), lambda l, j: (l, 0, j)),
        out_shape=jax.ShapeDtypeStruct((DEPTH, bsz, n_out), F32),
        compiler_params=pltpu.CompilerParams(vmem_limit_bytes=VMEM_LIMIT_BYTES),
        name="ada_mod",
    )(c, ada_w, ada_b.reshape(DEPTH, 1, n_out))


def _rope_kernel(pos_ref, invf_ref, sign_ref, cos_ref, sin_ref):
    ang = pos_ref[...] * invf_ref[...]
    cos_ref[...] = jnp.cos(ang)
    sin_ref[...] = jnp.sin(ang) * sign_ref[...]


def _rope_tables(positions):
    n = positions.size
    half = RET_DK // 2
    inv_freq = ROPE_BASE ** (-jnp.arange(half, dtype=F32) / half)
    invf = jnp.concatenate([inv_freq, inv_freq]).reshape(1, RET_DK)
    sign = jnp.concatenate([-jnp.ones((half,), F32), jnp.ones((half,), F32)]).reshape(1, RET_DK)
    pos = positions.astype(F32).reshape(n, 1)
    tm = 2048
    row = pl.BlockSpec((1, RET_DK), lambda i: (0, 0))
    out = pl.BlockSpec((tm, RET_DK), lambda i: (i, 0))
    return pl.pallas_call(
        _rope_kernel,
        grid=(n // tm,),
        in_specs=[pl.BlockSpec((tm, 1), lambda i: (i, 0)), row, row],
        out_specs=[out, out],
        out_shape=[jax.ShapeDtypeStruct((n, RET_DK), F32)] * 2,
        name="rope_tables",
    )(pos, invf, sign)


GA_LO = 2 * GLA_QK + 2 * GLA_V
OFF_GA = PROJ_WIDTH
W_AFTER = GA_LO + LANES


def _prep_w_proj_kernel(w_ref, o_ref):
    o_ref[0, :, :GA_LO] = w_ref[0, :, :GA_LO]
    gate = w_ref[0, :, GA_LO:W_AFTER]
    lane = lax.broadcasted_iota(jnp.int32, gate.shape, 1)
    o_ref[0, :, GA_LO:W_AFTER] = jnp.where(lane < GLA_RANK, gate, jnp.zeros_like(gate))
    o_ref[0, :, W_AFTER:] = w_ref[0, :, GA_LO + GLA_RANK:GA_LO + GLA_RANK + PROJ_WIDTH - GA_LO]


def _prep_w_proj(w_bf):
    depth, d, width = w_bf.shape
    rows = 256
    return pl.pallas_call(
        _prep_w_proj_kernel,
        grid=(depth, d // rows),
        in_specs=[pl.BlockSpec((1, rows, width), lambda l, i: (l, i, 0))],
        out_specs=pl.BlockSpec((1, rows, PROJ_WIDTH + LANES), lambda l, i: (l, i, 0)),
        out_shape=jax.ShapeDtypeStruct((depth, d, PROJ_WIDTH + LANES), BF16),
        name="prep_w_proj",
    )(w_bf)
PROJ_BLOCK = 256
TAIL_JOBS = 2
FFN_CHUNKS = 16
MOE_CHUNKS_PER_EXPERT = 8


def _attn_consts():
    r = ATTN_ROWS
    t = np.arange(r)
    same = (t[:, None] // CHUNK) == (t[None, :] // CHUNK)
    causal = t[:, None] >= t[None, :]
    tri = (same & causal).astype(np.float32)
    m_fwd = tri
    m_bwd = (same & ~causal).astype(np.float32)
    gam = 1.0 - 2.0 ** (-5.0 - np.arange(RET_HEADS, dtype=np.float64))
    lg = np.log(gam)
    dist = (t[:, None] - t[None, :]).astype(np.float64)
    d_ret = np.where(causal[None], np.exp(lg[:, None, None] * dist[None]),
                     np.where(same[None], np.exp(-lg[:, None, None] * dist[None]), 0.0))
    qdec = np.repeat(np.exp(lg[None, :] * (t[:, None] + 1.0)), RET_DK, axis=1)
    kdec = np.repeat(np.exp(lg[None, :] * (r - 1.0 - t[:, None])), RET_DK, axis=1)
    step_decay = [float(np.exp(lg[h] * r)) for h in range(RET_HEADS)]
    hmask = np.zeros((GLA_HEADS, 1, GLA_QK), np.float32)
    for h in range(GLA_HEADS):
        hmask[h, 0, h * GLA_DK:(h + 1) * GLA_DK] = 1.0
    return dict(tri=tri, m_fwd=m_fwd, m_bwd=m_bwd, d_ret=d_ret.astype(np.float32),
                qdec=qdec.astype(np.float32), kdec=kdec.astype(np.float32),
                step_decay=step_decay, hmask=hmask)


def _mixer_kernel(step_decay, steps_per_seq, n_tiles, n_cast, x_ref, mod_ref, g_ref, w_ref, *rest):
    proj_a, proj_b, o_scr, st_g, st_g_bf, st_r = rest[-6:]
    n_mix_in = len(rest) - 6 - 2 * n_cast - 2
    mix_in = rest[:n_mix_in]
    cast_src = rest[n_mix_in:n_mix_in + n_cast]
    y_ref, h2_ref = rest[n_mix_in + n_cast:n_mix_in + n_cast + 2]
    cast_dst = rest[n_mix_in + n_cast + 2:n_mix_in + 2 * n_cast + 2]
    j = pl.program_id(0)
    last = n_tiles
    bufs = (proj_a, proj_b)

    @pl.when((j - 1) % steps_per_seq == 0)
    def _():
        st_g[...] = jnp.zeros_like(st_g)
        st_g_bf[...] = jnp.zeros_like(st_g_bf)
        st_r[...] = jnp.zeros_like(st_r)

    def step(proj_w, proj_r):
        pending = []
        if proj_w is not None:
            m = mod_ref[0, 0]
            h = _rms_mod(x_ref[0], g_ref[0], _row(m, MOD_SC1), _row(m, MOD_SH1)).astype(BF16)

            def block(w0, out0, width):
                def emit():
                    proj_w[:, out0:out0 + width] = _dot(h, w_ref[0, :, w0:w0 + width])
                return emit

            for src, dst in zip(cast_src, cast_dst):
                dst[...] = src[...].astype(BF16)

            pending = ([block(c0, c0, PROJ_BLOCK) for c0 in range(0, GA_LO, PROJ_BLOCK)]
                       + [block(GA_LO, OFF_GA, LANES)]
                       + [block(W_AFTER + c0, GA_LO + c0, PROJ_BLOCK)
                          for c0 in range(0, PROJ_WIDTH - GA_LO, PROJ_BLOCK)])
        if proj_r is None:
            for job in pending:
                job()
        else:
            _attn_body(step_decay, pending, proj_r, *mix_in, y_ref, h2_ref, o_scr, st_g, st_g_bf, st_r)

    @pl.when(j == 0)
    def _():
        step(bufs[0], None)

    for parity in range(2):
        @pl.when(jnp.logical_and(jnp.logical_and(j > 0, j < last), j % 2 == parity))
        def _():
            step(bufs[parity], bufs[1 - parity])

    @pl.when(j == last)
    def _():
        step(None, bufs[(n_tiles - 1) % 2])


def _attn_body(step_decay, pending, proj_ref, cos_ref, sin_ref, walpha_ref, balpha_ref,
               glag_ref, gng_ref, gnb_ref, tri_ref, mfwd_ref, mbwd_ref, dret_ref,
               qdec_ref, kdec_ref, hmask_ref, xres_ref, modres_ref, wout_ref, gffn_ref,
               y_ref, h2_ref, o_scr, st_g, st_g_bf, st_r):
    r = ATTN_ROWS
    pending = list(pending)

    def interleave(n=1):
        for _ in range(min(n, len(pending) - TAIL_JOBS)):
            pending.pop(0)()

    z = _dot(proj_ref[:, OFF_GA:].astype(BF16), walpha_ref[0]) + balpha_ref[0]
    interleave(2)
    log_a = (jnp.minimum(z, 0.0) - jnp.log(1.0 + jnp.exp(-jnp.abs(z)))) * (1.0 / GLA_TAU)
    hi = log_a.astype(BF16)
    r1 = log_a - hi.astype(F32)
    mid = r1.astype(BF16)
    lo = (r1 - mid.astype(F32)).astype(BF16)
    tri = tri_ref[...]
    b = _dot(tri, hi) + _dot(tri, mid) + _dot(tri, lo)
    interleave(2)
    b3 = b.reshape(ATTN_CHUNKS, CHUNK, GLA_QK)
    b_last = b3[:, CHUNK - 1:CHUNK, :]
    k_upd_scale = jnp.exp(b_last - b3).reshape(r, GLA_QK)
    eb = jnp.exp(b)
    enb = jnp.exp(-b)
    q = proj_ref[:,OFF_GQ:OFF_GQ + GLA_QK] * (GLA_DK ** -0.5)
    k = proj_ref[:,OFF_GK:OFF_GK + GLA_QK]
    v_bf = proj_ref[:,OFF_GV:OFF_GV + GLA_V].astype(BF16)
    q_f = q * eb
    q_b = q * enb
    k_f = (k * enb).astype(BF16)
    k_b = (k * eb).astype(BF16)
    k_u = (k * k_upd_scale).astype(BF16)
    q_f_bf = q_f.astype(BF16)

    b_t = b.T
    heads_per_group = LANES // GLA_DK
    inter = []
    for g in range(ATTN_CHUNKS):
        rows = slice(g * CHUNK, (g + 1) * CHUNK)
        inter.append(_dot(q_f_bf[rows], st_g_bf[...]))
        decay = jnp.exp(b_t[:, (g + 1) * CHUNK - 1:(g + 1) * CHUNK])
        for grp in range(GLA_QK // LANES):
            v_lanes = slice(grp * heads_per_group * GLA_DV, (grp + 1) * heads_per_group * GLA_DV)
            upd = _dot_tn(k_u[rows, grp * LANES:(grp + 1) * LANES], v_bf[rows, v_lanes])
            for i in range(heads_per_group):
                h = grp * heads_per_group + i
                keys = slice(h * GLA_DK, (h + 1) * GLA_DK)
                st = (st_g[h] * decay[keys]
                      + upd[i * GLA_DK:(i + 1) * GLA_DK, i * GLA_DV:(i + 1) * GLA_DV])
                st_g[h] = st
                st_g_bf[keys, h * GLA_DV:(h + 1) * GLA_DV] = st.astype(BF16)
        interleave()
    o_inter = jnp.concatenate(inter, axis=0)

    m_fwd = mfwd_ref[...] > 0.5
    m_bwd = mbwd_ref[...] > 0.5
    for h in range(GLA_HEADS):
        hm = hmask_ref[h]
        s_f = _dot_nt((q_f * hm).astype(BF16), k_f)
        s_b = _dot_nt((q_b * hm).astype(BF16), k_b)
        sc = jnp.where(m_fwd, s_f, jnp.where(m_bwd, s_b, 0.0)).astype(BF16)
        lanes = slice(h * GLA_DV, (h + 1) * GLA_DV)
        o_h = _dot(sc, v_bf[:, lanes]) + o_inter[:, lanes]
        interleave()
        o_h = o_h * lax.rsqrt(jnp.mean(o_h * o_h, axis=-1, keepdims=True) + EPS)
        gate = proj_ref[:,OFF_GR + h * GLA_DV:OFF_GR + (h + 1) * GLA_DV]
        o_h = o_h * glag_ref[0, :, lanes] * (gate * _sigmoid(gate))
        o_scr[:, lanes] = o_h.astype(BF16)

    cos = cos_ref[0]
    sin = sin_ref[0]
    for h in range(RET_HEADS):
        lanes = slice(h * RET_DK, (h + 1) * RET_DK)
        qh = proj_ref[:,OFF_RQ + h * RET_DK:OFF_RQ + (h + 1) * RET_DK]
        kh = proj_ref[:,OFF_RK + h * RET_DK:OFF_RK + (h + 1) * RET_DK]
        vh = proj_ref[:,OFF_RV + h * RET_DV:OFF_RV + (h + 1) * RET_DV].astype(BF16)
        qh = (qh * cos + pltpu.roll(qh, RET_DK // 2, axis=1) * sin) * (RET_DK ** -0.5)
        kh = kh * cos + pltpu.roll(kh, RET_DK // 2, axis=1) * sin
        s = _dot_nt(qh.astype(BF16), kh.astype(BF16)) * dret_ref[h]
        st = st_r[h]
        o_h = _dot(s.astype(BF16), vh) + _dot((qh * qdec_ref[:, lanes]).astype(BF16), st.astype(BF16))
        st_r[h] = st * step_decay[h] + _dot_tn((kh * kdec_ref[:, lanes]).astype(BF16), vh)
        interleave()
        mu =jnp.mean(o_h, axis=-1, keepdims=True)
        d = o_h - mu
        var = jnp.mean(d * d, axis=-1, keepdims=True)
        o_h = d * lax.rsqrt(var + EPS) * gng_ref[0, :, lanes] + gnb_ref[0, :, lanes]
        gate = proj_ref[:,OFF_RG + h * RET_DV:OFF_RG + (h + 1) * RET_DV]
        o_h = o_h * (gate * _sigmoid(gate))
        o_scr[:, GLA_V + h * RET_DV:GLA_V + (h + 1) * RET_DV] = o_h.astype(BF16)
    interleave(len(pending))
    m = modres_ref[0, 0]
    y = xres_ref[0] + _row(m, MOD_G1) * _dot(o_scr[...], wout_ref[0])
    for job in pending:
        job()
    y_ref[0] = y
    h2_ref[0] = _rms_mod(y, gffn_ref[0], _row(m, MOD_SC2), _row(m, MOD_SH2)).astype(h2_ref.dtype)


def _mixer(layer, x, mod_all, g_all, w_proj_all, cos, sin, w_alpha_all, b_alpha_all,
           gla_g_all, gn_g_all, gn_b_all, w_out_all, g_ffn_all, h2_dtype, cast_jobs):
    bsz, seq, _ = x.shape
    r = ATTN_ROWS
    steps_per_seq = seq // r
    n_tiles = bsz * steps_per_seq
    c = _attn_consts()
    const2 = lambda shape: pl.BlockSpec(shape, lambda j: (0,) * len(shape))
    per_layer = lambda a: _layer_spec(a.shape, layer)
    proj_tile = lambda j: jnp.minimum(j, n_tiles - 1)
    mix_tile = lambda j: jnp.maximum(j - 1, 0)
    mix_blk = lambda w: pl.BlockSpec((1, r, w), lambda j: (mix_tile(j), 0, 0))

    def chunk_spec(a, chunks, first):
        return pl.BlockSpec((1,) + a.shape[1:], lambda j: (first + jnp.minimum(j, chunks - 1), 0, 0))

    cast_in = [chunk_spec(a, chunks, li * chunks) for a, chunks, li in cast_jobs]
    cast_out = [chunk_spec(a, chunks, 0) for a, chunks, li in cast_jobs]
    cast_shapes = [jax.ShapeDtypeStruct((chunks,) + a.shape[1:], BF16) for a, chunks, li in cast_jobs]
    out = pl.pallas_call(
        functools.partial(_mixer_kernel, c["step_decay"], steps_per_seq, n_tiles, len(cast_jobs)),
        grid=(n_tiles + 1,),
        in_specs=[
            pl.BlockSpec((1, r, D_MODEL), lambda j: (proj_tile(j), 0, 0)),
            pl.BlockSpec((1, 1, 6, D_MODEL), lambda j: (layer, proj_tile(j) // steps_per_seq, 0, 0)),
            per_layer(g_all), per_layer(w_proj_all),
            mix_blk(RET_DK), mix_blk(RET_DK),
            per_layer(w_alpha_all), per_layer(b_alpha_all),
            per_layer(gla_g_all), per_layer(gn_g_all), per_layer(gn_b_all),
            const2((r, r)), const2((r, r)), const2((r, r)), const2((RET_HEADS, r, r)),
            const2((r, RET_QK)), const2((r, RET_QK)),
            const2((GLA_HEADS, 1, GLA_QK)),
            mix_blk(D_MODEL),
            pl.BlockSpec((1, 1, 6, D_MODEL), lambda j: (layer, mix_tile(j) // steps_per_seq, 0, 0)),
            per_layer(w_out_all), per_layer(g_ffn_all),
        ] + cast_in,
        out_specs=[mix_blk(D_MODEL), mix_blk(D_MODEL)] + cast_out,
        out_shape=[jax.ShapeDtypeStruct((n_tiles, r, D_MODEL), F32),
                   jax.ShapeDtypeStruct((n_tiles, r, D_MODEL), h2_dtype)] + cast_shapes,
        scratch_shapes=[pltpu.VMEM((r, PROJ_WIDTH + LANES), F32),
                        pltpu.VMEM((r, PROJ_WIDTH + LANES), F32),
                        pltpu.VMEM((r, MIX_WIDTH), BF16),
                        pltpu.VMEM((GLA_HEADS, GLA_DK, GLA_DV), F32),
                        pltpu.VMEM((GLA_QK, GLA_V), BF16),
                        pltpu.VMEM((RET_HEADS, RET_DK, RET_DV), F32)],
        compiler_params=pltpu.CompilerParams(
            dimension_semantics=("arbitrary",), vmem_limit_bytes=VMEM_LIMIT_BYTES),
        name="mixer",
    )(x.reshape(n_tiles, r, D_MODEL), mod_all, g_all, w_proj_all,
      cos.reshape(n_tiles, r, RET_DK), sin.reshape(n_tiles, r, RET_DK),
      w_alpha_all, b_alpha_all, gla_g_all, gn_g_all, gn_b_all,
      jnp.asarray(c["tri"], BF16), jnp.asarray(c["m_fwd"]), jnp.asarray(c["m_bwd"]),
      jnp.asarray(c["d_ret"]), jnp.asarray(c["qdec"]), jnp.asarray(c["kdec"]),
      jnp.asarray(c["hmask"]),
      x.reshape(n_tiles, r, D_MODEL), mod_all, w_out_all, g_ffn_all, *[a for a, _, _ in cast_jobs])
    return out[0].reshape(x.shape), out[1].reshape(x.shape), out[2:]


ROUTE_ROWS = 512
EXPERT_ROWS = 512
SEG_ALIGN = 8
LOCAL_ROWS = -(-(2 * ROUTE_ROWS + N_EXPERTS * (SEG_ALIGN - 1)) // LANES) * LANES
PIECE_BITS = (max(ROUTE_ROWS, EXPERT_ROWS) // SEG_ALIGN).bit_length()
INFO_POS, INFO_PROB = 0, 2


def _route_kernel(hin_ref, rw_ref, tril_ref, upper_ref, h_ref, info_ref, cnt_ref):
    h = hin_ref[...]
    h_hi = h.astype(BF16)
    h_ref[...] = h_hi
    h_lo = (h - h_hi.astype(F32)).astype(BF16)
    hh = _dot(h_hi, rw_ref[...])
    logits = hh[:, :LANES] + hh[:, LANES:] + _dot(h_lo, rw_ref[:, :LANES])
    lane = lax.broadcasted_iota(jnp.int32, logits.shape, 1)
    neg = jnp.float32(-jnp.inf)
    lg = jnp.where(lane < N_EXPERTS, logits, neg)
    m1 = jnp.max(lg, axis=-1, keepdims=True)
    i1 = jnp.min(jnp.where(lg == m1, lane, LANES), axis=-1, keepdims=True)
    lg2 = jnp.where(lane == i1, neg, lg)
    m2 = jnp.max(lg2, axis=-1, keepdims=True)
    i2 = jnp.min(jnp.where(lg2 == m2, lane, LANES), axis=-1, keepdims=True)
    e2 = jnp.exp(m2 - m1)
    p1 = 1.0 / (1.0 + e2)
    p2 = e2 * p1
    sel1 = lane == i1
    sel2 = lane == i2
    onehot = jnp.where(sel1, 1.0, 0.0) + jnp.where(sel2, 1.0, 0.0)
    incl = _dot(tril_ref[...], onehot.astype(BF16))
    counts = incl[ROUTE_ROWS - 1:ROUTE_ROWS]
    seg_units = jnp.floor((counts + (SEG_ALIGN - 1)) * (1.0 / SEG_ALIGN))
    seg_units8 = jnp.broadcast_to(seg_units, (8, LANES)).astype(BF16)
    seg_off = _dot(seg_units8, upper_ref[...])[0:1] * SEG_ALIGN
    pos = incl - onehot + seg_off
    pos1 = jnp.sum(jnp.where(sel1, pos, 0.0), axis=-1, keepdims=True)
    pos2 = jnp.sum(jnp.where(sel2, pos, 0.0), axis=-1, keepdims=True)
    cnt_ref[...] = jnp.broadcast_to(counts, (8, LANES))
    rec = jnp.zeros(logits.shape, F32)
    for lane_id, val in ((INFO_POS, pos1), (INFO_POS + 1, pos2), (INFO_PROB, p1), (INFO_PROB + 1, p2)):
        rec = jnp.where(lane == lane_id, val, rec)
    info_ref[...] = rec


def _route(h_f32, router_w):
    n = h_f32.shape[0]
    tm = ROUTE_ROWS
    w_hi = router_w.astype(BF16)
    w_lo = (router_w - w_hi.astype(F32)).astype(BF16)
    pad = ((0, 0), (0, LANES - N_EXPERTS))
    rw = jnp.concatenate([jnp.pad(w_hi, pad), jnp.pad(w_lo, pad)], axis=1)
    tril = jnp.asarray(np.tril(np.ones((tm, tm), np.float32)), BF16)
    upper = jnp.asarray(np.triu(np.ones((LANES, LANES), np.float32), 1), BF16)
    return pl.pallas_call(
        _route_kernel,
        grid=(n // tm,),
        in_specs=[pl.BlockSpec((tm, D_MODEL), lambda i: (i, 0)),
                  pl.BlockSpec((D_MODEL, 2 * LANES), lambda i: (0, 0)),
                  pl.BlockSpec((tm, tm), lambda i: (0, 0)),
                  pl.BlockSpec((LANES, LANES), lambda i: (0, 0))],
        out_specs=[pl.BlockSpec((tm, D_MODEL), lambda i: (i, 0)),
                   pl.BlockSpec((tm, LANES), lambda i: (i, 0)),
                   pl.BlockSpec((8, LANES), lambda i: (i, 0))],
        out_shape=[jax.ShapeDtypeStruct((n, D_MODEL), BF16),
                   jax.ShapeDtypeStruct((n, LANES), F32),
                   jax.ShapeDtypeStruct((n // tm * 8, LANES), F32)],
        compiler_params=pltpu.CompilerParams(
            dimension_semantics=("arbitrary",), vmem_limit_bytes=VMEM_LIMIT_BYTES),
        name="route",
    )(h_f32, rw, tril, upper)


def _rows(ref, row, n_rows):
    return ref.at[pl.ds(pl.multiple_of(row, SEG_ALIGN), n_rows)]


def _for_each_piece(n_units, fn):
    for bit in reversed(range(PIECE_BITS)):
        covered = (n_units >> (bit + 1)) << (bit + 1)

        @pl.when(((n_units >> bit) & 1) == 1)
        def _():
            fn(covered * SEG_ALIGN, SEG_ALIGN << bit)


def _for_each_segment_piece(seg_ref, n_seg, tile, fn):
    for e in range(N_EXPERTS):
        s = tile * N_EXPERTS + e
        sorted_row = seg_ref[s]
        local_row = seg_ref[n_seg + s]
        _for_each_piece(seg_ref[2 * n_seg + s],
                        lambda first, n_rows: fn(local_row + first, sorted_row + first, n_rows))


def _selection(info, coeff1, coeff2):
    lane = lax.broadcasted_iota(jnp.int32, (info.shape[0], LOCAL_ROWS), 1)
    pos1 = info[:, INFO_POS:INFO_POS + 1].astype(jnp.int32)
    pos2 = info[:, INFO_POS + 1:INFO_POS + 2].astype(jnp.int32)
    return jnp.where(lane == pos1, coeff1, jnp.where(lane == pos2, coeff2, 0.0)).astype(BF16)


def _dispatch_kernel(n_seg, seg_ref, pad_ref, h_ref, info_ref, xs_ref, sorted_scr, zero_scr, sems, pad_sem):
    i = pl.program_id(0)
    n_steps = pl.num_programs(0)
    slot = i % 2

    def move(tile, buf_slot, op):
        def piece(local_row, sorted_row, n_rows):
            cp = pltpu.make_async_copy(_rows(sorted_scr.at[buf_slot], local_row, n_rows),
                                       _rows(xs_ref, sorted_row, n_rows), sems.at[buf_slot])
            op(cp)
        _for_each_segment_piece(seg_ref, n_seg, tile, piece)

    @pl.when(i == 0)
    def _():
        zero_scr[...] = jnp.zeros_like(zero_scr)
        for op in (lambda cp: cp.start(), lambda cp: cp.wait()):
            for e in range(N_EXPERTS):
                lo = pad_ref[e]
                _for_each_piece(pad_ref[N_EXPERTS + e],
                                lambda first, n_rows: op(pltpu.make_async_copy(
                                    zero_scr.at[pl.ds(0, n_rows)], _rows(xs_ref, lo + first, n_rows),
                                    pad_sem)))

        def tile_copy(j):
            return pltpu.make_async_copy(zero_scr, _rows(xs_ref, j * EXPERT_ROWS, EXPERT_ROWS), pad_sem)

        def fill_tile(j, carry):
            tile_copy(j).start()
            return carry

        def drain_tile(j, carry):
            tile_copy(j).wait()
            return carry

        n_tiles = xs_ref.shape[0] // EXPERT_ROWS
        lax.fori_loop(pad_ref[2 * N_EXPERTS], n_tiles, fill_tile, 0)
        lax.fori_loop(pad_ref[2 * N_EXPERTS], n_tiles, drain_tile, 0)

    @pl.when(i >= 2)
    def _():
        move(i - 2, slot, lambda cp: cp.wait())

    sel = _selection(info_ref[...], 1.0, 1.0)
    sorted_scr[slot] = _dot_tn(sel, h_ref[...])
    move(i, slot, lambda cp: cp.start())

    @pl.when(i == n_steps - 1)
    def _():
        @pl.when(i >= 1)
        def _():
            move(i - 1, 1 - slot, lambda cp: cp.wait())
        move(i, slot, lambda cp: cp.wait())


def _dispatch(h2, info, segs, pads, n_sorted):
    n = h2.shape[0]
    tm = ROUTE_ROWS
    n_seg = n // tm * N_EXPERTS
    return pl.pallas_call(
        functools.partial(_dispatch_kernel, n_seg),
        grid_spec=pltpu.PrefetchScalarGridSpec(
            num_scalar_prefetch=2,
            grid=(n // tm,),
            in_specs=[pl.BlockSpec((tm, D_MODEL), lambda i, s, p: (i, 0)),
                      pl.BlockSpec((tm, LANES), lambda i, s, p: (i, 0))],
            out_specs=pl.BlockSpec(memory_space=pl.ANY),
            scratch_shapes=[pltpu.VMEM((2, LOCAL_ROWS, D_MODEL), F32),
                            pltpu.VMEM((EXPERT_ROWS, D_MODEL), F32),
                            pltpu.SemaphoreType.DMA((2,)), pltpu.SemaphoreType.DMA(())]),
        out_shape=jax.ShapeDtypeStruct((n_sorted, D_MODEL), F32),
        compiler_params=pltpu.CompilerParams(
            dimension_semantics=("arbitrary",), vmem_limit_bytes=VMEM_LIMIT_BYTES),
        name="dispatch",
    )(segs, pads, h2, info)


def _expert_kernel(tile_blk, tile_e, n_act, x_ref, w1_ref, w3_ref, w2_ref, y_ref):
    active = pl.program_id(0) < n_act[0]

    @pl.when(active)
    def _():
        h = x_ref[...].astype(BF16)
        a = _dot(h, w1_ref[0])
        u = (a * _sigmoid(a) * _dot(h, w3_ref[0])).astype(BF16)
        y_ref[...] = _dot(u, w2_ref[0])

    @pl.when(jnp.logical_not(active))
    def _():
        y_ref[...] = jnp.zeros_like(y_ref)


def _experts(xs, tile_blk, tile_e, n_act, w1, w3, w2):
    n_sorted = xs.shape[0]
    tm = EXPERT_ROWS
    f = w1.shape[-1]
    return pl.pallas_call(
        _expert_kernel,
        grid_spec=pltpu.PrefetchScalarGridSpec(
            num_scalar_prefetch=3,
            grid=(n_sorted // tm,),
            in_specs=[pl.BlockSpec((tm, D_MODEL), lambda i, b, e, n: (b[i], 0)),
                      pl.BlockSpec((1, D_MODEL, f), lambda i, b, e, n: (e[i], 0, 0)),
                      pl.BlockSpec((1, D_MODEL, f), lambda i, b, e, n: (e[i], 0, 0)),
                      pl.BlockSpec((1, f, D_MODEL), lambda i, b, e, n: (e[i], 0, 0))],
            out_specs=pl.BlockSpec((tm, D_MODEL), lambda i, b, e, n: (i, 0))),
        out_shape=jax.ShapeDtypeStruct((n_sorted, D_MODEL), F32),
        compiler_params=pltpu.CompilerParams(
            dimension_semantics=("arbitrary",), vmem_limit_bytes=VMEM_LIMIT_BYTES),
        name="experts",
    )(tile_blk, tile_e, n_act, xs, w1, w3, w2)


def _combine_kernel(final_norm, n_seg, seg_ref, x_ref, mod_ref, info_ref, fg_ref, ys_ref, out_ref,
                    ybuf, sems):
    i = pl.program_id(0)
    n_steps = pl.num_programs(0)
    slot = i % 2

    def move(tile, buf_slot, op):
        def piece(local_row, sorted_row, n_rows):
            cp = pltpu.make_async_copy(_rows(ys_ref, sorted_row, n_rows),
                                       _rows(ybuf.at[buf_slot], local_row, n_rows), sems.at[buf_slot])
            op(cp)
        _for_each_segment_piece(seg_ref, n_seg, tile, piece)

    @pl.when(i == 0)
    def _():
        ybuf[...] = jnp.zeros_like(ybuf)
        move(0, 0, lambda cp: cp.start())

    @pl.when(i + 1 < n_steps)
    def _():
        move(i + 1, 1 - slot, lambda cp: cp.start())

    move(i, slot, lambda cp: cp.wait())
    info = info_ref[...]
    sel = _selection(info, info[:, INFO_PROB:INFO_PROB + 1], info[:, INFO_PROB + 1:INFO_PROB + 2])
    y = _dot(sel, ybuf[slot].astype(BF16))
    out = x_ref[...] + _row(mod_ref[0, 0], MOD_G2) * y
    if final_norm:
        ms = jnp.mean(out * out, axis=-1, keepdims=True)
        out = out * lax.rsqrt(ms + EPS) * fg_ref[...]
    out_ref[...] = out


def _combine(layer, x2, mod_all, info, ys, segs, final_g, final_norm, rows_per_batch):
    n = x2.shape[0]
    tm = ROUTE_ROWS
    per_b = rows_per_batch // tm
    n_seg = n // tm * N_EXPERTS
    return pl.pallas_call(
        functools.partial(_combine_kernel, final_norm, n_seg),
        grid_spec=pltpu.PrefetchScalarGridSpec(
            num_scalar_prefetch=1,
            grid=(n // tm,),
            in_specs=[pl.BlockSpec((tm, D_MODEL), lambda i, s: (i, 0)),
                      pl.BlockSpec((1, 1, 6, D_MODEL), lambda i, s: (layer, i // per_b, 0, 0)),
                      pl.BlockSpec((tm, LANES), lambda i, s: (i, 0)),
                      pl.BlockSpec((1, D_MODEL), lambda i, s: (0, 0)),
                      pl.BlockSpec(memory_space=pl.ANY)],
            out_specs=pl.BlockSpec((tm, D_MODEL), lambda i, s: (i, 0)),
            scratch_shapes=[pltpu.VMEM((2, LOCAL_ROWS, D_MODEL), F32),
                            pltpu.SemaphoreType.DMA((2,))]),
        out_shape=jax.ShapeDtypeStruct(x2.shape, F32),
        compiler_params=pltpu.CompilerParams(
            dimension_semantics=("arbitrary",), vmem_limit_bytes=VMEM_LIMIT_BYTES),
        name="combine",
    )(segs, x2, mod_all, info, final_g, ys)


def _moe(layer, x, h_f32, mod_all, router_w, w1, w3, w2, final_g, final_norm):
    bsz, seq, _ = x.shape
    n = bsz * seq
    tm = EXPERT_ROWS
    n_tok_tiles = n // ROUTE_ROWS
    x2 = x.reshape(n, D_MODEL)
    h2, info, cnt = _route(h_f32.reshape(n, D_MODEL), router_w)
    counts = cnt.reshape(n_tok_tiles, 8, LANES)[:, 0, :N_EXPERTS].astype(jnp.int32)
    seg_rows = (counts + (SEG_ALIGN - 1)) // SEG_ALIGN * SEG_ALIGN
    group_rows = jnp.sum(seg_rows, axis=0)
    tiles = (group_rows + (tm - 1)) // tm
    tile_end = jnp.cumsum(tiles)
    group_off = (tile_end - tiles) * tm
    seg_sorted = group_off[None, :] + jnp.cumsum(seg_rows, axis=0) - seg_rows
    seg_local = jnp.cumsum(seg_rows, axis=1) - seg_rows
    segs = jnp.concatenate([seg_sorted.reshape(-1), seg_local.reshape(-1),
                            (seg_rows // SEG_ALIGN).reshape(-1)]).astype(jnp.int32)
    max_rows = 2 * n + n_tok_tiles * N_EXPERTS * (SEG_ALIGN - 1)
    n_tiles = -(-max_rows // tm) + N_EXPERTS
    n_act = tile_end[-1]
    t = jnp.minimum(jnp.arange(n_tiles, dtype=jnp.int32), n_act - 1)
    tile_e = jnp.sum(t[:, None] >= tile_end[None, :], axis=1).astype(jnp.int32)
    pad_lo = group_off + group_rows
    pads = jnp.concatenate([pad_lo, (tile_end * tm - pad_lo) // SEG_ALIGN,
                            n_act[None]]).astype(jnp.int32)
    xs = _dispatch(h2, info, segs, pads, n_tiles * tm)
    ys = _experts(xs, t, tile_e, n_act.reshape(1).astype(jnp.int32), w1, w3, w2)
    out = _combine(layer, x2, mod_all, info, ys, segs, final_g, final_norm, seq)
    return out.reshape(x.shape)


def _ffn_kernel(final_norm, x_ref, h_ref, mod_ref, w1_ref, w3_ref, w2_ref, fg_ref, y_ref):
    h = h_ref[0]
    a = _dot(h, w1_ref[...])
    u = (a * _sigmoid(a) * _dot(h, w3_ref[...])).astype(BF16)
    out = x_ref[0] + _row(mod_ref[0, 0], MOD_G2) * _dot(u, w2_ref[...])
    if final_norm:
        ms = jnp.mean(out * out, axis=-1, keepdims=True)
        out = out * lax.rsqrt(ms + EPS) * fg_ref[...]
    y_ref[0] = out


def _ffn(layer, x, h2, mod_all, w1, w3, w2, final_g, final_norm):
    bsz, seq, _ = x.shape
    f = w1.shape[-1]
    tm = 512
    row = pl.BlockSpec((1, tm, D_MODEL), lambda b, i: (b, i, 0))
    resident = lambda shape: pl.BlockSpec(shape, lambda b, i: (0, 0), pipeline_mode=pl.Buffered(1))
    return pl.pallas_call(
        functools.partial(_ffn_kernel, final_norm),
        grid=(bsz, seq // tm),
        in_specs=[row, row,
                  pl.BlockSpec((1, 1, 6, D_MODEL), lambda b, i: (layer, b, 0, 0)),
                  resident((D_MODEL, f)), resident((D_MODEL, f)), resident((f, D_MODEL)),
                  pl.BlockSpec((1, D_MODEL), lambda b, i: (0, 0))],
        out_specs=row,
        out_shape=jax.ShapeDtypeStruct(x.shape, F32),
        compiler_params=pltpu.CompilerParams(
            dimension_semantics=("arbitrary", "arbitrary"),
            vmem_limit_bytes=VMEM_LIMIT_BYTES),
        name="ffn",
    )(x, h2, mod_all, w1, w3, w2, final_g)


def kernel(x, c, positions, ada_w, ada_b, norm_mix_g, norm_ffn_g, w_in, gla_w_alpha, gla_b_alpha,
           gla_norm_g, ret_gn_g, ret_gn_b, w_out, ffn_w1, ffn_w3, ffn_w2, router_w, moe_w1,
           moe_w3, moe_w2, final_g):
    bsz = x.shape[0]
    mod_all = _ada_mod(c, ada_w, ada_b).reshape(DEPTH, bsz, 6, D_MODEL)
    cos, sin = _rope_tables(positions)
    lane_pad = PROJ_WIDTH + LANES - w_in.shape[-1]
    w_proj_all = _prep_w_proj(jnp.pad(w_in, ((0, 0), (0, 0), (0, lane_pad))).astype(BF16))
    w_alpha_all = jnp.pad(gla_w_alpha, ((0, 0), (0, LANES - GLA_RANK), (0, 0))).astype(BF16)
    vec = lambda a: a.reshape(a.shape[0], 1, a.shape[1])
    b_alpha_all, gla_g_all, gn_g_all, gn_b_all = map(vec, (gla_b_alpha, gla_norm_g, ret_gn_g, ret_gn_b))
    g_mix_all, g_ffn_all = vec(norm_mix_g), vec(norm_ffn_g)
    w_out_all = w_out.astype(BF16)
    fg = final_g.reshape(1, D_MODEL)
    d_ff = ffn_w1.shape[-1]
    d_fe = moe_w1.shape[-1]
    n_dense, n_moe = ffn_w1.shape[0], moe_w1.shape[0]
    dense_src = (ffn_w1.reshape(n_dense * FFN_CHUNKS, D_MODEL // FFN_CHUNKS, d_ff),
                 ffn_w3.reshape(n_dense * FFN_CHUNKS, D_MODEL // FFN_CHUNKS, d_ff),
                 ffn_w2.reshape(n_dense * FFN_CHUNKS, d_ff // FFN_CHUNKS, D_MODEL))
    moe_chunks = N_EXPERTS * MOE_CHUNKS_PER_EXPERT
    moe_src = (moe_w1.reshape(n_moe * moe_chunks, D_MODEL // MOE_CHUNKS_PER_EXPERT, d_fe),
               moe_w3.reshape(n_moe * moe_chunks, D_MODEL // MOE_CHUNKS_PER_EXPERT, d_fe),
               moe_w2.reshape(n_moe * moe_chunks, d_fe // MOE_CHUNKS_PER_EXPERT, D_MODEL))
    for layer in range(DEPTH):
        li = layer // 2
        dense = layer % 2 == 0
        jobs = [(a, FFN_CHUNKS if dense else moe_chunks, li) for a in (dense_src if dense else moe_src)]
        x, h2, (w1, w3, w2) = _mixer(layer, x, mod_all, g_mix_all, w_proj_all, cos, sin,
                                     w_alpha_all, b_alpha_all, gla_g_all, gn_g_all, gn_b_all, w_out_all,
                                     g_ffn_all, BF16 if dense else F32, jobs)
        last = layer == DEPTH - 1
        if dense:
            x = _ffn(layer, x, h2, mod_all, w1.reshape(D_MODEL, d_ff), w3.reshape(D_MODEL, d_ff),
                     w2.reshape(d_ff, D_MODEL), fg, last)
        else:
            x = _moe(layer, x, h2, mod_all, router_w[li], w1.reshape(N_EXPERTS, D_MODEL, d_fe),
                     w3.reshape(N_EXPERTS, D_MODEL, d_fe), w2.reshape(N_EXPERTS, d_fe, D_MODEL), fg, last)
    return x
```

```python
import functools

import numpy as np
import jax
import jax.numpy as jnp
from jax import lax
from jax.experimental import pallas as pl
from jax.experimental.pallas import tpu as pltpu

F32 = jnp.float32
BF16 = jnp.bfloat16

D_MODEL = 1024
DEPTH = 4
CHUNK = 64
GLA_HEADS = 4
GLA_DK = 64
GLA_DV = 128
GLA_RANK = 16
GLA_TAU = 16.0
RET_HEADS = 4
RET_DK = 128
RET_DV = 128
ROPE_BASE = 10000.0
N_EXPERTS = 8
EPS = 1e-6
GLA_QK = GLA_HEADS * GLA_DK
GLA_V = GLA_HEADS * GLA_DV
RET_QK = RET_HEADS * RET_DK
RET_V = RET_HEADS * RET_DV
MIX_WIDTH = GLA_V + RET_V

LANES = 128
VMEM_LIMIT_BYTES = 56 * 1024 * 1024

PROJ_WIDTH = 2 * GLA_QK + 2 * GLA_V + 2 * RET_QK + 2 * RET_V
OFF_GQ = 0
OFF_GK = OFF_GQ + GLA_QK
OFF_GV = OFF_GK + GLA_QK
OFF_GR = OFF_GV + GLA_V
OFF_RQ = OFF_GR + GLA_V
OFF_RK = OFF_RQ + RET_QK
OFF_RV = OFF_RK + RET_QK
OFF_RG = OFF_RV + RET_V

ATTN_ROWS = 256
ATTN_CHUNKS = ATTN_ROWS // CHUNK

MOD_SH1, MOD_SC1, MOD_G1, MOD_SH2, MOD_SC2, MOD_G2 = range(6)


def _sigmoid(x):
    return 1.0 / (1.0 + jnp.exp(-x))


def _rms_mod(x, g, sc, sh):
    ms = jnp.mean(x * x, axis=-1, keepdims=True)
    return (x * lax.rsqrt(ms + EPS)) * g * (1.0 + sc) + sh


def _dot(a, b):
    return jnp.dot(a, b, preferred_element_type=F32)


def _dot_nt(a, b):
    return lax.dot_general(a, b, (((1,), (1,)), ((), ())), preferred_element_type=F32)


def _dot_tn(a, b):
    return lax.dot_general(a, b, (((0,), (0,)), ((), ())), preferred_element_type=F32)


def _row(m, r):
    return m[r:r + 1]


def _layer_spec(shape, layer):
    zeros = (0,) * (len(shape) - 1)
    return pl.BlockSpec((1,) + tuple(shape[1:]), lambda *_: (layer,) + zeros)


def _ada_kernel(c_ref, w_ref, b_ref, o_ref):
    c = c_ref[...]
    cond = c * _sigmoid(c)
    c_hi = cond.astype(BF16)
    c_lo = (cond - c_hi.astype(F32)).astype(BF16)
    w = w_ref[0]
    w_hi = w.astype(BF16)
    w_lo = (w - w_hi.astype(F32)).astype(BF16)
    o_ref[0] = _dot(c_hi, w_hi) + _dot(c_hi, w_lo) + _dot(c_lo, w_hi) + b_ref[0]


def _ada_mod(c, ada_w, ada_b):
    bsz = c.shape[0]
    tn = 2048
    n_out = ada_w.shape[-1]
    return pl.pallas_call(
        _ada_kernel,
        grid=(DEPTH, n_out // tn),
        in_specs=[
            pl.BlockSpec((bsz, D_MODEL), lambda l, j: (0, 0)),
            pl.BlockSpec((1, D_MODEL, tn), lambda l, j: (l, 0, j)),
            pl.BlockSpec((1, 1, tn), lambda l, j: (l, 0, j)),
        ],
        out_specs=pl.BlockSpec((1, bsz, tn), lambda l, j: (l, 0, j)),
        out_shape=jax.ShapeDtypeStruct((DEPTH, bsz, n_out), F32),
        compiler_params=pltpu.CompilerParams(vmem_limit_bytes=VMEM_LIMIT_BYTES),
        name="ada_mod",
    )(c, ada_w, ada_b.reshape(DEPTH, 1, n_out))


def _rope_kernel(pos_ref, invf_ref, sign_ref, cos_ref, sin_ref):
    ang = pos_ref[...] * invf_ref[...]
    cos_ref[...] = jnp.cos(ang)
    sin_ref[...] = jnp.sin(ang) * sign_ref[...]


def _rope_tables(positions):
    n = positions.size
    half = RET_DK // 2
    inv_freq = ROPE_BASE ** (-jnp.arange(half, dtype=F32) / half)
    invf = jnp.concatenate([inv_freq, inv_freq]).reshape(1, RET_DK)
    sign = jnp.concatenate([-jnp.ones((half,), F32), jnp.ones((half,), F32)]).reshape(1, RET_DK)
    pos = positions.astype(F32).reshape(n, 1)
    tm = 2048
    row = pl.BlockSpec((1, RET_DK), lambda i: (0, 0))
    out = pl.BlockSpec((tm, RET_DK), lambda i: (i, 0))
    return pl.pallas_call(
        _rope_kernel,
        grid=(n // tm,),
        in_specs=[pl.BlockSpec((tm, 1), lambda i: (i, 0)), row, row],
        out_specs=[out, out],
        out_shape=[jax.ShapeDtypeStruct((n, RET_DK), F32)] * 2,
        name="rope_tables",
    )(pos, invf, sign)


GA_LO = 2 * GLA_QK + 2 * GLA_V
OFF_GA = PROJ_WIDTH
W_AFTER = GA_LO + LANES


def _prep_w_proj_kernel(w_ref, o_ref):
    o_ref[0, :, :GA_LO] = w_ref[0, :, :GA_LO]
    gate = w_ref[0, :, GA_LO:W_AFTER]
    lane = lax.broadcasted_iota(jnp.int32, gate.shape, 1)
    o_ref[0, :, GA_LO:W_AFTER] = jnp.where(lane < GLA_RANK, gate, jnp.zeros_like(gate))
    o_ref[0, :, W_AFTER:] = w_ref[0, :, GA_LO + GLA_RANK:GA_LO + GLA_RANK + PROJ_WIDTH - GA_LO]


def _prep_w_proj(w_bf):
    depth, d, width = w_bf.shape
    rows = 256
    return pl.pallas_call(
        _prep_w_proj_kernel,
        grid=(depth, d // rows),
        in_specs=[pl.BlockSpec((1, rows, width), lambda l, i: (l, i, 0))],
        out_specs=pl.BlockSpec((1, rows, PROJ_WIDTH + LANES), lambda l, i: (l, i, 0)),
        out_shape=jax.ShapeDtypeStruct((depth, d, PROJ_WIDTH + LANES), BF16),
        name="prep_w_proj",
    )(w_bf)
PROJ_BLOCK = 256
TAIL_JOBS = 2
FFN_CHUNKS = 16
MOE_CHUNKS_PER_EXPERT = 8


def _attn_consts():
    r = ATTN_ROWS
    t = np.arange(r)
    same = (t[:, None] // CHUNK) == (t[None, :] // CHUNK)
    causal = t[:, None] >= t[None, :]
    tri = (same & causal).astype(np.float32)
    m_fwd = tri
    m_bwd = (same & ~causal).astype(np.float32)
    gam = 1.0 - 2.0 ** (-5.0 - np.arange(RET_HEADS, dtype=np.float64))
    lg = np.log(gam)
    dist = (t[:, None] - t[None, :]).astype(np.float64)
    d_ret = np.where(causal[None], np.exp(lg[:, None, None] * dist[None]),
                     np.where(same[None], np.exp(-lg[:, None, None] * dist[None]), 0.0))
    qdec = np.repeat(np.exp(lg[None, :] * (t[:, None] + 1.0)), RET_DK, axis=1)
    kdec = np.repeat(np.exp(lg[None, :] * (r - 1.0 - t[:, None])), RET_DK, axis=1)
    step_decay = [float(np.exp(lg[h] * r)) for h in range(RET_HEADS)]
    hmask = np.zeros((GLA_HEADS, 1, GLA_QK), np.float32)
    for h in range(GLA_HEADS):
        hmask[h, 0, h * GLA_DK:(h + 1) * GLA_DK] = 1.0
    return dict(tri=tri, m_fwd=m_fwd, m_bwd=m_bwd, d_ret=d_ret.astype(np.float32),
                qdec=qdec.astype(np.float32), kdec=kdec.astype(np.float32),
                step_decay=step_decay, hmask=hmask)


def _mixer_kernel(step_decay, steps_per_seq, n_tiles, n_cast, x_ref, mod_ref, g_ref, w_ref, *rest):
    proj_a, proj_b, o_scr, st_g, st_g_bf, st_r = rest[-6:]
    n_mix_in = len(rest) - 6 - 2 * n_cast - 2
    mix_in = rest[:n_mix_in]
    cast_src = rest[n_mix_in:n_mix_in + n_cast]
    y_ref, h2_ref = rest[n_mix_in + n_cast:n_mix_in + n_cast + 2]
    cast_dst = rest[n_mix_in + n_cast + 2:n_mix_in + 2 * n_cast + 2]
    j = pl.program_id(0)
    last = n_tiles
    bufs = (proj_a, proj_b)

    @pl.when((j - 1) % steps_per_seq == 0)
    def _():
        st_g[...] = jnp.zeros_like(st_g)
        st_g_bf[...] = jnp.zeros_like(st_g_bf)
        st_r[...] = jnp.zeros_like(st_r)

    def step(proj_w, proj_r):
        pending = []
        if proj_w is not None:
            m = mod_ref[0, 0]
            h = _rms_mod(x_ref[0], g_ref[0], _row(m, MOD_SC1), _row(m, MOD_SH1)).astype(BF16)

            def block(w0, out0, width):
                def emit():
                    proj_w[:, out0:out0 + width] = _dot(h, w_ref[0, :, w0:w0 + width])
                return emit

            for src, dst in zip(cast_src, cast_dst):
                dst[...] = src[...].astype(BF16)

            pending = ([block(c0, c0, PROJ_BLOCK) for c0 in range(0, GA_LO, PROJ_BLOCK)]
                       + [block(GA_LO, OFF_GA, LANES)]
                       + [block(W_AFTER + c0, GA_LO + c0, PROJ_BLOCK)
                          for c0 in range(0, PROJ_WIDTH - GA_LO, PROJ_BLOCK)])
        if proj_r is None:
            for job in pending:
                job()
        else:
            _attn_body(step_decay, pending, proj_r, *mix_in, y_ref, h2_ref, o_scr, st_g, st_g_bf, st_r)

    @pl.when(j == 0)
    def _():
        step(bufs[0], None)

    for parity in range(2):
        @pl.when(jnp.logical_and(jnp.logical_and(j > 0, j < last), j % 2 == parity))
        def _():
            step(bufs[parity], bufs[1 - parity])

    @pl.when(j == last)
    def _():
        step(None, bufs[(n_tiles - 1) % 2])


def _attn_body(step_decay, pending, proj_ref, cos_ref, sin_ref, walpha_ref, balpha_ref,
               glag_ref, gng_ref, gnb_ref, tri_ref, mfwd_ref, mbwd_ref, dret_ref,
               qdec_ref, kdec_ref, hmask_ref, xres_ref, modres_ref, wout_ref, gffn_ref,
               y_ref, h2_ref, o_scr, st_g, st_g_bf, st_r):
    r = ATTN_ROWS
    pending = list(pending)

    def interleave(n=1):
        for _ in range(min(n, len(pending) - TAIL_JOBS)):
            pending.pop(0)()

    z = _dot(proj_ref[:, OFF_GA:].astype(BF16), walpha_ref[0]) + balpha_ref[0]
    interleave(2)
    log_a = (jnp.minimum(z, 0.0) - jnp.log(1.0 + jnp.exp(-jnp.abs(z)))) * (1.0 / GLA_TAU)
    hi = log_a.astype(BF16)
    r1 = log_a - hi.astype(F32)
    mid = r1.astype(BF16)
    lo = (r1 - mid.astype(F32)).astype(BF16)
    tri = tri_ref[...]
    b = _dot(tri, hi) + _dot(tri, mid) + _dot(tri, lo)
    interleave(2)
    b3 = b.reshape(ATTN_CHUNKS, CHUNK, GLA_QK)
    b_last = b3[:, CHUNK - 1:CHUNK, :]
    k_upd_scale = jnp.exp(b_last - b3).reshape(r, GLA_QK)
    eb = jnp.exp(b)
    enb = jnp.exp(-b)
    q = proj_ref[:,OFF_GQ:OFF_GQ + GLA_QK] * (GLA_DK ** -0.5)
    k = proj_ref[:,OFF_GK:OFF_GK + GLA_QK]
    v_bf = proj_ref[:,OFF_GV:OFF_GV + GLA_V].astype(BF16)
    q_f = q * eb
    q_b = q * enb
    k_f = (k * enb).astype(BF16)
    k_b = (k * eb).astype(BF16)
    k_u = (k * k_upd_scale).astype(BF16)
    q_f_bf = q_f.astype(BF16)

    b_t = b.T
    heads_per_group = LANES // GLA_DK
    inter = []
    for g in range(ATTN_CHUNKS):
        rows = slice(g * CHUNK, (g + 1) * CHUNK)
        inter.append(_dot(q_f_bf[rows], st_g_bf[...]))
        decay = jnp.exp(b_t[:, (g + 1) * CHUNK - 1:(g + 1) * CHUNK])
        for grp in range(GLA_QK // LANES):
            v_lanes = slice(grp * heads_per_group * GLA_DV, (grp + 1) * heads_per_group * GLA_DV)
            upd = _dot_tn(k_u[rows, grp * LANES:(grp + 1) * LANES], v_bf[rows, v_lanes])
            for i in range(heads_per_group):
                h = grp * heads_per_group + i
                keys = slice(h * GLA_DK, (h + 1) * GLA_DK)
                st = (st_g[h] * decay[keys]
                      + upd[i * GLA_DK:(i + 1) * GLA_DK, i * GLA_DV:(i + 1) * GLA_DV])
                st_g[h] = st
                st_g_bf[keys, h * GLA_DV:(h + 1) * GLA_DV] = st.astype(BF16)
        interleave()
    o_inter = jnp.concatenate(inter, axis=0)

    m_fwd = mfwd_ref[...] > 0.5
    m_bwd = mbwd_ref[...] > 0.5
    for h in range(GLA_HEADS):
        hm = hmask_ref[h]
        s_f = _dot_nt((q_f * hm).astype(BF16), k_f)
        s_b = _dot_nt((q_b * hm).astype(BF16), k_b)
        sc = jnp.where(m_fwd, s_f, jnp.where(m_bwd, s_b, 0.0)).astype(BF16)
        lanes = slice(h * GLA_DV, (h + 1) * GLA_DV)
        o_h = _dot(sc, v_bf[:, lanes]) + o_inter[:, lanes]
        interleave()
        o_h = o_h * lax.rsqrt(jnp.mean(o_h * o_h, axis=-1, keepdims=True) + EPS)
        gate = proj_ref[:,OFF_GR + h * GLA_DV:OFF_GR + (h + 1) * GLA_DV]
        o_h = o_h * glag_ref[0, :, lanes] * (gate * _sigmoid(gate))
        o_scr[:, lanes] = o_h.astype(BF16)

    cos = cos_ref[0]
    sin = sin_ref[0]
    for h in range(RET_HEADS):
        lanes = slice(h * RET_DK, (h + 1) * RET_DK)
        qh = proj_ref[:,OFF_RQ + h * RET_DK:OFF_RQ + (h + 1) * RET_DK]
        kh = proj_ref[:,OFF_RK + h * RET_DK:OFF_RK + (h + 1) * RET_DK]
        vh = proj_ref[:,OFF_RV + h * RET_DV:OFF_RV + (h + 1) * RET_DV].astype(BF16)
        qh = (qh * cos + pltpu.roll(qh, RET_DK // 2, axis=1) * sin) * (RET_DK ** -0.5)
        kh = kh * cos + pltpu.roll(kh, RET_DK // 2, axis=1) * sin
        s = _dot_nt(qh.astype(BF16), kh.astype(BF16)) * dret_ref[h]
        st = st_r[h]
        o_h = _dot(s.astype(BF16), vh) + _dot((qh * qdec_ref[:, lanes]).astype(BF16), st.astype(BF16))
        st_r[h] = st * step_decay[h] + _dot_tn((kh * kdec_ref[:, lanes]).astype(BF16), vh)
        interleave()
        mu =jnp.mean(o_h, axis=-1, keepdims=True)
        d = o_h - mu
        var = jnp.mean(d * d, axis=-1, keepdims=True)
        o_h = d * lax.rsqrt(var + EPS) * gng_ref[0, :, lanes] + gnb_ref[0, :, lanes]
        gate = proj_ref[:,OFF_RG + h * RET_DV:OFF_RG + (h + 1) * RET_DV]
        o_h = o_h * (gate * _sigmoid(gate))
        o_scr[:, GLA_V + h * RET_DV:GLA_V + (h + 1) * RET_DV] = o_h.astype(BF16)
    interleave(len(pending))
    m = modres_ref[0, 0]
    y = xres_ref[0] + _row(m, MOD_G1) * _dot(o_scr[...], wout_ref[0])
    for job in pending:
        job()
    y_ref[0] = y
    h2_ref[0] = _rms_mod(y, gffn_ref[0], _row(m, MOD_SC2), _row(m, MOD_SH2)).astype(h2_ref.dtype)


def _mixer(layer, x, mod_all, g_all, w_proj_all, cos, sin, w_alpha_all, b_alpha_all,
           gla_g_all, gn_g_all, gn_b_all, w_out_all, g_ffn_all, h2_dtype, cast_jobs):
    bsz, seq, _ = x.shape
    r = ATTN_ROWS
    steps_per_seq = seq // r
    n_tiles = bsz * steps_per_seq
    c = _attn_consts()
    const2 = lambda shape: pl.BlockSpec(shape, lambda j: (0,) * len(shape))
    per_layer = lambda a: _layer_spec(a.shape, layer)
    proj_tile = lambda j: jnp.minimum(j, n_tiles - 1)
    mix_tile = lambda j: jnp.maximum(j - 1, 0)
    mix_blk = lambda w: pl.BlockSpec((1, r, w), lambda j: (mix_tile(j), 0, 0))

    def chunk_spec(a, chunks, first):
        return pl.BlockSpec((1,) + a.shape[1:], lambda j: (first + jnp.minimum(j, chunks - 1), 0, 0))

    cast_in = [chunk_spec(a, chunks, li * chunks) for a, chunks, li in cast_jobs]
    cast_out = [chunk_spec(a, chunks, 0) for a, chunks, li in cast_jobs]
    cast_shapes = [jax.ShapeDtypeStruct((chunks,) + a.shape[1:], BF16) for a, chunks, li in cast_jobs]
    out = pl.pallas_call(
        functools.partial(_mixer_kernel, c["step_decay"], steps_per_seq, n_tiles, len(cast_jobs)),
        grid=(n_tiles + 1,),
        in_specs=[
            pl.BlockSpec((1, r, D_MODEL), lambda j: (proj_tile(j), 0, 0)),
            pl.BlockSpec((1, 1, 6, D_MODEL), lambda j: (layer, proj_tile(j) // steps_per_seq, 0, 0)),
            per_layer(g_all), per_layer(w_proj_all),
            mix_blk(RET_DK), mix_blk(RET_DK),
            per_layer(w_alpha_all), per_layer(b_alpha_all),
            per_layer(gla_g_all), per_layer(gn_g_all), per_layer(gn_b_all),
            const2((r, r)), const2((r, r)), const2((r, r)), const2((RET_HEADS, r, r)),
            const2((r, RET_QK)), const2((r, RET_QK)),
            const2((GLA_HEADS, 1, GLA_QK)),
            mix_blk(D_MODEL),
            pl.BlockSpec((1, 1, 6, D_MODEL), lambda j: (layer, mix_tile(j) // steps_per_seq, 0, 0)),
            per_layer(w_out_all), per_layer(g_ffn_all),
        ] + cast_in,
        out_specs=[mix_blk(D_MODEL), mix_blk(D_MODEL)] + cast_out,
        out_shape=[jax.ShapeDtypeStruct((n_tiles, r, D_MODEL), F32),
                   jax.ShapeDtypeStruct((n_tiles, r, D_MODEL), h2_dtype)] + cast_shapes,
        scratch_shapes=[pltpu.VMEM((r, PROJ_WIDTH + LANES), F32),
                        pltpu.VMEM((r, PROJ_WIDTH + LANES), F32),
                        pltpu.VMEM((r, MIX_WIDTH), BF16),
                        pltpu.VMEM((GLA_HEADS, GLA_DK, GLA_DV), F32),
                        pltpu.VMEM((GLA_QK, GLA_V), BF16),
                        pltpu.VMEM((RET_HEADS, RET_DK, RET_DV), F32)],
        compiler_params=pltpu.CompilerParams(
            dimension_semantics=("arbitrary",), vmem_limit_bytes=VMEM_LIMIT_BYTES),
        name="mixer",
    )(x.reshape(n_tiles, r, D_MODEL), mod_all, g_all, w_proj_all,
      cos.reshape(n_tiles, r, RET_DK), sin.reshape(n_tiles, r, RET_DK),
      w_alpha_all, b_alpha_all, gla_g_all, gn_g_all, gn_b_all,
      jnp.asarray(c["tri"], BF16), jnp.asarray(c["m_fwd"]), jnp.asarray(c["m_bwd"]),
      jnp.asarray(c["d_ret"]), jnp.asarray(c["qdec"]), jnp.asarray(c["kdec"]),
      jnp.asarray(c["hmask"]),
      x.reshape(n_tiles, r, D_MODEL), mod_all, w_out_all, g_ffn_all, *[a for a, _, _ in cast_jobs])
    return out[0].reshape(x.shape), out[1].reshape(x.shape), out[2:]


ROUTE_ROWS = 512
EXPERT_ROWS = 512
SEG_ALIGN = 8
LOCAL_ROWS = -(-(2 * ROUTE_ROWS + N_EXPERTS * (SEG_ALIGN - 1)) // LANES) * LANES
PIECE_BITS = (max(ROUTE_ROWS, EXPERT_ROWS) // SEG_ALIGN).bit_length()
INFO_POS, INFO_PROB = 0, 2


def _route_kernel(hin_ref, rw_ref, tril_ref, upper_ref, h_ref, info_ref, cnt_ref):
    h = hin_ref[...]
    h_hi = h.astype(BF16)
    h_ref[...] = h_hi
    h_lo = (h - h_hi.astype(F32)).astype(BF16)
    hh = _dot(h_hi, rw_ref[...])
    logits = hh[:, :LANES] + hh[:, LANES:] + _dot(h_lo, rw_ref[:, :LANES])
    lane = lax.broadcasted_iota(jnp.int32, logits.shape, 1)
    neg = jnp.float32(-jnp.inf)
    lg = jnp.where(lane < N_EXPERTS, logits, neg)
    m1 = jnp.max(lg, axis=-1, keepdims=True)
    i1 = jnp.min(jnp.where(lg == m1, lane, LANES), axis=-1, keepdims=True)
    lg2 = jnp.where(lane == i1, neg, lg)
    m2 = jnp.max(lg2, axis=-1, keepdims=True)
    i2 = jnp.min(jnp.where(lg2 == m2, lane, LANES), axis=-1, keepdims=True)
    e2 = jnp.exp(m2 - m1)
    p1 = 1.0 / (1.0 + e2)
    p2 = e2 * p1
    sel1 = lane == i1
    sel2 = lane == i2
    onehot = jnp.where(sel1, 1.0, 0.0) + jnp.where(sel2, 1.0, 0.0)
    incl = _dot(tril_ref[...], onehot.astype(BF16))
    counts = incl[ROUTE_ROWS - 1:ROUTE_ROWS]
    seg_units = jnp.floor((counts + (SEG_ALIGN - 1)) * (1.0 / SEG_ALIGN))
    seg_units8 = jnp.broadcast_to(seg_units, (8, LANES)).astype(BF16)
    seg_off = _dot(seg_units8, upper_ref[...])[0:1] * SEG_ALIGN
    pos = incl - onehot + seg_off
    pos1 = jnp.sum(jnp.where(sel1, pos, 0.0), axis=-1, keepdims=True)
    pos2 = jnp.sum(jnp.where(sel2, pos, 0.0), axis=-1, keepdims=True)
    cnt_ref[...] = jnp.broadcast_to(counts, (8, LANES))
    rec = jnp.zeros(logits.shape, F32)
    for lane_id, val in ((INFO_POS, pos1), (INFO_POS + 1, pos2), (INFO_PROB, p1), (INFO_PROB + 1, p2)):
        rec = jnp.where(lane == lane_id, val, rec)
    info_ref[...] = rec


def _route(h_f32, router_w):
    n = h_f32.shape[0]
    tm = ROUTE_ROWS
    w_hi = router_w.astype(BF16)
    w_lo = (router_w - w_hi.astype(F32)).astype(BF16)
    pad = ((0, 0), (0, LANES - N_EXPERTS))
    rw = jnp.concatenate([jnp.pad(w_hi, pad), jnp.pad(w_lo, pad)], axis=1)
    tril = jnp.asarray(np.tril(np.ones((tm, tm), np.float32)), BF16)
    upper = jnp.asarray(np.triu(np.ones((LANES, LANES), np.float32), 1), BF16)
    return pl.pallas_call(
        _route_kernel,
        grid=(n // tm,),
        in_specs=[pl.BlockSpec((tm, D_MODEL), lambda i: (i, 0)),
                  pl.BlockSpec((D_MODEL, 2 * LANES), lambda i: (0, 0)),
                  pl.BlockSpec((tm, tm), lambda i: (0, 0)),
                  pl.BlockSpec((LANES, LANES), lambda i: (0, 0))],
        out_specs=[pl.BlockSpec((tm, D_MODEL), lambda i: (i, 0)),
                   pl.BlockSpec((tm, LANES), lambda i: (i, 0)),
                   pl.BlockSpec((8, LANES), lambda i: (i, 0))],
        out_shape=[jax.ShapeDtypeStruct((n, D_MODEL), BF16),
                   jax.ShapeDtypeStruct((n, LANES), F32),
                   jax.ShapeDtypeStruct((n // tm * 8, LANES), F32)],
        compiler_params=pltpu.CompilerParams(
            dimension_semantics=("arbitrary",), vmem_limit_bytes=VMEM_LIMIT_BYTES),
        name="route",
    )(h_f32, rw, tril, upper)


def _rows(ref, row, n_rows):
    return ref.at[pl.ds(pl.multiple_of(row, SEG_ALIGN), n_rows)]


def _for_each_piece(n_units, fn):
    for bit in reversed(range(PIECE_BITS)):
        covered = (n_units >> (bit + 1)) << (bit + 1)

        @pl.when(((n_units >> bit) & 1) == 1)
        def _():
            fn(covered * SEG_ALIGN, SEG_ALIGN << bit)


def _for_each_segment_piece(seg_ref, n_seg, tile, fn):
    for e in range(N_EXPERTS):
        s = tile * N_EXPERTS + e
        sorted_row = seg_ref[s]
        local_row = seg_ref[n_seg + s]
        _for_each_piece(seg_ref[2 * n_seg + s],
                        lambda first, n_rows: fn(local_row + first, sorted_row + first, n_rows, e % 2))


def _selection(info, coeff1, coeff2):
    lane = lax.broadcasted_iota(jnp.int32, (info.shape[0], LOCAL_ROWS), 1)
    pos1 = info[:, INFO_POS:INFO_POS + 1].astype(jnp.int32)
    pos2 = info[:, INFO_POS + 1:INFO_POS + 2].astype(jnp.int32)
    return jnp.where(lane == pos1, coeff1, jnp.where(lane == pos2, coeff2, 0.0)).astype(BF16)


def _dispatch_kernel(n_seg, seg_ref, pad_ref, h_ref, info_ref, xs_ref, sorted_scr, zero_scr, sems, pad_sem):
    i = pl.program_id(0)
    n_steps = pl.num_programs(0)
    slot = i % 2

    def move(tile, buf_slot, op):
        def piece(local_row, sorted_row, n_rows, dma_priority):
            cp = pltpu.make_async_copy(_rows(sorted_scr.at[buf_slot], local_row, n_rows),
                                       _rows(xs_ref, sorted_row, n_rows), sems.at[buf_slot])
            op(cp, dma_priority)
        _for_each_segment_piece(seg_ref, n_seg, tile, piece)

    @pl.when(i == 0)
    def _():
        zero_scr[...] = jnp.zeros_like(zero_scr)
        for op in (lambda cp: cp.start(), lambda cp: cp.wait()):
            for e in range(N_EXPERTS):
                lo = pad_ref[e]
                _for_each_piece(pad_ref[N_EXPERTS + e],
                                lambda first, n_rows: op(pltpu.make_async_copy(
                                    zero_scr.at[pl.ds(0, n_rows)], _rows(xs_ref, lo + first, n_rows),
                                    pad_sem)))

        def tile_copy(j):
            return pltpu.make_async_copy(zero_scr, _rows(xs_ref, j * EXPERT_ROWS, EXPERT_ROWS), pad_sem)

        def fill_tile(j, carry):
            tile_copy(j).start()
            return carry

        def drain_tile(j, carry):
            tile_copy(j).wait()
            return carry

        n_tiles = xs_ref.shape[0] // EXPERT_ROWS
        lax.fori_loop(pad_ref[2 * N_EXPERTS], n_tiles, fill_tile, 0)
        lax.fori_loop(pad_ref[2 * N_EXPERTS], n_tiles, drain_tile, 0)

    @pl.when(i >= 2)
    def _():
        move(i - 2, slot, lambda cp, _: cp.wait())

    sel = _selection(info_ref[...], 1.0, 1.0)
    sorted_scr[slot] = _dot_tn(sel, h_ref[...])
    move(i, slot, lambda cp, priority: cp.start(priority=priority))

    @pl.when(i == n_steps - 1)
    def _():
        @pl.when(i >= 1)
        def _():
            move(i - 1, 1 - slot, lambda cp, _: cp.wait())
        move(i, slot, lambda cp, _: cp.wait())


def _dispatch(h2, info, segs, pads, n_sorted):
    n = h2.shape[0]
    tm = ROUTE_ROWS
    n_seg = n // tm * N_EXPERTS
    return pl.pallas_call(
        functools.partial(_dispatch_kernel, n_seg),
        grid_spec=pltpu.PrefetchScalarGridSpec(
            num_scalar_prefetch=2,
            grid=(n // tm,),
            in_specs=[pl.BlockSpec((tm, D_MODEL), lambda i, s, p: (i, 0)),
                      pl.BlockSpec((tm, LANES), lambda i, s, p: (i, 0))],
            out_specs=pl.BlockSpec(memory_space=pl.ANY),
            scratch_shapes=[pltpu.VMEM((2, LOCAL_ROWS, D_MODEL), F32),
                            pltpu.VMEM((EXPERT_ROWS, D_MODEL), F32),
                            pltpu.SemaphoreType.DMA((2,)), pltpu.SemaphoreType.DMA(())]),
        out_shape=jax.ShapeDtypeStruct((n_sorted, D_MODEL), F32),
        compiler_params=pltpu.CompilerParams(
            dimension_semantics=("arbitrary",), vmem_limit_bytes=VMEM_LIMIT_BYTES),
        name="dispatch",
    )(segs, pads, h2, info)


def _expert_kernel(tile_blk, tile_e, n_act, x_ref, w1_ref, w3_ref, w2_ref, y_ref):
    active = pl.program_id(0) < n_act[0]

    @pl.when(active)
    def _():
        h = x_ref[...].astype(BF16)
        a = _dot(h, w1_ref[0])
        u = (a * _sigmoid(a) * _dot(h, w3_ref[0])).astype(BF16)
        y_ref[...] = _dot(u, w2_ref[0])

    @pl.when(jnp.logical_not(active))
    def _():
        y_ref[...] = jnp.zeros_like(y_ref)


def _experts(xs, tile_blk, tile_e, n_act, w1, w3, w2):
    n_sorted = xs.shape[0]
    tm = EXPERT_ROWS
    f = w1.shape[-1]
    return pl.pallas_call(
        _expert_kernel,
        grid_spec=pltpu.PrefetchScalarGridSpec(
            num_scalar_prefetch=3,
            grid=(n_sorted // tm,),
            in_specs=[pl.BlockSpec((tm, D_MODEL), lambda i, b, e, n: (b[i], 0)),
                      pl.BlockSpec((1, D_MODEL, f), lambda i, b, e, n: (e[i], 0, 0)),
                      pl.BlockSpec((1, D_MODEL, f), lambda i, b, e, n: (e[i], 0, 0)),
                      pl.BlockSpec((1, f, D_MODEL), lambda i, b, e, n: (e[i], 0, 0))],
            out_specs=pl.BlockSpec((tm, D_MODEL), lambda i, b, e, n: (i, 0))),
        out_shape=jax.ShapeDtypeStruct((n_sorted, D_MODEL), F32),
        compiler_params=pltpu.CompilerParams(
            dimension_semantics=("arbitrary",), vmem_limit_bytes=VMEM_LIMIT_BYTES),
        name="experts",
    )(tile_blk, tile_e, n_act, xs, w1, w3, w2)


def _combine_kernel(final_norm, n_seg, seg_ref, x_ref, mod_ref, info_ref, fg_ref, ys_ref, out_ref,
                    ybuf, sems):
    i = pl.program_id(0)
    n_steps = pl.num_programs(0)
    slot = i % 2

    def move(tile, buf_slot, op):
        def piece(local_row, sorted_row, n_rows, dma_priority):
            cp = pltpu.make_async_copy(_rows(ys_ref, sorted_row, n_rows),
                                       _rows(ybuf.at[buf_slot], local_row, n_rows), sems.at[buf_slot])
            op(cp, dma_priority)
        _for_each_segment_piece(seg_ref, n_seg, tile, piece)

    @pl.when(i == 0)
    def _():
        ybuf[...] = jnp.zeros_like(ybuf)
        move(0, 0, lambda cp, priority: cp.start(priority=priority))

    @pl.when(i + 1 < n_steps)
    def _():
        move(i + 1, 1 - slot, lambda cp, priority: cp.start(priority=priority))

    move(i, slot, lambda cp, _: cp.wait())
    info = info_ref[...]
    sel = _selection(info, info[:, INFO_PROB:INFO_PROB + 1], info[:, INFO_PROB + 1:INFO_PROB + 2])
    y = _dot(sel, ybuf[slot].astype(BF16))
    out = x_ref[...] + _row(mod_ref[0, 0], MOD_G2) * y
    if final_norm:
        ms = jnp.mean(out * out, axis=-1, keepdims=True)
        out = out * lax.rsqrt(ms + EPS) * fg_ref[...]
    out_ref[...] = out


def _combine(layer, x2, mod_all, info, ys, segs, final_g, final_norm, rows_per_batch):
    n = x2.shape[0]
    tm = ROUTE_ROWS
    per_b = rows_per_batch // tm
    n_seg = n // tm * N_EXPERTS
    return pl.pallas_call(
        functools.partial(_combine_kernel, final_norm, n_seg),
        grid_spec=pltpu.PrefetchScalarGridSpec(
            num_scalar_prefetch=1,
            grid=(n // tm,),
            in_specs=[pl.BlockSpec((tm, D_MODEL), lambda i, s: (i, 0)),
                      pl.BlockSpec((1, 1, 6, D_MODEL), lambda i, s: (layer, i // per_b, 0, 0)),
                      pl.BlockSpec((tm, LANES), lambda i, s: (i, 0)),
                      pl.BlockSpec((1, D_MODEL), lambda i, s: (0, 0)),
                      pl.BlockSpec(memory_space=pl.ANY)],
            out_specs=pl.BlockSpec((tm, D_MODEL), lambda i, s: (i, 0)),
            scratch_shapes=[pltpu.VMEM((2, LOCAL_ROWS, D_MODEL), F32),
                            pltpu.SemaphoreType.DMA((2,))]),
        out_shape=jax.ShapeDtypeStruct(x2.shape, F32),
        compiler_params=pltpu.CompilerParams(
            dimension_semantics=("arbitrary",), vmem_limit_bytes=VMEM_LIMIT_BYTES),
        name="combine",
    )(segs, x2, mod_all, info, final_g, ys)


def _moe(layer, x, h_f32, mod_all, router_w, w1, w3, w2, final_g, final_norm):
    bsz, seq, _ = x.shape
    n = bsz * seq
    tm = EXPERT_ROWS
    n_tok_tiles = n // ROUTE_ROWS
    x2 = x.reshape(n, D_MODEL)
    h2, info, cnt = _route(h_f32.reshape(n, D_MODEL), router_w)
    counts = cnt.reshape(n_tok_tiles, 8, LANES)[:, 0, :N_EXPERTS].astype(jnp.int32)
    seg_rows = (counts + (SEG_ALIGN - 1)) // SEG_ALIGN * SEG_ALIGN
    group_rows = jnp.sum(seg_rows, axis=0)
    tiles = (group_rows + (tm - 1)) // tm
    tile_end = jnp.cumsum(tiles)
    group_off = (tile_end - tiles) * tm
    seg_sorted = group_off[None, :] + jnp.cumsum(seg_rows, axis=0) - seg_rows
    seg_local = jnp.cumsum(seg_rows, axis=1) - seg_rows
    segs = jnp.concatenate([seg_sorted.reshape(-1), seg_local.reshape(-1),
                            (seg_rows // SEG_ALIGN).reshape(-1)]).astype(jnp.int32)
    max_rows = 2 * n + n_tok_tiles * N_EXPERTS * (SEG_ALIGN - 1)
    n_tiles = -(-max_rows // tm) + N_EXPERTS
    n_act = tile_end[-1]
    t = jnp.minimum(jnp.arange(n_tiles, dtype=jnp.int32), n_act - 1)
    tile_e = jnp.sum(t[:, None] >= tile_end[None, :], axis=1).astype(jnp.int32)
    pad_lo = group_off + group_rows
    pads = jnp.concatenate([pad_lo, (tile_end * tm - pad_lo) // SEG_ALIGN,
                            n_act[None]]).astype(jnp.int32)
    xs = _dispatch(h2, info, segs, pads, n_tiles * tm)
    ys = _experts(xs, t, tile_e, n_act.reshape(1).astype(jnp.int32), w1, w3, w2)
    out = _combine(layer, x2, mod_all, info, ys, segs, final_g, final_norm, seq)
    return out.reshape(x.shape)


def _ffn_kernel(final_norm, x_ref, h_ref, mod_ref, w1_ref, w3_ref, w2_ref, fg_ref, y_ref):
    h = h_ref[0]
    a = _dot(h, w1_ref[...])
    u = (a * _sigmoid(a) * _dot(h, w3_ref[...])).astype(BF16)
    out = x_ref[0] + _row(mod_ref[0, 0], MOD_G2) * _dot(u, w2_ref[...])
    if final_norm:
        ms = jnp.mean(out * out, axis=-1, keepdims=True)
        out = out * lax.rsqrt(ms + EPS) * fg_ref[...]
    y_ref[0] = out


def _ffn(layer, x, h2, mod_all, w1, w3, w2, final_g, final_norm):
    bsz, seq, _ = x.shape
    f = w1.shape[-1]
    tm = 512
    row = pl.BlockSpec((1, tm, D_MODEL), lambda b, i: (b, i, 0))
    resident = lambda shape: pl.BlockSpec(shape, lambda b, i: (0, 0), pipeline_mode=pl.Buffered(1))
    return pl.pallas_call(
        functools.partial(_ffn_kernel, final_norm),
        grid=(bsz, seq // tm),
        in_specs=[row, row,
                  pl.BlockSpec((1, 1, 6, D_MODEL), lambda b, i: (layer, b, 0, 0)),
                  resident((D_MODEL, f)), resident((D_MODEL, f)), resident((f, D_MODEL)),
                  pl.BlockSpec((1, D_MODEL), lambda b, i: (0, 0))],
        out_specs=row,
        out_shape=jax.ShapeDtypeStruct(x.shape, F32),
        compiler_params=pltpu.CompilerParams(
            dimension_semantics=("arbitrary", "arbitrary"),
            vmem_limit_bytes=VMEM_LIMIT_BYTES),
        name="ffn",
    )(x, h2, mod_all, w1, w3, w2, final_g)


def kernel(x, c, positions, ada_w, ada_b, norm_mix_g, norm_ffn_g, w_in, gla_w_alpha, gla_b_alpha,
           gla_norm_g, ret_gn_g, ret_gn_b, w_out, ffn_w1, ffn_w3, ffn_w2, router_w, moe_w1,
           moe_w3, moe_w2, final_g):
    bsz = x.shape[0]
    mod_all = _ada_mod(c, ada_w, ada_b).reshape(DEPTH, bsz, 6, D_MODEL)
    cos, sin = _rope_tables(positions)
    lane_pad = PROJ_WIDTH + LANES - w_in.shape[-1]
    w_proj_all = _prep_w_proj(jnp.pad(w_in, ((0, 0), (0, 0), (0, lane_pad))).astype(BF16))
    w_alpha_all = jnp.pad(gla_w_alpha, ((0, 0), (0, LANES - GLA_RANK), (0, 0))).astype(BF16)
    vec = lambda a: a.reshape(a.shape[0], 1, a.shape[1])
    b_alpha_all, gla_g_all, gn_g_all, gn_b_all = map(vec, (gla_b_alpha, gla_norm_g, ret_gn_g, ret_gn_b))
    g_mix_all, g_ffn_all = vec(norm_mix_g), vec(norm_ffn_g)
    w_out_all = w_out.astype(BF16)
    fg = final_g.reshape(1, D_MODEL)
    d_ff = ffn_w1.shape[-1]
    d_fe = moe_w1.shape[-1]
    n_dense, n_moe = ffn_w1.shape[0], moe_w1.shape[0]
    dense_src = (ffn_w1.reshape(n_dense * FFN_CHUNKS, D_MODEL // FFN_CHUNKS, d_ff),
                 ffn_w3.reshape(n_dense * FFN_CHUNKS, D_MODEL // FFN_CHUNKS, d_ff),
                 ffn_w2.reshape(n_dense * FFN_CHUNKS, d_ff // FFN_CHUNKS, D_MODEL))
    moe_chunks = N_EXPERTS * MOE_CHUNKS_PER_EXPERT
    moe_src = (moe_w1.reshape(n_moe * moe_chunks, D_MODEL // MOE_CHUNKS_PER_EXPERT, d_fe),
               moe_w3.reshape(n_moe * moe_chunks, D_MODEL // MOE_CHUNKS_PER_EXPERT, d_fe),
               moe_w2.reshape(n_moe * moe_chunks, d_fe // MOE_CHUNKS_PER_EXPERT, D_MODEL))
    for layer in range(DEPTH):
        li = layer // 2
        dense = layer % 2 == 0
        jobs = [(a, FFN_CHUNKS if dense else moe_chunks, li) for a in (dense_src if dense else moe_src)]
        x, h2, (w1, w3, w2) = _mixer(layer, x, mod_all, g_mix_all, w_proj_all, cos, sin,
                                     w_alpha_all, b_alpha_all, gla_g_all, gn_g_all, gn_b_all, w_out_all,
                                     g_ffn_all, BF16 if dense else F32, jobs)
        last = layer == DEPTH - 1
        if dense:
            x = _ffn(layer, x, h2, mod_all, w1.reshape(D_MODEL, d_ff), w3.reshape(D_MODEL, d_ff),
                     w2.reshape(d_ff, D_MODEL), fg, last)
        else:
            x = _moe(layer, x, h2, mod_all, router_w[li], w1.reshape(N_EXPERTS, D_MODEL, d_fe),
                     w3.reshape(N_EXPERTS, D_MODEL, d_fe), w2.reshape(N_EXPERTS, d_fe, D_MODEL), fg, last)
    return x
```
